```python
import math
import jax, jax.numpy as jnp
from jax import lax
import numpy as np

D_MODEL = 1024
BATCH = 16
SEQ = 4096
DEPTH = 4

N_MIXERS = 3
GRID_W = 64
DEEPNORM_ALPHA = (2 * DEPTH) ** 0.25
DEEPNORM_BETA = (8 * DEPTH) ** -0.25
LN_EPS = 1e-5

GLA_HEADS = 4
GLA_KEY_DIM = D_MODEL // 2
GLA_VAL_DIM = D_MODEL
GLA_HEAD_K = GLA_KEY_DIM // GLA_HEADS
GLA_HEAD_V = GLA_VAL_DIM // GLA_HEADS
GLA_GATE_RANK = 16
GLA_TAU = 16.0
GLA_CHUNK = 64
GLA_NORM_EPS = 1e-6
GLA_IN_WIDTH = 2 * GLA_KEY_DIM + 2 * GLA_VAL_DIM + 2 * GLA_GATE_RANK

HYENA_ORDER = 2
HYENA_POS_EMB = 33
HYENA_FILTER_HIDDEN = 64
HYENA_DECAY_TARGET = 1e-2
HYENA_FAST_DECAY = 0.3
HYENA_SLOW_DECAY = 1.5
HYENA_WINDOW_SHIFT = 0.05
HYENA_FILTER_INIT = 0.05

NA_HEAD_DIM = 32
NA_HEADS = D_MODEL // NA_HEAD_DIM
NA_WIN_ROWS = 8
NA_WIN_COLS = 16
NA_QUERY_COLS = 16
NA_BAND_COLS = NA_WIN_COLS + NA_QUERY_COLS

N_EXPERTS = 32
TOP_K = 4
MOE_D_FF = D_MODEL
SWIGLU_ALPHA = 1.702
SWIGLU_LIMIT = 7.0

N_GLA_LAYERS = len(range(0, DEPTH, N_MIXERS))
N_HYENA_LAYERS = len(range(1, DEPTH, N_MIXERS))
N_NA_LAYERS = len(range(2, DEPTH, N_MIXERS))

kernel_name = "hybrid_gla_hyena_natten_moe_encoder"


def _layer_norm(x, g, b):
    xf = x.astype(jnp.float32)
    mu = jnp.mean(xf, axis=-1, keepdims=True)
    var = jnp.mean(jnp.square(xf - mu), axis=-1, keepdims=True)
    return ((xf - mu) * lax.rsqrt(var + LN_EPS)).astype(x.dtype) * g + b


def _gla_scan(q, k, v, log_a, strict):
    B, L, H, dk = q.shape
    dv = v.shape[-1]
    C = GLA_CHUNK
    N = L // C

    def chunks(t):
        return t.astype(jnp.float32).reshape(B, N, C, H, t.shape[-1]).transpose(1, 0, 3, 2, 4)

    qc, kc, vc = chunks(q), chunks(k), chunks(v)
    bc = jnp.cumsum(chunks(log_a), axis=3)
    mask = jnp.tril(jnp.ones((C, C), dtype=bool), k=-1 if strict else 0)

    def step(state, inp):
        q_, k_, v_, b_ = inp
        diff = b_[:, :, :, None, :] - b_[:, :, None, :, :]
        decay = jnp.exp(jnp.where(mask[:, :, None], diff, -jnp.inf))
        scores = jnp.einsum('bhijd,bhjd->bhij', q_[:, :, :, None, :] * decay, k_)
        b_last = b_[:, :, -1:, :]
        out = (jnp.einsum('bhij,bhje->bhie', scores, v_)
               + jnp.einsum('bhid,bhde->bhie', q_ * jnp.exp(b_), state))
        state = (jnp.exp(b_last[:, :, 0, :])[..., None] * state
                 + jnp.einsum('bhjd,bhje->bhde', k_ * jnp.exp(b_last - b_), v_))
        return state, out

    s0 = jnp.zeros((B, H, dk, dv), jnp.float32)
    _, o = lax.scan(step, s0, (qc, kc, vc, bc))
    return o.transpose(1, 0, 3, 2, 4).reshape(B, L, H, dv)


def _gla_mixer(x, w_in, gate_w2, gate_b, norm_g, w_out):
    B, L, _ = x.shape
    H, dk, dv, r = GLA_HEADS, GLA_HEAD_K, GLA_HEAD_V, GLA_GATE_RANK
    proj = x @ w_in
    q, k, v, og, glow = jnp.split(
        proj, [GLA_KEY_DIM, 2 * GLA_KEY_DIM, 2 * GLA_KEY_DIM + GLA_VAL_DIM,
               2 * GLA_KEY_DIM + 2 * GLA_VAL_DIM], axis=-1)
    glow = glow.reshape(B, L, 2, r)
    logits = jnp.einsum('blnr,nrk->blnk', glow, gate_w2) + gate_b
    log_a = jax.nn.log_sigmoid(logits.astype(jnp.float32)) / GLA_TAU
    q = q.reshape(B, L, H, dk) * dk ** -0.5
    k = k.reshape(B, L, H, dk)
    v = v.reshape(B, L, H, dv)
    la_f = log_a[:, :, 0].reshape(B, L, H, dk)
    la_b = log_a[:, :, 1].reshape(B, L, H, dk)
    o_f = _gla_scan(q, k, v, la_f, strict=False)
    flip = lambda t: jnp.flip(t, axis=1)
    o_b = flip(_gla_scan(flip(q), flip(k), flip(v), flip(la_b), strict=True))
    o = o_f + o_b
    o = o * lax.rsqrt(jnp.mean(jnp.square(o), axis=-1, keepdims=True) + GLA_NORM_EPS)
    o = (o * norm_g.astype(jnp.float32)).astype(x.dtype).reshape(B, L, GLA_VAL_DIM)
    return (o * jax.nn.silu(og)) @ w_out


def _short_conv3(u, w, b):
    up = jnp.pad(u, ((0, 0), (1, 1), (0, 0)))
    return up[:, :-2] * w[0] + up[:, 1:-1] * w[1] + up[:, 2:] * w[2] + b


def _hyena_filters(L, w1, b1, sin_freq, w2, b2, w3):
    f32 = jnp.float32
    w1, b1, sin_freq, w2, b2, w3 = (a.astype(f32) for a in (w1, b1, sin_freq, w2, b2, w3))
    d = w3.shape[-1] // (2 * HYENA_ORDER)
    t = jnp.arange(L, dtype=f32)
    t_norm = t / max(L - 1, 1)
    bands = (HYENA_POS_EMB - 1) // 2
    freqs = jnp.linspace(1e-4, bands - 1, bands, dtype=f32)
    ang = (2.0 * math.pi / L) * t[:, None] * freqs[None, :]
    pe = jnp.concatenate([t_norm[:, None], jnp.cos(ang), -jnp.sin(ang)], axis=-1)
    h = jnp.sin(sin_freq[0] * (pe @ w1 + b1))
    h = jnp.sin(sin_freq[1] * (h @ w2 + b2))
    h = (h @ w3).reshape(L, HYENA_ORDER, 2, d)
    min_decay = math.log(HYENA_DECAY_TARGET) / HYENA_SLOW_DECAY
    max_decay = math.log(HYENA_DECAY_TARGET) / HYENA_FAST_DECAY
    deltas = jnp.abs(jnp.linspace(min_decay, max_decay, d, dtype=f32))
    window = jnp.exp(-t_norm[:, None] * deltas[None, :]) + HYENA_WINDOW_SHIFT
    h = h * window[:, None, None, :]
    h_fwd, h_bwd = h[:, :, 0], h[:, :, 1]
    two_sided = jnp.concatenate([h_fwd, jnp.zeros_like(h_fwd[:1]), h_bwd[:0:-1]], axis=0)
    return jnp.fft.rfft(two_sided, axis=0)


def _fft_long_conv(z, filt_f, bias):
    L = z.shape[1]
    Z = jnp.fft.rfft(z, n=2 * L, axis=1)
    y = jnp.fft.irfft(Z * filt_f[None], n=2 * L, axis=1)[:, :L]
    return y + z * bias.astype(jnp.float32)


def _hyena_mixer(x, w_in, b_in, conv_w, conv_b, ffn_w1, ffn_b1, sin_freq, ffn_w2, ffn_b2,
                 ffn_w3, filt_bias, w_out, b_out):
    L = x.shape[1]
    u = _short_conv3(x @ w_in + b_in, conv_w, conv_b)
    v, x1, x2 = jnp.split(u.astype(jnp.float32), 3, axis=-1)
    filt = _hyena_filters(L, ffn_w1, ffn_b1, sin_freq, ffn_w2, ffn_b2, ffn_w3)
    z = x1 * _fft_long_conv(v, filt[:, 0], filt_bias[0])
    z = x2 * _fft_long_conv(z, filt[:, 1], filt_bias[1])
    return z.astype(x.dtype) @ w_out + b_out


def _na_mixer(x, w_in, b_in, rpb, w_out, b_out):
    B, L, D = x.shape
    H, dh = NA_HEADS, D // NA_HEADS
    R = L // GRID_W
    KH = min(NA_WIN_ROWS, R)
    KW, QW, BW = NA_WIN_COLS, NA_QUERY_COLS, NA_BAND_COLS
    NCB = GRID_W // QW
    q, k, v = jnp.split(x @ w_in + b_in, 3, axis=-1)
    q = (q * dh ** -0.5).reshape(B, R, GRID_W, H, dh)
    k = k.reshape(B, R, GRID_W, H, dh)
    v = v.reshape(B, R, GRID_W, H, dh)
    rows = jnp.arange(R)
    row_start = jnp.clip(rows - KH // 2, 0, R - KH)
    key_rows = row_start[:, None] + jnp.arange(KH)[None, :]
    dr_idx = key_rows - rows[:, None] + (NA_WIN_ROWS - 1)
    q_cols = jnp.arange(NCB)[:, None] * QW + jnp.arange(QW)[None, :]
    col_start = jnp.clip(q_cols - KW // 2, 0, GRID_W - KW)
    band_start = jnp.clip(jnp.arange(NCB) * QW - KW // 2, 0, GRID_W - BW)
    band_cols = band_start[:, None] + jnp.arange(BW)[None, :]
    in_win = ((band_cols[:, None, :] >= col_start[..., None])
              & (band_cols[:, None, :] < col_start[..., None] + KW))
    dc_idx = jnp.clip(band_cols[:, None, :] - q_cols[..., None] + (KW - 1), 0, 2 * KW - 2)
    rpb32 = rpb.astype(jnp.float32)

    def row_block(inp):
        q_r, rows_r, dr_r = inp
        q_r = q_r.reshape(B, NCB, QW, H, dh)
        k_r = jnp.take(jnp.take(k, rows_r, axis=1), band_cols, axis=2)
        v_r = jnp.take(jnp.take(v, rows_r, axis=1), band_cols, axis=2)
        s = jnp.einsum('bcqhd,bacjhd->bhcqaj', q_r, k_r).astype(jnp.float32)
        bias = rpb32[:, dr_r][:, :, dc_idx].transpose(0, 2, 3, 1, 4)
        s = jnp.where(in_win[None, None, :, :, None, :], s + bias[None], -jnp.inf)
        p = jax.nn.softmax(s.reshape(B, H, NCB, QW, KH * BW), axis=-1)
        p = p.reshape(s.shape).astype(v.dtype)
        o = jnp.einsum('bhcqaj,bacjhd->bcqhd', p, v_r)
        return o.reshape(B, GRID_W, D)

    o = lax.map(row_block, (q.transpose(1, 0, 2, 3, 4), key_rows, dr_idx))
    o = o.transpose(1, 0, 2, 3).reshape(B, L, D)
    return o @ w_out + b_out


def _clamped_swiglu(h):
    glu, lin = jnp.split(h, 2, axis=-1)
    glu = jnp.minimum(glu, SWIGLU_LIMIT)
    lin = jnp.clip(lin, -SWIGLU_LIMIT, SWIGLU_LIMIT)
    return glu * jax.nn.sigmoid(SWIGLU_ALPHA * glu) * (lin + 1.0)


def _moe(x, router_w, router_b, w_gu, b_gu, w_down, b_down):
    B, L, D = x.shape
    xt = x.reshape(B * L, D)
    logits = (xt @ router_w + router_b).astype(jnp.float32)
    top_v, top_i = lax.top_k(logits, TOP_K)
    gates = jax.nn.softmax(top_v, axis=-1)
    combine = jnp.sum(jax.nn.one_hot(top_i, N_EXPERTS, dtype=jnp.float32) * gates[..., None],
                      axis=1).astype(x.dtype)
    out = jnp.zeros_like(xt)
    for e in range(N_EXPERTS):
        y = _clamped_swiglu(xt @ w_gu[e] + b_gu[e]) @ w_down[e] + b_down[e]
        out = out + combine[:, e:e + 1] * y
    return out.reshape(B, L, D)


def setup_inputs(seed: int = 0) -> dict:
    key = jax.random.key(seed)
    ks = iter(jax.random.split(key, 40))
    D, F = D_MODEL, MOE_D_FF
    nA, nB, nC = N_GLA_LAYERS, N_HYENA_LAYERS, N_NA_LAYERS

    def nrm(shape, scale):
        return scale * jax.random.normal(next(ks), shape, jnp.float32)

    return {
        "x": nrm((BATCH, SEQ, D), 1.0),
        "ln1_g": 1.0 + nrm((DEPTH, D), 0.02),
        "ln1_b": nrm((DEPTH, D), 0.02),
        "ln2_g": 1.0 + nrm((DEPTH, D), 0.02),
        "ln2_b": nrm((DEPTH, D), 0.02),
        "gla_w_in": nrm((nA, D, GLA_IN_WIDTH), D ** -0.5),
        "gla_gate_w2": nrm((nA, 2, GLA_GATE_RANK, GLA_KEY_DIM), GLA_GATE_RANK ** -0.5),
        "gla_gate_b": nrm((nA, 2, GLA_KEY_DIM), 0.1),
        "gla_norm_g": 1.0 + nrm((nA, GLA_HEAD_V), 0.02),
        "gla_w_out": nrm((nA, GLA_VAL_DIM, D), GLA_VAL_DIM ** -0.5 * DEEPNORM_BETA),
        "hy_w_in": nrm((nB, D, 3 * D), D ** -0.5),
        "hy_b_in": nrm((nB, 3 * D), 0.02),
        "hy_conv_w": nrm((nB, 3, 3 * D), 3 ** -0.5),
        "hy_conv_b": nrm((nB, 3 * D), 0.02),
        "hy_ffn_w1": nrm((nB, HYENA_POS_EMB, HYENA_FILTER_HIDDEN), HYENA_POS_EMB ** -0.5),
        "hy_ffn_b1": nrm((nB, HYENA_FILTER_HIDDEN), 0.5),
        "hy_sin_freq": 1.0 + nrm((nB, 2, HYENA_FILTER_HIDDEN), 0.02),
        "hy_ffn_w2": nrm((nB, HYENA_FILTER_HIDDEN, HYENA_FILTER_HIDDEN), HYENA_FILTER_HIDDEN ** -0.5),
        "hy_ffn_b2": nrm((nB, HYENA_FILTER_HIDDEN), 0.5),
        "hy_ffn_w3": nrm((nB, HYENA_FILTER_HIDDEN, 2 * HYENA_ORDER * D),
                          HYENA_FILTER_HIDDEN ** -0.5 * HYENA_FILTER_INIT),
        "hy_filt_bias": nrm((nB, HYENA_ORDER, D), 0.5),
        "hy_w_out": nrm((nB, D, D), D ** -0.5 * DEEPNORM_BETA),
        "hy_b_out": nrm((nB, D), 0.02),
        "na_w_in": nrm((nC, D, 3 * D), D ** -0.5),
        "na_b_in": nrm((nC, 3 * D), 0.02),
        "na_rpb": nrm((nC, NA_HEADS, 2 * NA_WIN_ROWS - 1, 2 * NA_WIN_COLS - 1), 0.02),
        "na_w_out": nrm((nC, D, D), D ** -0.5 * DEEPNORM_BETA),
        "na_b_out": nrm((nC, D), 0.02),
        "moe_router_w": nrm((DEPTH, D, N_EXPERTS), D ** -0.5),
        "moe_router_b": nrm((DEPTH, N_EXPERTS), 0.01),
        "moe_w_gu": nrm((DEPTH, N_EXPERTS, D, 2 * F), D ** -0.5),
        "moe_b_gu": nrm((DEPTH, N_EXPERTS, 2 * F), 0.01),
        "moe_w_down": nrm((DEPTH, N_EXPERTS, F, D), F ** -0.5 * DEEPNORM_BETA),
        "moe_b_down": nrm((DEPTH, N_EXPERTS, D), 0.01),
    }


def reference(x, ln1_g, ln1_b, ln2_g, ln2_b,
              gla_w_in, gla_gate_w2, gla_gate_b, gla_norm_g, gla_w_out,
              hy_w_in, hy_b_in, hy_conv_w, hy_conv_b, hy_ffn_w1, hy_ffn_b1, hy_sin_freq,
              hy_ffn_w2, hy_ffn_b2, hy_ffn_w3, hy_filt_bias, hy_w_out, hy_b_out,
              na_w_in, na_b_in, na_rpb, na_w_out, na_b_out,
              moe_router_w, moe_router_b, moe_w_gu, moe_b_gu, moe_w_down, moe_b_down):
    for i in range(DEPTH):
        m, j = i % N_MIXERS, i // N_MIXERS
        if m == 0:
            h = _gla_mixer(x, gla_w_in[j], gla_gate_w2[j], gla_gate_b[j], gla_norm_g[j], gla_w_out[j])
        elif m == 1:
            h = _hyena_mixer(x, hy_w_in[j], hy_b_in[j], hy_conv_w[j], hy_conv_b[j], hy_ffn_w1[j],
                             hy_ffn_b1[j], hy_sin_freq[j], hy_ffn_w2[j], hy_ffn_b2[j], hy_ffn_w3[j],
                             hy_filt_bias[j], hy_w_out[j], hy_b_out[j])
        else:
            h = _na_mixer(x, na_w_in[j], na_b_in[j], na_rpb[j], na_w_out[j], na_b_out[j])
        x = _layer_norm(DEEPNORM_ALPHA * x + h, ln1_g[i], ln1_b[i])
        f = _moe(x, moe_router_w[i], moe_router_b[i], moe_w_gu[i], moe_b_gu[i],
                 moe_w_down[i], moe_b_down[i])
        x = _layer_norm(DEEPNORM_ALPHA * x + f, ln2_g[i], ln2_b[i])
    return x
```

```python
import functools
import math

import jax
import jax.numpy as jnp
from jax import lax
from jax.experimental import pallas as pl
from jax.experimental.pallas import tpu as pltpu

F32, BF16, I32 = jnp.float32, jnp.bfloat16, jnp.int32
HIGHEST = lax.Precision.HIGHEST
NT_DIMS = (((1,), (1,)), ((), ()))
TN_DIMS = (((0,), (0,)), ((), ()))

D_MODEL = 1024
DEPTH = 4
N_MIXERS = 3
GRID_W = 64
ALPHA = (2 * DEPTH) ** 0.25
LN_EPS = 1e-5

GLA_HEADS = 4
GLA_KEY_DIM = 512
GLA_VAL_DIM = 1024
GLA_HEAD_K = 128
GLA_HEAD_V = 256
GLA_GATE_RANK = 16
GLA_TAU = 16.0
GLA_NORM_EPS = 1e-6

HYENA_ORDER = 2
HYENA_POS_EMB = 33
HYENA_FILTER_HIDDEN = 64
HYENA_DECAY_TARGET = 1e-2
HYENA_FAST_DECAY = 0.3
HYENA_SLOW_DECAY = 1.5
HYENA_WINDOW_SHIFT = 0.05

NA_HEAD_DIM = 32
NA_HEADS = 32
NA_WIN_ROWS = 8
NA_WIN_COLS = 16

N_EXPERTS = 32
TOP_K = 4
SWIGLU_ALPHA = 1.702
SWIGLU_LIMIT = 7.0

LANES = 128
BF16_SUBLANES = 16
VMEM_LIMIT = 48 * 1024 * 1024

ROW_TILE = 512
GLA_CHUNK = 64
GLA_SUB = 16
GLA_ROWS = 256
GLA_EXP_CLAMP = 80.0
DFT_TILE = 512
DFT_COLS = 512
NA_KEY_ROWS = 10
NEG = -1e30

MOE_TB = 512
MOE_G = BF16_SUBLANES
MOE_RB = MOE_TB * TOP_K + 512
MOE_PCH = 512
MOE_TM = 512
MOE_CPT = MOE_TM // MOE_G


def _cparams(sem):
    return pltpu.CompilerParams(dimension_semantics=sem, vmem_limit_bytes=VMEM_LIMIT)


def _ln_rows(y, g, b):
    mu = jnp.mean(y, axis=-1, keepdims=True)
    yc = y - mu
    var = jnp.mean(yc * yc, axis=-1, keepdims=True)
    return yc * lax.rsqrt(var + LN_EPS) * g + b


def _row2(v):
    return v.reshape(1, -1)


def _proj_ln_kernel(a_ref, w_ref, bias_ref, x_ref, g_ref, b_ref, o_ref):
    h = jnp.dot(a_ref[...], w_ref[...], preferred_element_type=F32) + bias_ref[...]
    o_ref[...] = _ln_rows(ALPHA * x_ref[...] + h, g_ref[...], b_ref[...])


def _proj_ln(a, w, bias, x, g, b):
    n, d = x.shape
    row = lambda i: (i, 0)
    fix = lambda i: (0, 0)
    return pl.pallas_call(
        _proj_ln_kernel,
        grid=(n // ROW_TILE,),
        in_specs=[pl.BlockSpec((ROW_TILE, a.shape[1]), row), pl.BlockSpec(w.shape, fix),
                  pl.BlockSpec((1, d), fix), pl.BlockSpec((ROW_TILE, d), row),
                  pl.BlockSpec((1, d), fix), pl.BlockSpec((1, d), fix)],
        out_specs=pl.BlockSpec((ROW_TILE, d), row),
        out_shape=jax.ShapeDtypeStruct((n, d), F32),
        compiler_params=_cparams(("parallel",)),
        name="proj_ln",
    )(a, w, _row2(bias), x, _row2(g), _row2(b))


def _gla_in_kernel(x_ref, w_ref, w2_ref, gb_ref, q_ref, k_ref, v_ref, g_ref, la_ref):
    kd, vd = GLA_KEY_DIM, GLA_VAL_DIM
    proj = jnp.dot(x_ref[...].astype(BF16), w_ref[...], preferred_element_type=F32)
    q_ref[...] = (proj[:, :kd] * GLA_HEAD_K ** -0.5).astype(BF16)
    k_ref[...] = proj[:, kd:2 * kd].astype(BF16)
    v_ref[...] = proj[:, 2 * kd:2 * kd + vd].astype(BF16)
    og = proj[:, 2 * kd + vd:2 * kd + 2 * vd]
    g_ref[...] = (og * jax.nn.sigmoid(og)).astype(BF16)
    glow = proj[:, 2 * kd + 2 * vd:]
    z = jnp.dot(glow, w2_ref[...], precision=HIGHEST, preferred_element_type=F32) + gb_ref[...]
    la_ref[...] = (jnp.minimum(z, 0.0) - jnp.log(1.0 + jnp.exp(-jnp.abs(z)))) * (1.0 / GLA_TAU)


def _gla_in(x, w_in, gate_w2, gate_b):
    n, d = x.shape
    kd, vd, r = GLA_KEY_DIM, GLA_VAL_DIM, GLA_GATE_RANK
    width = 2 * kd + 2 * vd
    w = jnp.pad(w_in, ((0, 0), (0, LANES - 2 * r))).astype(BF16)
    w2 = jnp.zeros((LANES, 2 * kd), F32)
    w2 = w2.at[:r, :kd].set(gate_w2[0]).at[r:2 * r, kd:].set(gate_w2[1])
    gb = gate_b.reshape(1, 2 * kd)
    row = lambda i: (i, 0)
    fix = lambda i: (0, 0)
    return pl.pallas_call(
        _gla_in_kernel,
        grid=(n // ROW_TILE,),
        in_specs=[pl.BlockSpec((ROW_TILE, d), row), pl.BlockSpec((d, width + LANES), fix),
                  pl.BlockSpec((LANES, 2 * kd), fix), pl.BlockSpec((1, 2 * kd), fix)],
        out_specs=[pl.BlockSpec((ROW_TILE, kd), row), pl.BlockSpec((ROW_TILE, kd), row),
                   pl.BlockSpec((ROW_TILE, vd), row), pl.BlockSpec((ROW_TILE, vd), row),
                   pl.BlockSpec((ROW_TILE, 2 * kd), row)],
        out_shape=[jax.ShapeDtypeStruct((n, kd), BF16), jax.ShapeDtypeStruct((n, kd), BF16),
                   jax.ShapeDtypeStruct((n, vd), BF16), jax.ShapeDtypeStruct((n, vd), BF16),
                   jax.ShapeDtypeStruct((n, 2 * kd), F32)],
        compiler_params=_cparams(("parallel",)),
        name="gla_in",
    )(x, w, w2, gb)


def _gla_chunk(q, k, v, la, st, reverse):
    c_rows = q.shape[0]
    ri = lax.broadcasted_iota(I32, (c_rows, c_rows), 0)
    ci = lax.broadcasted_iota(I32, (c_rows, c_rows), 1)
    if reverse:
        cum_mat = (ci >= ri).astype(F32)
        mask = ci > ri
    else:
        cum_mat = (ci <= ri).astype(F32)
        mask = ci <= ri
    c = jnp.dot(cum_mat, la, precision=HIGHEST, preferred_element_type=F32)
    ctot = c[0:1] if reverse else c[c_rows - 1:c_rows]
    qf = q.astype(F32)
    kf = k.astype(F32)
    qe = (qf * jnp.exp(c)).astype(BF16)
    o = lax.dot_general(qe, st.astype(BF16), NT_DIMS, preferred_element_type=F32)
    blocks = []
    for i in range(c_rows // GLA_SUB):
        lo, hi = i * GLA_SUB, (i + 1) * GLA_SUB
        edge = hi - 1 if reverse else lo
        ref = c[edge:edge + 1] - la[edge:edge + 1]
        qi = (qf[lo:hi] * jnp.exp(c[lo:hi] - ref)).astype(BF16)
        ki = (kf * jnp.exp(jnp.minimum(ref - c, GLA_EXP_CLAMP))).astype(BF16)
        blocks.append(lax.dot_general(qi, ki, NT_DIMS, preferred_element_type=F32))
    scores = jnp.where(mask, jnp.concatenate(blocks, axis=0), 0.0).astype(BF16)
    o = o + jnp.dot(scores, v, preferred_element_type=F32)
    kd = (kf * jnp.exp(ctot - c)).astype(BF16)
    st_new = st * jnp.exp(ctot) + lax.dot_general(v, kd, TN_DIMS, preferred_element_type=F32)
    return o, st_new


def _gla_scan_kernel(qf_ref, kf_ref, vf_ref, laf_ref, qb_ref, kb_ref, vb_ref, lab_ref,
                     of_ref, ob_ref, sf_ref, sb_ref):
    @pl.when(pl.program_id(2) == 0)
    def _():
        sf_ref[...] = jnp.zeros_like(sf_ref)
        sb_ref[...] = jnp.zeros_like(sb_ref)

    nch = GLA_ROWS // GLA_CHUNK
    sf = sf_ref[...]
    sb = sb_ref[...]
    for ch in range(nch):
        lo = ch * GLA_CHUNK
        o, sf = _gla_chunk(qf_ref[lo:lo + GLA_CHUNK], kf_ref[lo:lo + GLA_CHUNK],
                           vf_ref[lo:lo + GLA_CHUNK], laf_ref[lo:lo + GLA_CHUNK], sf, False)
        of_ref[lo:lo + GLA_CHUNK] = o.astype(BF16)
        lo = (nch - 1 - ch) * GLA_CHUNK
        o, sb = _gla_chunk(qb_ref[lo:lo + GLA_CHUNK], kb_ref[lo:lo + GLA_CHUNK],
                           vb_ref[lo:lo + GLA_CHUNK], lab_ref[lo:lo + GLA_CHUNK], sb, True)
        ob_ref[lo:lo + GLA_CHUNK] = o.astype(BF16)
    sf_ref[...] = sf
    sb_ref[...] = sb


def _gla_scan(q, k, v, la, batch):
    n = q.shape[0]
    nblk = n // batch // GLA_ROWS
    hk, hv, nh = GLA_HEAD_K, GLA_HEAD_V, GLA_HEADS
    fwd = lambda b, h, i: (b * nblk + i, h)
    bwd = lambda b, h, i: (b * nblk + nblk - 1 - i, h)
    bwd_la = lambda b, h, i: (b * nblk + nblk - 1 - i, nh + h)
    return pl.pallas_call(
        _gla_scan_kernel,
        grid=(batch, nh, nblk),
        in_specs=[pl.BlockSpec((GLA_ROWS, hk), fwd), pl.BlockSpec((GLA_ROWS, hk), fwd),
                  pl.BlockSpec((GLA_ROWS, hv), fwd), pl.BlockSpec((GLA_ROWS, hk), fwd),
                  pl.BlockSpec((GLA_ROWS, hk), bwd), pl.BlockSpec((GLA_ROWS, hk), bwd),
                  pl.BlockSpec((GLA_ROWS, hv), bwd), pl.BlockSpec((GLA_ROWS, hk), bwd_la)],
        out_specs=[pl.BlockSpec((GLA_ROWS, hv), fwd), pl.BlockSpec((GLA_ROWS, hv), bwd)],
        out_shape=[jax.ShapeDtypeStruct((n, GLA_VAL_DIM), BF16)] * 2,
        scratch_shapes=[pltpu.VMEM((hv, hk), F32), pltpu.VMEM((hv, hk), F32)],
        compiler_params=_cparams(("parallel", "parallel", "arbitrary")),
        name="gla_scan",
    )(q, k, v, la, q, k, v, la)


def _gla_out_kernel(of_ref, ob_ref, gate_ref, ng_ref, w_ref, x_ref, g_ref, b_ref, o_ref):
    o = of_ref[...].astype(F32) + ob_ref[...].astype(F32)
    parts = []
    for h in range(GLA_HEADS):
        oh = o[:, h * GLA_HEAD_V:(h + 1) * GLA_HEAD_V]
        ms = jnp.mean(oh * oh, axis=-1, keepdims=True)
        parts.append(oh * lax.rsqrt(ms + GLA_NORM_EPS))
    on = jnp.concatenate(parts, axis=-1) * ng_ref[...]
    a = (on * gate_ref[...].astype(F32)).astype(BF16)
    h = jnp.dot(a, w_ref[...], preferred_element_type=F32)
    o_ref[...] = _ln_rows(ALPHA * x_ref[...] + h, g_ref[...], b_ref[...])


def _gla_mixer_ln(x, batch, w_in, gate_w2, gate_b, norm_g, w_out, ln_g, ln_b):
    n, d = x.shape
    q, k, v, gate, la = _gla_in(x, w_in, gate_w2, gate_b)
    o_f, o_b = _gla_scan(q, k, v, la, batch)
    row = lambda i: (i, 0)
    fix = lambda i: (0, 0)
    vd = GLA_VAL_DIM
    return pl.pallas_call(
        _gla_out_kernel,
        grid=(n // ROW_TILE,),
        in_specs=[pl.BlockSpec((ROW_TILE, vd), row), pl.BlockSpec((ROW_TILE, vd), row),
                  pl.BlockSpec((ROW_TILE, vd), row), pl.BlockSpec((1, vd), fix),
                  pl.BlockSpec((vd, d), fix), pl.BlockSpec((ROW_TILE, d), row),
                  pl.BlockSpec((1, d), fix), pl.BlockSpec((1, d), fix)],
        out_specs=pl.BlockSpec((ROW_TILE, d), row),
        out_shape=jax.ShapeDtypeStruct((n, d), F32),
        compiler_params=_cparams(("parallel",)),
        name="gla_out",
    )(o_f, o_b, gate, _row2(jnp.tile(norm_g, GLA_HEADS)), w_out.astype(BF16), x,
      _row2(ln_g), _row2(ln_b))


def _hy_in_kernel(x_ref, w_ref, b_ref, cw_ref, cb_ref, o_ref):
    p = jnp.dot(x_ref[...], w_ref[...], preferred_element_type=F32) + b_ref[...]
    rows = p.shape[0]
    ri = lax.broadcasted_iota(I32, p.shape, 0)
    prev = jnp.where(ri == 0, 0.0, pltpu.roll(p, 1, axis=0))
    nxt = jnp.where(ri == rows - 1, 0.0, pltpu.roll(p, rows - 1, axis=0))
    cw = cw_ref[...]
    o_ref[...] = (prev * cw[0:1] + p * cw[1:2] + nxt * cw[2:3] + cb_ref[...]).astype(BF16)


def _hy_in(xb, batch, w_in, b_in, conv_w, conv_b):
    n, d = xb.shape
    seq = n // batch
    width = w_in.shape[1]
    cols = 256
    return pl.pallas_call(
        _hy_in_kernel,
        grid=(batch, width // cols),
        in_specs=[pl.BlockSpec((seq, d), lambda b, j: (b, 0)), pl.BlockSpec((d, cols), lambda b, j: (0, j)),
                  pl.BlockSpec((1, cols), lambda b, j: (0, j)), pl.BlockSpec((3, cols), lambda b, j: (0, j)),
                  pl.BlockSpec((1, cols), lambda b, j: (0, j))],
        out_specs=pl.BlockSpec((seq, cols), lambda b, j: (b, j)),
        out_shape=jax.ShapeDtypeStruct((n, width), BF16),
        compiler_params=_cparams(("parallel", "arbitrary")),
        name="hy_in",
    )(xb, w_in.astype(BF16), _row2(b_in), conv_w, _row2(conv_b))


def _hy_filter_kernel(pe_ref, w1_ref, b1_ref, f_ref, w2_ref, b2_ref, w3_ref, win_ref, hs_ref, hd_ref):
    f = f_ref[...]
    h = jnp.sin(f[0:1] * (jnp.dot(pe_ref[...], w1_ref[...], precision=HIGHEST,
                                  preferred_element_type=F32) + b1_ref[...]))
    h = jnp.sin(f[1:2] * (jnp.dot(h, w2_ref[...], precision=HIGHEST,
                                  preferred_element_type=F32) + b2_ref[...]))
    h = jnp.dot(h, w3_ref[...], precision=HIGHEST, preferred_element_type=F32)
    win = win_ref[...]
    d = win.shape[1]
    rows = win.shape[0]
    t = pl.program_id(0) * rows + lax.broadcasted_iota(I32, (rows, 1), 0)
    for o in range(HYENA_ORDER):
        hf = h[:, 2 * o * d:(2 * o + 1) * d] * win
        hb = jnp.where(t == 0, 0.0, h[:, (2 * o + 1) * d:(2 * o + 2) * d] * win)
        hs_ref[:, o * d:(o + 1) * d] = (hf + hb).astype(BF16)
        hd_ref[:, o * d:(o + 1) * d] = (hf - hb).astype(BF16)


def _hy_spec_kernel(c_ref, s_ref, hs_ref, hd_ref, hr_ref, hi_ref, hn_ref):
    hs = hs_ref[...]
    a1 = jnp.dot(c_ref[...], hs, preferred_element_type=F32)
    a2 = jnp.dot(s_ref[...], hd_ref[...], preferred_element_type=F32)
    a3 = jnp.dot(s_ref[...], hs, preferred_element_type=F32)
    ri = lax.broadcasted_iota(I32, a1.shape, 0) + pl.program_id(0) * a1.shape[0]
    hr_ref[...] = a1
    hi_ref[...] = jnp.where(ri == 0, 0.0, a2)
    hn_ref[...] = jnp.where(ri == 0, a3, a1)


def _dft_consts(seq):
    k = jnp.arange(seq, dtype=I32)
    ang = ((k[:, None] * k[None, :]) % (2 * seq)).astype(F32) * (math.pi / seq)
    cos, sin = jnp.cos(ang), jnp.sin(ang)
    sgn = jnp.where(k % 2 == 0, 1.0, -1.0).astype(F32)
    fwd_i = jnp.where(k[:, None] == 0, sgn[None, :], -sin)
    scale = jnp.where(k == 0, 0.5 / seq, 1.0 / seq).astype(F32)
    inv_r = cos * scale[None, :]
    inv_i = jnp.where(k[None, :] == 0, sgn[:, None] * (0.5 / seq), -sin / seq)
    return cos.astype(BF16), fwd_i.astype(BF16), inv_r.astype(BF16), inv_i.astype(BF16)


def _hy_filters(seq, consts, w1, b1, sin_freq, w2, b2, w3):
    cos, fwd_i = consts[0], consts[1]
    d = w3.shape[-1] // (2 * HYENA_ORDER)
    hid = HYENA_FILTER_HIDDEN
    t = jnp.arange(seq, dtype=F32)
    t_norm = t / max(seq - 1, 1)
    bands = (HYENA_POS_EMB - 1) // 2
    freqs = jnp.linspace(1e-4, bands - 1, bands, dtype=F32)
    ang = (2.0 * math.pi / seq) * t[:, None] * freqs[None, :]
    pe = jnp.concatenate([t_norm[:, None], jnp.cos(ang), -jnp.sin(ang)], axis=-1)
    pe = jnp.pad(pe, ((0, 0), (0, LANES - HYENA_POS_EMB)))
    min_decay = math.log(HYENA_DECAY_TARGET) / HYENA_SLOW_DECAY
    max_decay = math.log(HYENA_DECAY_TARGET) / HYENA_FAST_DECAY
    deltas = jnp.abs(jnp.linspace(min_decay, max_decay, d, dtype=F32))
    window = jnp.exp(-t_norm[:, None] * deltas[None, :]) + HYENA_WINDOW_SHIFT
    padc = LANES - hid
    w1p = jnp.pad(w1, ((0, LANES - HYENA_POS_EMB), (0, padc)))
    w2p = jnp.pad(w2, ((0, padc), (0, padc)))
    w3p = jnp.pad(w3, ((0, padc), (0, 0)))
    rows = 256
    fix = lambda i: (0, 0)
    row = lambda i: (i, 0)
    hs, hd = pl.pallas_call(
        _hy_filter_kernel,
        grid=(seq // rows,),
        in_specs=[pl.BlockSpec((rows, LANES), row), pl.BlockSpec((LANES, LANES), fix),
                  pl.BlockSpec((1, LANES), fix), pl.BlockSpec((2, LANES), fix),
                  pl.BlockSpec((LANES, LANES), fix), pl.BlockSpec((1, LANES), fix),
                  pl.BlockSpec((LANES, 2 * HYENA_ORDER * d), fix), pl.BlockSpec((rows, d), row)],
        out_specs=[pl.BlockSpec((rows, HYENA_ORDER * d), row)] * 2,
        out_shape=[jax.ShapeDtypeStruct((seq, HYENA_ORDER * d), BF16)] * 2,
        compiler_params=_cparams(("parallel",)),
        name="hy_filter",
    )(pe, w1p, _row2(jnp.pad(b1, (0, padc))), jnp.pad(sin_freq, ((0, 0), (0, padc))), w2p,
      _row2(jnp.pad(b2, (0, padc))), w3p, window)
    tf, tn = DFT_TILE, DFT_COLS
    plane = jax.ShapeDtypeStruct((seq, HYENA_ORDER * d), F32)
    return pl.pallas_call(
        _hy_spec_kernel,
        grid=(seq // tf, HYENA_ORDER * d // tn),
        in_specs=[pl.BlockSpec((tf, seq), lambda i, j: (i, 0)), pl.BlockSpec((tf, seq), lambda i, j: (i, 0)),
                  pl.BlockSpec((seq, tn), lambda i, j: (0, j)), pl.BlockSpec((seq, tn), lambda i, j: (0, j))],
        out_specs=[pl.BlockSpec((tf, tn), lambda i, j: (i, j))] * 3,
        out_shape=[plane] * 3,
        compiler_params=_cparams(("parallel", "arbitrary")),
        name="hy_spec",
    )(cos, fwd_i, hs, hd)


def _dft_fwd_kernel(c_ref, s_ref, z_ref, hr_ref, hi_ref, hn_ref, yr_ref, yi_ref):
    z = z_ref[0]
    zr = jnp.dot(c_ref[...], z, preferred_element_type=F32)
    zi = jnp.dot(s_ref[...], z, preferred_element_type=F32)
    hi = hi_ref[...]
    yr_ref[0] = (zr * hr_ref[...] - zi * hi).astype(BF16)
    yi_ref[0] = (zr * hi + zi * hn_ref[...]).astype(BF16)


def _dft_inv_kernel(gr_ref, gi_ref, yr_ref, yi_ref, z_ref, gate_ref, fb_ref, o_ref):
    y = (jnp.dot(gr_ref[...], yr_ref[0], preferred_element_type=F32)
         + jnp.dot(gi_ref[...], yi_ref[0], preferred_element_type=F32))
    z = z_ref[0].astype(F32)
    o_ref[0] = (gate_ref[0].astype(F32) * (y + z * fb_ref[...])).astype(BF16)


def _long_conv_gate(consts, planes, order, z, z_col, gate, gate_col, filt_bias):
    cos, fwd_i, inv_r, inv_i = consts
    batch, seq, _ = z.shape
    d = D_MODEL
    tf, tn = DFT_TILE, DFT_COLS
    nj = d // tn
    mat = pl.BlockSpec((tf, seq), lambda i, b, j: (i, 0))
    spec_plane = pl.BlockSpec((tf, tn), lambda i, b, j: (i, order * nj + j))
    full_cols = lambda off: pl.BlockSpec((1, seq, tn), lambda i, b, j: (b, 0, off + j))
    tile_cols = lambda off: pl.BlockSpec((1, tf, tn), lambda i, b, j: (b, i, off + j))
    spec_shape = jax.ShapeDtypeStruct((batch, seq, d), BF16)
    yr, yi = pl.pallas_call(
        _dft_fwd_kernel,
        grid=(seq // tf, batch, nj),
        in_specs=[mat, mat, full_cols(z_col), spec_plane, spec_plane, spec_plane],
        out_specs=[tile_cols(0), tile_cols(0)],
        out_shape=[spec_shape, spec_shape],
        compiler_params=_cparams(("parallel", "parallel", "arbitrary")),
        name="dft_fwd",
    )(cos, fwd_i, z, planes[0], planes[1], planes[2])
    return pl.pallas_call(
        _dft_inv_kernel,
        grid=(seq // tf, batch, nj),
        in_specs=[mat, mat, full_cols(0), full_cols(0), tile_cols(z_col), tile_cols(gate_col),
                  pl.BlockSpec((1, tn), lambda i, b, j: (0, j))],
        out_specs=tile_cols(0),
        out_shape=spec_shape,
        compiler_params=_cparams(("parallel", "parallel", "arbitrary")),
        name="dft_inv",
    )(inv_r, inv_i, yr, yi, z, gate, _row2(filt_bias))


def _hyena_mixer_ln(x, batch, w_in, b_in, conv_w, conv_b, ffn_w1, ffn_b1, sin_freq, ffn_w2, ffn_b2,
                    ffn_w3, filt_bias, w_out, b_out, ln_g, ln_b):
    n, d = x.shape
    seq = n // batch
    nj = d // DFT_COLS
    consts = _dft_consts(seq)
    planes = _hy_filters(seq, consts, ffn_w1, ffn_b1, sin_freq, ffn_w2, ffn_b2, ffn_w3)
    u = _hy_in(x.astype(BF16), batch, w_in, b_in, conv_w, conv_b).reshape(batch, seq, 3 * d)
    z = _long_conv_gate(consts, planes, 0, u, 0, u, nj, filt_bias[0])
    z = _long_conv_gate(consts, planes, 1, z, 0, u, 2 * nj, filt_bias[1])
    return _proj_ln(z.reshape(n, d), w_out.astype(BF16), b_out, x, ln_g, ln_b)


def _na_in_kernel(x_ref, w_ref, b_ref, q_ref, k_ref, v_ref):
    d = D_MODEL
    p = jnp.dot(x_ref[...].astype(BF16), w_ref[...], preferred_element_type=F32) + b_ref[...]
    q_ref[...] = (p[:, :d] * NA_HEAD_DIM ** -0.5).astype(BF16)
    k_ref[...] = p[:, d:2 * d].astype(BF16)
    v_ref[...] = p[:, 2 * d:].astype(BF16)


def _na_attn_kernel(q_ref, k_ref, vt_ref, bias_ref, o_ref):
    hpg = LANES // NA_HEAD_DIM
    n_rows = GRID_W
    rp = pl.program_id(2)
    r0 = 2 * rp
    rs0 = jnp.clip(r0 - NA_WIN_ROWS // 2, 0, n_rows - NA_WIN_ROWS)
    rs1 = jnp.clip(r0 + 1 - NA_WIN_ROWS // 2, 0, n_rows - NA_WIN_ROWS)
    start = jnp.minimum(rs0, n_rows - NA_KEY_ROWS)
    tok0 = pl.multiple_of(start * GRID_W, LANES)
    nkeys = NA_KEY_ROWS * GRID_W
    q = q_ref[...]
    kw = k_ref[pl.ds(tok0, nkeys), :]
    vtw = vt_ref[0, :, pl.ds(tok0, nkeys)]
    lane = lax.broadcasted_iota(I32, (1, LANES), 1)
    sub = lax.broadcasted_iota(I32, (LANES, 1), 0)
    right = lane >= GRID_W
    acc = jnp.zeros((LANES, LANES), F32)
    for h in range(hpg):
        in_head = (lane >= h * NA_HEAD_DIM) & (lane < (h + 1) * NA_HEAD_DIM)
        kh = jnp.where(in_head, kw, jnp.zeros_like(kw))
        s = lax.dot_general(kh, q, NT_DIMS, preferred_element_type=F32)
        tiles = []
        for a in range(NA_KEY_ROWS):
            kr = start + a
            dr0 = kr - r0 + (NA_WIN_ROWS - 1)
            pen0 = jnp.where((kr >= rs0) & (kr < rs0 + NA_WIN_ROWS), 0.0, NEG)
            pen1 = jnp.where((kr >= rs1) & (kr < rs1 + NA_WIN_ROWS), 0.0, NEG)
            bt = bias_ref[h, jnp.clip(dr0, 0, 2 * NA_WIN_ROWS - 1)]
            tiles.append(bt + jnp.where(right, pen1, pen0))
        s = s + jnp.concatenate(tiles, axis=0)
        m = jnp.max(s, axis=0, keepdims=True)
        p = jnp.exp(s - m)
        p = (p / jnp.sum(p, axis=0, keepdims=True)).astype(BF16)
        in_head_rows = (sub >= h * NA_HEAD_DIM) & (sub < (h + 1) * NA_HEAD_DIM)
        vh = jnp.where(in_head_rows, vtw, jnp.zeros_like(vtw))
        acc = acc + jnp.dot(vh, p, preferred_element_type=F32)
    o_ref[...] = acc.T.astype(BF16)


def _na_bias_table(rpb):
    nh = rpb.shape[0]
    kc = jnp.arange(GRID_W)[:, None]
    qc = jnp.arange(GRID_W)[None, :]
    cs = jnp.clip(qc - NA_WIN_COLS // 2, 0, GRID_W - NA_WIN_COLS)
    inwin = (kc >= cs) & (kc < cs + NA_WIN_COLS)
    dc = jnp.clip(kc - qc + (NA_WIN_COLS - 1), 0, 2 * NA_WIN_COLS - 2)
    colb = jnp.where(inwin[None, None], rpb.astype(F32)[:, :, dc], NEG)
    pad = jnp.full((nh, 1, GRID_W, GRID_W), NEG, F32)
    left = jnp.concatenate([colb, pad], axis=1)
    rgt = jnp.concatenate([pad, colb], axis=1)
    return jnp.concatenate([left, rgt], axis=-1)


def _na_mixer_ln(x, batch, w_in, b_in, rpb, w_out, b_out, ln_g, ln_b):
    n, d = x.shape
    seq = n // batch
    assert seq == GRID_W * GRID_W
    row = lambda i: (i, 0)
    fix = lambda i: (0, 0)
    qkv_shape = jax.ShapeDtypeStruct((n, d), BF16)
    q, k, v = pl.pallas_call(
        _na_in_kernel,
        grid=(n // ROW_TILE,),
        in_specs=[pl.BlockSpec((ROW_TILE, d), row), pl.BlockSpec((d, 3 * d), fix), pl.BlockSpec((1, 3 * d), fix)],
        out_specs=[pl.BlockSpec((ROW_TILE, d), row)] * 3,
        out_shape=[qkv_shape] * 3,
        compiler_params=_cparams(("parallel",)),
        name="na_in",
    )(x, w_in.astype(BF16), _row2(b_in))
    vt = jnp.swapaxes(v.reshape(batch, seq, d), 1, 2)
    bias = _na_bias_table(rpb)
    hpg = LANES // NA_HEAD_DIM
    n_pairs = GRID_W // 2
    o = pl.pallas_call(
        _na_attn_kernel,
        grid=(batch, d // LANES, n_pairs),
        in_specs=[pl.BlockSpec((LANES, LANES), lambda b, g, r: (b * n_pairs + r, g)),
                  pl.BlockSpec((seq, LANES), lambda b, g, r: (b, g)),
                  pl.BlockSpec((1, LANES, seq), lambda b, g, r: (b, g, 0)),
                  pl.BlockSpec((hpg, 2 * NA_WIN_ROWS, GRID_W, LANES), lambda b, g, r: (g, 0, 0, 0))],
        out_specs=pl.BlockSpec((LANES, LANES), lambda b, g, r: (b * n_pairs + r, g)),
        out_shape=qkv_shape,
        compiler_params=_cparams(("parallel", "parallel", "arbitrary")),
        name="na_attn",
    )(q, k, vt, bias)
    return _proj_ln(o, w_out.astype(BF16), b_out, x, ln_g, ln_b)


def _moe_route_kernel(x_ref, wr_ref, rb_ref, xs_ref, pos_ref, gate_ref, cnt_ref):
    ne, tb = N_EXPERTS, MOE_TB
    x = x_ref[...]
    logits = lax.dot_general(wr_ref[...], x, NT_DIMS, precision=HIGHEST,
                             preferred_element_type=F32) + rb_ref[...]
    eio = lax.broadcasted_iota(I32, (ne, tb), 0)
    work = logits
    vals, hots = [], []
    for _ in range(TOP_K):
        m = jnp.max(work, axis=0, keepdims=True)
        idx = jnp.min(jnp.where(work == m, eio, ne), axis=0, keepdims=True)
        hot = eio == idx
        vals.append(m)
        hots.append(hot)
        work = jnp.where(hot, -jnp.inf, work)
    exps = [jnp.exp(v - vals[0]) for v in vals]
    denom = exps[0] + exps[1] + exps[2] + exps[3]
    sel = jnp.zeros((ne, tb), F32)
    for hot in hots:
        sel = sel + jnp.where(hot, 1.0, 0.0)
    earlier = (lax.broadcasted_iota(I32, (tb, tb), 0) < lax.broadcasted_iota(I32, (tb, tb), 1))
    rank = jnp.dot(sel.astype(BF16), jnp.where(earlier, 1.0, 0.0).astype(BF16),
                   preferred_element_type=F32)
    cnt = jnp.sum(sel, axis=1, keepdims=True)
    cnt_pad = jnp.ceil(cnt * (1.0 / MOE_G)) * MOE_G
    below = (lax.broadcasted_iota(I32, (ne, ne), 1) < lax.broadcasted_iota(I32, (ne, ne), 0))
    seg_off = jnp.dot(jnp.where(below, 1.0, 0.0), jnp.broadcast_to(cnt_pad, (ne, tb)),
                      precision=HIGHEST, preferred_element_type=F32)
    slot = seg_off + rank
    pos = [jnp.sum(jnp.where(hot, slot, 0.0), axis=0, keepdims=True).astype(I32) for hot in hots]
    zero_row = jnp.zeros((1, tb), I32)
    pos_ref[0] = jnp.concatenate(pos + [zero_row] * (8 - TOP_K), axis=0)
    gate_ref[0] = jnp.concatenate([e / denom for e in exps] + [zero_row.astype(F32)] * (8 - TOP_K), axis=0)
    cnt_ref[0] = jnp.broadcast_to(cnt_pad, (ne, LANES)).astype(I32)
    xb = x.astype(BF16)
    for r in range(MOE_RB // MOE_PCH):
        jio = lax.broadcasted_iota(I32, (MOE_PCH, tb), 0) + r * MOE_PCH
        onehot = jnp.zeros((MOE_PCH, tb), F32)
        for p in pos:
            onehot = onehot + jnp.where(jio == p, 1.0, 0.0)
        xs_ref[r * MOE_PCH:(r + 1) * MOE_PCH] = jnp.dot(
            onehot.astype(BF16), xb, preferred_element_type=F32).astype(BF16)


def _moe_expert_kernel(te_ref, src_ref, nt_ref, xs_hbm, wgu_ref, bgu_ref, wd_ref, bd_ref, ys_in_hbm,
                       ys_hbm, xbuf, ybuf, sem_in, sem_out):
    del ys_in_hbm, te_ref
    t = pl.program_id(0)
    nt = nt_ref[0]
    s = t % 2
    zero_chunk = nt_ref[1]
    dump_chunk = nt_ref[2]

    def in_copy(tile, slot, c):
        chunk = src_ref[tile * MOE_CPT + c]
        return pltpu.make_async_copy(
            xs_hbm.at[pl.ds(pl.multiple_of(chunk * MOE_G, MOE_G), MOE_G)],
            xbuf.at[slot, pl.ds(c * MOE_G, MOE_G)], sem_in.at[slot])

    def out_copy(tile, slot, c):
        chunk = src_ref[tile * MOE_CPT + c]
        chunk = jnp.where(chunk == zero_chunk, dump_chunk + slot * MOE_CPT + c, chunk)
        return pltpu.make_async_copy(
            ybuf.at[slot, pl.ds(c * MOE_G, MOE_G)],
            ys_hbm.at[pl.ds(pl.multiple_of(chunk * MOE_G, MOE_G), MOE_G)], sem_out.at[slot])

    def start_in(tile, slot):
        for c in range(MOE_CPT):
            in_copy(tile, slot, c).start()

    def wait_out(tile, slot):
        for c in range(MOE_CPT):
            out_copy(tile, slot, c).wait()

    @pl.when(t == 0)
    def _():
        start_in(0, 0)

    @pl.when(t + 1 < nt)
    def _():
        start_in(t + 1, 1 - s)

    @pl.when(t < nt)
    def _():
        for c in range(MOE_CPT):
            in_copy(t, s, c).wait()
        f = wd_ref.shape[1]
        h = jnp.dot(xbuf[s], wgu_ref[0], preferred_element_type=F32) + bgu_ref[0]
        glu = jnp.minimum(h[:, :f], SWIGLU_LIMIT)
        lin = jnp.clip(h[:, f:], -SWIGLU_LIMIT, SWIGLU_LIMIT)
        act = (glu * jax.nn.sigmoid(SWIGLU_ALPHA * glu) * (lin + 1.0)).astype(BF16)
        y = jnp.dot(act, wd_ref[0], preferred_element_type=F32) + bd_ref[0]

        @pl.when(t >= 2)
        def _():
            wait_out(t - 2, s)

        ybuf[s] = y.astype(BF16)
        for c in range(MOE_CPT):
            out_copy(t, s, c).start()

        @pl.when(t == nt - 1)
        def _():
            wait_out(t, s)

            @pl.when(t >= 1)
            def _():
                wait_out(t - 1, 1 - s)


def _moe_combine_kernel(ys_ref, pos_ref, gate_ref, x_ref, g_ref, b_ref, o_ref):
    tb = MOE_TB
    pos = pos_ref[...]
    gate = gate_ref[...]
    f = jnp.zeros((tb, D_MODEL), F32)
    for r in range(MOE_RB // MOE_PCH):
        jio = lax.broadcasted_iota(I32, (tb, MOE_PCH), 1) + r * MOE_PCH
        w = jnp.zeros((tb, MOE_PCH), F32)
        for k in range(TOP_K):
            w = w + jnp.where(jio == pos[:, k:k + 1], gate[:, k:k + 1], 0.0)
        f = f + jnp.dot(w.astype(BF16), ys_ref[r * MOE_PCH:(r + 1) * MOE_PCH], preferred_element_type=F32)
    o_ref[...] = _ln_rows(ALPHA * x_ref[...] + f, g_ref[...], b_ref[...])


def _moe_tile_tables(cnt_pad, n_tiles_max):
    nb, ne = cnt_pad.shape
    nch = cnt_pad // MOE_G
    seg_off = jnp.cumsum(cnt_pad, axis=1) - cnt_pad
    base_chunk = (jnp.arange(nb, dtype=I32)[:, None] * MOE_RB + seg_off) // MOE_G
    tiles_e = (jnp.sum(nch, axis=0) + MOE_CPT - 1) // MOE_CPT
    tile_start = jnp.cumsum(tiles_e) - tiles_e
    n_tiles = jnp.sum(tiles_e)
    nch_eb = nch.T
    seg_start = (tile_start * MOE_CPT)[:, None] + jnp.cumsum(nch_eb, axis=1) - nch_eb
    seg_start = seg_start.reshape(-1)
    slots = jnp.arange(n_tiles_max * MOE_CPT, dtype=I32)
    seg = jnp.searchsorted(seg_start, slots, side="right").astype(I32) - 1
    off = slots - seg_start[seg]
    zero_chunk = (MOE_RB - MOE_G) // MOE_G
    src = jnp.where(off < nch_eb.reshape(-1)[seg], base_chunk.T.reshape(-1)[seg] + off, zero_chunk)
    tile_e = jnp.searchsorted(tile_start, jnp.arange(n_tiles_max, dtype=I32), side="right").astype(I32) - 1
    dump_chunk = nb * MOE_RB // MOE_G
    meta = jnp.stack([n_tiles.astype(I32), jnp.int32(zero_chunk), jnp.int32(dump_chunk)])
    return jnp.clip(tile_e, 0, ne - 1), src.astype(I32), meta


def _moe_ln(x, router_w, router_b, w_gu, b_gu, w_down, b_down, ln_g, ln_b):
    n, d = x.shape
    ne, f = N_EXPERTS, w_down.shape[1]
    nb = n // MOE_TB
    assert MOE_RB >= MOE_TB * TOP_K + ne * (MOE_G - 1) + MOE_G
    blk = lambda i: (i, 0)
    blk3 = lambda i: (i, 0, 0)
    fix = lambda i: (0, 0)
    xs, pos, gates, cnt = pl.pallas_call(
        _moe_route_kernel,
        grid=(nb,),
        in_specs=[pl.BlockSpec((MOE_TB, d), blk), pl.BlockSpec((ne, d), fix), pl.BlockSpec((ne, 1), fix)],
        out_specs=[pl.BlockSpec((MOE_RB, d), blk), pl.BlockSpec((1, 8, MOE_TB), blk3),
                   pl.BlockSpec((1, 8, MOE_TB), blk3), pl.BlockSpec((1, ne, LANES), blk3)],
        out_shape=[jax.ShapeDtypeStruct((nb * MOE_RB, d), BF16), jax.ShapeDtypeStruct((nb, 8, MOE_TB), I32),
                   jax.ShapeDtypeStruct((nb, 8, MOE_TB), F32), jax.ShapeDtypeStruct((nb, ne, LANES), I32)],
        compiler_params=_cparams(("parallel",)),
        name="moe_route",
    )(x, router_w.T, router_b.reshape(ne, 1))

    n_tiles_max = nb * MOE_RB // MOE_TM + ne
    tile_e, src, meta = _moe_tile_tables(cnt[:, :, 0], n_tiles_max)
    ys_rows = nb * MOE_RB + 2 * MOE_TM
    ys = pl.pallas_call(
        _moe_expert_kernel,
        grid_spec=pltpu.PrefetchScalarGridSpec(
            num_scalar_prefetch=3,
            grid=(n_tiles_max,),
            in_specs=[pl.BlockSpec(memory_space=pl.ANY),
                      pl.BlockSpec((1, d, 2 * f), lambda t, te, sr, nt: (te[t], 0, 0)),
                      pl.BlockSpec((1, 1, 2 * f), lambda t, te, sr, nt: (te[t], 0, 0)),
                      pl.BlockSpec((1, f, d), lambda t, te, sr, nt: (te[t], 0, 0)),
                      pl.BlockSpec((1, 1, d), lambda t, te, sr, nt: (te[t], 0, 0)),
                      pl.BlockSpec(memory_space=pl.ANY)],
            out_specs=pl.BlockSpec(memory_space=pl.ANY),
            scratch_shapes=[pltpu.VMEM((2, MOE_TM, d), BF16), pltpu.VMEM((2, MOE_TM, d), BF16),
                            pltpu.SemaphoreType.DMA((2,)), pltpu.SemaphoreType.DMA((2,))]),
        out_shape=jax.ShapeDtypeStruct((ys_rows, d), BF16),
        input_output_aliases={8: 0},
        compiler_params=_cparams(("arbitrary",)),
        name="moe_expert",
    )(tile_e, src, meta, xs, w_gu.astype(BF16), b_gu.reshape(ne, 1, 2 * f), w_down.astype(BF16),
      b_down.reshape(ne, 1, d), jnp.zeros((ys_rows, d), BF16))

    pos_t = jnp.swapaxes(pos, 1, 2).reshape(n, 8)
    gate_t = jnp.swapaxes(gates, 1, 2).reshape(n, 8)
    return pl.pallas_call(
        _moe_combine_kernel,
        grid=(nb,),
        in_specs=[pl.BlockSpec((MOE_RB, d), blk), pl.BlockSpec((MOE_TB, 8), blk), pl.BlockSpec((MOE_TB, 8), blk),
                  pl.BlockSpec((MOE_TB, d), blk), pl.BlockSpec((1, d), fix), pl.BlockSpec((1, d), fix)],
        out_specs=pl.BlockSpec((MOE_TB, d), blk),
        out_shape=jax.ShapeDtypeStruct((n, d), F32),
        compiler_params=_cparams(("parallel",)),
        name="moe_combine",
    )(ys, pos_t, gate_t, x, _row2(ln_g), _row2(ln_b))


def kernel(x, ln1_g, ln1_b, ln2_g, ln2_b, gla_w_in, gla_gate_w2, gla_gate_b, gla_norm_g, gla_w_out,
           hy_w_in, hy_b_in, hy_conv_w, hy_conv_b, hy_ffn_w1, hy_ffn_b1, hy_sin_freq, hy_ffn_w2,
           hy_ffn_b2, hy_ffn_w3, hy_filt_bias, hy_w_out, hy_b_out, na_w_in, na_b_in, na_rpb, na_w_out,
           na_b_out, moe_router_w, moe_router_b, moe_w_gu, moe_b_gu, moe_w_down, moe_b_down):
    batch, seq, d = x.shape
    xt = x.reshape(batch * seq, d)
    for i in range(DEPTH):
        m, j = i % N_MIXERS, i // N_MIXERS
        if m == 0:
            xt = _gla_mixer_ln(xt, batch, gla_w_in[j], gla_gate_w2[j], gla_gate_b[j], gla_norm_g[j],
                               gla_w_out[j], ln1_g[i], ln1_b[i])
        elif m == 1:
            xt = _hyena_mixer_ln(xt, batch, hy_w_in[j], hy_b_in[j], hy_conv_w[j], hy_conv_b[j],
                                 hy_ffn_w1[j], hy_ffn_b1[j], hy_sin_freq[j], hy_ffn_w2[j], hy_ffn_b2[j],
                                 hy_ffn_w3[j], hy_filt_bias[j], hy_w_out[j], hy_b_out[j],
                                 ln1_g[i], ln1_b[i])
        else:
            xt = _na_mixer_ln(xt, batch, na_w_in[j], na_b_in[j], na_rpb[j], na_w_out[j], na_b_out[j],
                              ln1_g[i], ln1_b[i])
        xt = _moe_ln(xt, moe_router_w[i], moe_router_b[i], moe_w_gu[i], moe_b_gu[i], moe_w_down[i],
                     moe_b_down[i], ln2_g[i], ln2_b[i])
    return xt.reshape(batch, seq, d)
```

```python
import functools
import math

import jax
import jax.numpy as jnp
from jax import lax
from jax.experimental import pallas as pl
from jax.experimental.pallas import tpu as pltpu

F32, BF16, I32 = jnp.float32, jnp.bfloat16, jnp.int32
HIGHEST = lax.Precision.HIGHEST
NT_DIMS = (((1,), (1,)), ((), ()))
TN_DIMS = (((0,), (0,)), ((), ()))

D_MODEL = 1024
DEPTH = 4
N_MIXERS = 3
GRID_W = 64
ALPHA = (2 * DEPTH) ** 0.25
LN_EPS = 1e-5

GLA_HEADS = 4
GLA_KEY_DIM = 512
GLA_VAL_DIM = 1024
GLA_HEAD_K = 128
GLA_HEAD_V = 256
GLA_GATE_RANK = 16
GLA_TAU = 16.0
GLA_NORM_EPS = 1e-6

HYENA_ORDER = 2
HYENA_POS_EMB = 33
HYENA_FILTER_HIDDEN = 64
HYENA_DECAY_TARGET = 1e-2
HYENA_FAST_DECAY = 0.3
HYENA_SLOW_DECAY = 1.5
HYENA_WINDOW_SHIFT = 0.05

NA_HEAD_DIM = 32
NA_HEADS = 32
NA_WIN_ROWS = 8
NA_WIN_COLS = 16

N_EXPERTS = 32
TOP_K = 4
SWIGLU_ALPHA = 1.702
SWIGLU_LIMIT = 7.0

LANES = 128
BF16_SUBLANES = 16
VMEM_LIMIT = 48 * 1024 * 1024

ROW_TILE = 512
GLA_CHUNK = 64
GLA_SUB = 16
GLA_ROWS = 256
GLA_EXP_CLAMP = 80.0
DFT_TILE = 512
DFT_COLS = 512
NA_KEY_ROWS = 10
NA_PAIRS = 4
NEG = -1e30

MOE_TB = 512
MOE_G = BF16_SUBLANES
MOE_RB = MOE_TB * TOP_K + 512
MOE_PCH = 256
MOE_BLOCK_CHUNKS = MOE_RB // MOE_G
MOE_ZERO_CHUNK = MOE_BLOCK_CHUNKS - 1
MOE_SPARE_CHUNK = MOE_BLOCK_CHUNKS - 2
MOE_TM = 512
MOE_CPT = MOE_TM // MOE_G


def _cparams(sem):
    return pltpu.CompilerParams(dimension_semantics=sem, vmem_limit_bytes=VMEM_LIMIT)


def _ln_rows(y, g, b):
    mu = jnp.mean(y, axis=-1, keepdims=True)
    yc = y - mu
    var = jnp.mean(yc * yc, axis=-1, keepdims=True)
    return yc * lax.rsqrt(var + LN_EPS) * g + b


def _row2(v):
    return v.reshape(1, -1)


def _proj_ln_kernel(a_ref, w_ref, bias_ref, x_ref, g_ref, b_ref, o_ref):
    h = jnp.dot(a_ref[...], w_ref[...], preferred_element_type=F32) + bias_ref[...]
    o_ref[...] = _ln_rows(ALPHA * x_ref[...] + h, g_ref[...], b_ref[...])


def _proj_ln(a, w, bias, x, g, b):
    n, d = x.shape
    row = lambda i: (i, 0)
    fix = lambda i: (0, 0)
    return pl.pallas_call(
        _proj_ln_kernel,
        grid=(n // ROW_TILE,),
        in_specs=[pl.BlockSpec((ROW_TILE, a.shape[1]), row), pl.BlockSpec(w.shape, fix),
                  pl.BlockSpec((1, d), fix), pl.BlockSpec((ROW_TILE, d), row),
                  pl.BlockSpec((1, d), fix), pl.BlockSpec((1, d), fix)],
        out_specs=pl.BlockSpec((ROW_TILE, d), row),
        out_shape=jax.ShapeDtypeStruct((n, d), F32),
        compiler_params=_cparams(("parallel",)),
        name="proj_ln",
    )(a, w, _row2(bias), x, _row2(g), _row2(b))


def _gla_in_kernel(x_ref, w_ref, w2_ref, gb_ref, q_ref, k_ref, v_ref, g_ref, la_ref):
    kd, vd = GLA_KEY_DIM, GLA_VAL_DIM
    proj = jnp.dot(x_ref[...].astype(BF16), w_ref[...], preferred_element_type=F32)
    q_ref[...] = (proj[:, :kd] * GLA_HEAD_K ** -0.5).astype(BF16)
    k_ref[...] = proj[:, kd:2 * kd].astype(BF16)
    v_ref[...] = proj[:, 2 * kd:2 * kd + vd].astype(BF16)
    og = proj[:, 2 * kd + vd:2 * kd + 2 * vd]
    g_ref[...] = (og * jax.nn.sigmoid(og)).astype(BF16)
    glow = proj[:, 2 * kd + 2 * vd:]
    z = jnp.dot(glow, w2_ref[...], precision=HIGHEST, preferred_element_type=F32) + gb_ref[...]
    la_ref[...] = (jnp.minimum(z, 0.0) - jnp.log(1.0 + jnp.exp(-jnp.abs(z)))) * (1.0 / GLA_TAU)


def _gla_in(x, w_in, gate_w2, gate_b):
    n, d = x.shape
    kd, vd, r = GLA_KEY_DIM, GLA_VAL_DIM, GLA_GATE_RANK
    width = 2 * kd + 2 * vd
    w = jnp.pad(w_in, ((0, 0), (0, LANES - 2 * r))).astype(BF16)
    w2 = jnp.zeros((LANES, 2 * kd), F32)
    w2 = w2.at[:r, :kd].set(gate_w2[0]).at[r:2 * r, kd:].set(gate_w2[1])
    gb = gate_b.reshape(1, 2 * kd)
    row = lambda i: (i, 0)
    fix = lambda i: (0, 0)
    return pl.pallas_call(
        _gla_in_kernel,
        grid=(n // ROW_TILE,),
        in_specs=[pl.BlockSpec((ROW_TILE, d), row), pl.BlockSpec((d, width + LANES), fix),
                  pl.BlockSpec((LANES, 2 * kd), fix), pl.BlockSpec((1, 2 * kd), fix)],
        out_specs=[pl.BlockSpec((ROW_TILE, kd), row), pl.BlockSpec((ROW_TILE, kd), row),
                   pl.BlockSpec((ROW_TILE, vd), row), pl.BlockSpec((ROW_TILE, vd), row),
                   pl.BlockSpec((ROW_TILE, 2 * kd), row)],
        out_shape=[jax.ShapeDtypeStruct((n, kd), BF16), jax.ShapeDtypeStruct((n, kd), BF16),
                   jax.ShapeDtypeStruct((n, vd), BF16), jax.ShapeDtypeStruct((n, vd), BF16),
                   jax.ShapeDtypeStruct((n, 2 * kd), F32)],
        compiler_params=_cparams(("parallel",)),
        name="gla_in",
    )(x, w, w2, gb)


def _gla_direction(q_ref, k_ref, v_ref, la_ref, st, reverse):
    rows, cr = q_ref.shape[0], GLA_CHUNK
    nch = rows // cr
    shift = cr.bit_length() - 1
    ri = lax.broadcasted_iota(I32, (rows, rows), 0)
    ci = lax.broadcasted_iota(I32, (rows, rows), 1)
    same_chunk = jnp.right_shift(ri, shift) == jnp.right_shift(ci, shift)
    within = (ci >= ri) if reverse else (ci <= ri)
    cum_mat = jnp.where(same_chunk & within, 1.0, 0.0).astype(BF16)
    la = la_ref[...]
    la_hi = la.astype(BF16)
    rest = la - la_hi.astype(F32)
    la_mid = rest.astype(BF16)
    la_lo = (rest - la_mid.astype(F32)).astype(BF16)
    c_all = (jnp.dot(cum_mat, la_hi, preferred_element_type=F32)
             + jnp.dot(cum_mat, la_mid, preferred_element_type=F32)
             + jnp.dot(cum_mat, la_lo, preferred_element_type=F32))
    rj = lax.broadcasted_iota(I32, (cr, cr), 0)
    cj = lax.broadcasted_iota(I32, (cr, cr), 1)
    mask = (cj > rj) if reverse else (cj <= rj)

    qes, decs, intra, upd = [], [], [], []
    for ch in range(nch):
        lo = ch * cr
        c = c_all[lo:lo + cr]
        qf = q_ref[lo:lo + cr].astype(F32)
        kf = k_ref[lo:lo + cr].astype(F32)
        v = v_ref[lo:lo + cr]
        ctot = c[0:1] if reverse else c[cr - 1:cr]
        qes.append((qf * jnp.exp(c)).astype(BF16))
        decs.append(jnp.exp(ctot))
        blocks = []
        for i in range(cr // GLA_SUB):
            blo, bhi = i * GLA_SUB, (i + 1) * GLA_SUB
            edge = bhi - 1 if reverse else blo
            ref = c[edge:edge + 1] - la[lo + edge:lo + edge + 1]
            qi = (qf[blo:bhi] * jnp.exp(c[blo:bhi] - ref)).astype(BF16)
            ki = (kf * jnp.exp(jnp.minimum(ref - c, GLA_EXP_CLAMP))).astype(BF16)
            blocks.append(lax.dot_general(qi, ki, NT_DIMS, preferred_element_type=F32))
        scores = jnp.where(mask, jnp.concatenate(blocks, axis=0), 0.0).astype(BF16)
        intra.append(jnp.dot(scores, v, preferred_element_type=F32))
        kd = (kf * jnp.exp(ctot - c)).astype(BF16)
        upd.append(lax.dot_general(v, kd, TN_DIMS, preferred_element_type=F32))

    states = [None] * nch
    for ch in (reversed(range(nch)) if reverse else range(nch)):
        states[ch] = st
        st = st * decs[ch] + upd[ch]
    outs = [intra[ch] + lax.dot_general(qes[ch], states[ch].astype(BF16), NT_DIMS, preferred_element_type=F32)
            for ch in range(nch)]
    return outs, st


def _gla_scan_kernel(qf_ref, kf_ref, vf_ref, laf_ref, qb_ref, kb_ref, vb_ref, lab_ref,
                     of_ref, ob_ref, sf_ref, sb_ref):
    @pl.when(pl.program_id(2) == 0)
    def _():
        sf_ref[...] = jnp.zeros_like(sf_ref)
        sb_ref[...] = jnp.zeros_like(sb_ref)

    outs_f, sf = _gla_direction(qf_ref, kf_ref, vf_ref, laf_ref, sf_ref[...], False)
    outs_b, sb = _gla_direction(qb_ref, kb_ref, vb_ref, lab_ref, sb_ref[...], True)
    for ch in range(GLA_ROWS // GLA_CHUNK):
        lo = ch * GLA_CHUNK
        of_ref[lo:lo + GLA_CHUNK] = outs_f[ch].astype(BF16)
        ob_ref[lo:lo + GLA_CHUNK] = outs_b[ch].astype(BF16)
    sf_ref[...] = sf
    sb_ref[...] = sb


def _gla_scan(q, k, v, la, batch):
    n = q.shape[0]
    nblk = n // batch // GLA_ROWS
    hk, hv, nh = GLA_HEAD_K, GLA_HEAD_V, GLA_HEADS
    fwd = lambda b, h, i: (b * nblk + i, h)
    bwd = lambda b, h, i: (b * nblk + nblk - 1 - i, h)
    bwd_la = lambda b, h, i: (b * nblk + nblk - 1 - i, nh + h)
    return pl.pallas_call(
        _gla_scan_kernel,
        grid=(batch, nh, nblk),
        in_specs=[pl.BlockSpec((GLA_ROWS, hk), fwd), pl.BlockSpec((GLA_ROWS, hk), fwd),
                  pl.BlockSpec((GLA_ROWS, hv), fwd), pl.BlockSpec((GLA_ROWS, hk), fwd),
                  pl.BlockSpec((GLA_ROWS, hk), bwd), pl.BlockSpec((GLA_ROWS, hk), bwd),
                  pl.BlockSpec((GLA_ROWS, hv), bwd), pl.BlockSpec((GLA_ROWS, hk), bwd_la)],
        out_specs=[pl.BlockSpec((GLA_ROWS, hv), fwd), pl.BlockSpec((GLA_ROWS, hv), bwd)],
        out_shape=[jax.ShapeDtypeStruct((n, GLA_VAL_DIM), BF16)] * 2,
        scratch_shapes=[pltpu.VMEM((hv, hk), F32), pltpu.VMEM((hv, hk), F32)],
        compiler_params=_cparams(("parallel", "parallel", "arbitrary")),
        name="gla_scan",
    )(q, k, v, la, q, k, v, la)


def _gla_out_kernel(of_ref, ob_ref, gate_ref, ng_ref, w_ref, x_ref, g_ref, b_ref, o_ref):
    o = of_ref[...].astype(F32) + ob_ref[...].astype(F32)
    parts = []
    for h in range(GLA_HEADS):
        oh = o[:, h * GLA_HEAD_V:(h + 1) * GLA_HEAD_V]
        ms = jnp.mean(oh * oh, axis=-1, keepdims=True)
        parts.append(oh * lax.rsqrt(ms + GLA_NORM_EPS))
    on = jnp.concatenate(parts, axis=-1) * ng_ref[...]
    a = (on * gate_ref[...].astype(F32)).astype(BF16)
    h = jnp.dot(a, w_ref[...], preferred_element_type=F32)
    o_ref[...] = _ln_rows(ALPHA * x_ref[...] + h, g_ref[...], b_ref[...])


def _gla_mixer_ln(x, batch, w_in, gate_w2, gate_b, norm_g, w_out, ln_g, ln_b):
    n, d = x.shape
    q, k, v, gate, la = _gla_in(x, w_in, gate_w2, gate_b)
    o_f, o_b = _gla_scan(q, k, v, la, batch)
    row = lambda i: (i, 0)
    fix = lambda i: (0, 0)
    vd = GLA_VAL_DIM
    return pl.pallas_call(
        _gla_out_kernel,
        grid=(n // ROW_TILE,),
        in_specs=[pl.BlockSpec((ROW_TILE, vd), row), pl.BlockSpec((ROW_TILE, vd), row),
                  pl.BlockSpec((ROW_TILE, vd), row), pl.BlockSpec((1, vd), fix),
                  pl.BlockSpec((vd, d), fix), pl.BlockSpec((ROW_TILE, d), row),
                  pl.BlockSpec((1, d), fix), pl.BlockSpec((1, d), fix)],
        out_specs=pl.BlockSpec((ROW_TILE, d), row),
        out_shape=jax.ShapeDtypeStruct((n, d), F32),
        compiler_params=_cparams(("parallel",)),
        name="gla_out",
    )(o_f, o_b, gate, _row2(jnp.tile(norm_g, GLA_HEADS)), w_out.astype(BF16), x,
      _row2(ln_g), _row2(ln_b))


def _hy_in_kernel(x_ref, w_ref, b_ref, cw_ref, cb_ref, o_ref):
    p = jnp.dot(x_ref[...], w_ref[...], preferred_element_type=F32) + b_ref[...]
    rows = p.shape[0]
    ri = lax.broadcasted_iota(I32, p.shape, 0)
    prev = jnp.where(ri == 0, 0.0, pltpu.roll(p, 1, axis=0))
    nxt = jnp.where(ri == rows - 1, 0.0, pltpu.roll(p, rows - 1, axis=0))
    cw = cw_ref[...]
    o_ref[...] = (prev * cw[0:1] + p * cw[1:2] + nxt * cw[2:3] + cb_ref[...]).astype(BF16)


def _hy_in(xb, batch, w_in, b_in, conv_w, conv_b):
    n, d = xb.shape
    seq = n // batch
    width = w_in.shape[1]
    cols = 256
    return pl.pallas_call(
        _hy_in_kernel,
        grid=(batch, width // cols),
        in_specs=[pl.BlockSpec((seq, d), lambda b, j: (b, 0)), pl.BlockSpec((d, cols), lambda b, j: (0, j)),
                  pl.BlockSpec((1, cols), lambda b, j: (0, j)), pl.BlockSpec((3, cols), lambda b, j: (0, j)),
                  pl.BlockSpec((1, cols), lambda b, j: (0, j))],
        out_specs=pl.BlockSpec((seq, cols), lambda b, j: (b, j)),
        out_shape=jax.ShapeDtypeStruct((n, width), BF16),
        compiler_params=_cparams(("parallel", "arbitrary")),
        name="hy_in",
    )(xb, w_in.astype(BF16), _row2(b_in), conv_w, _row2(conv_b))


def _hy_filter_kernel(pe_ref, w1_ref, b1_ref, f_ref, w2_ref, b2_ref, w3_ref, win_ref, hs_ref, hd_ref):
    f = f_ref[...]
    h = jnp.sin(f[0:1] * (jnp.dot(pe_ref[...], w1_ref[...], precision=HIGHEST,
                                  preferred_element_type=F32) + b1_ref[...]))
    h = jnp.sin(f[1:2] * (jnp.dot(h, w2_ref[...], precision=HIGHEST,
                                  preferred_element_type=F32) + b2_ref[...]))
    h = jnp.dot(h, w3_ref[...], precision=HIGHEST, preferred_element_type=F32)
    win = win_ref[...]
    d = win.shape[1]
    rows = win.shape[0]
    t = pl.program_id(0) * rows + lax.broadcasted_iota(I32, (rows, 1), 0)
    for o in range(HYENA_ORDER):
        hf = h[:, 2 * o * d:(2 * o + 1) * d] * win
        hb = jnp.where(t == 0, 0.0, h[:, (2 * o + 1) * d:(2 * o + 2) * d] * win)
        hs_ref[:, o * d:(o + 1) * d] = (hf + hb).astype(BF16)
        hd_ref[:, o * d:(o + 1) * d] = (hf - hb).astype(BF16)


def _hy_spec_kernel(c_ref, s_ref, hs_ref, hd_ref, hr_ref, hi_ref, hn_ref):
    hs = hs_ref[...]
    a1 = jnp.dot(c_ref[...], hs, preferred_element_type=F32)
    a2 = jnp.dot(s_ref[...], hd_ref[...], preferred_element_type=F32)
    a3 = jnp.dot(s_ref[...], hs, preferred_element_type=F32)
    ri = lax.broadcasted_iota(I32, a1.shape, 0) + pl.program_id(0) * a1.shape[0]
    hr_ref[...] = a1
    hi_ref[...] = jnp.where(ri == 0, 0.0, a2)
    hn_ref[...] = jnp.where(ri == 0, a3, a1)


def _dft_consts(seq):
    k = jnp.arange(seq, dtype=I32)
    step = LANES
    assert seq % step == 0
    phase = lambda t: ((k[:, None] * t[None, :]) % (2 * seq)).astype(F32) * (math.pi / seq)
    ang_hi = phase(jnp.arange(seq // step, dtype=I32) * step)[:, :, None]
    ang_lo = phase(jnp.arange(step, dtype=I32))[:, None, :]
    cos = (jnp.cos(ang_hi) * jnp.cos(ang_lo) - jnp.sin(ang_hi) * jnp.sin(ang_lo)).reshape(seq, seq)
    sin = (jnp.sin(ang_hi) * jnp.cos(ang_lo) + jnp.cos(ang_hi) * jnp.sin(ang_lo)).reshape(seq, seq)
    sgn = jnp.where(k % 2 == 0, 1.0, -1.0).astype(F32)
    fwd_i = jnp.where(k[:, None] == 0, sgn[None, :], -sin)
    scale = jnp.where(k == 0, 0.5 / seq, 1.0 / seq).astype(F32)
    inv_r = cos * scale[None, :]
    inv_i = jnp.where(k[None, :] == 0, sgn[:, None] * (0.5 / seq), -sin / seq)
    return cos.astype(BF16), fwd_i.astype(BF16), inv_r.astype(BF16), inv_i.astype(BF16)


def _hy_filters(seq, consts, w1, b1, sin_freq, w2, b2, w3):
    cos, fwd_i = consts[0], consts[1]
    d = w3.shape[-1] // (2 * HYENA_ORDER)
    hid = HYENA_FILTER_HIDDEN
    t = jnp.arange(seq, dtype=F32)
    t_norm = t / max(seq - 1, 1)
    bands = (HYENA_POS_EMB - 1) // 2
    freqs = jnp.linspace(1e-4, bands - 1, bands, dtype=F32)
    ang = (2.0 * math.pi / seq) * t[:, None] * freqs[None, :]
    pe = jnp.concatenate([t_norm[:, None], jnp.cos(ang), -jnp.sin(ang)], axis=-1)
    pe = jnp.pad(pe, ((0, 0), (0, LANES - HYENA_POS_EMB)))
    min_decay = math.log(HYENA_DECAY_TARGET) / HYENA_SLOW_DECAY
    max_decay = math.log(HYENA_DECAY_TARGET) / HYENA_FAST_DECAY
    deltas = jnp.abs(jnp.linspace(min_decay, max_decay, d, dtype=F32))
    window = jnp.exp(-t_norm[:, None] * deltas[None, :]) + HYENA_WINDOW_SHIFT
    padc = LANES - hid
    w1p = jnp.pad(w1, ((0, LANES - HYENA_POS_EMB), (0, padc)))
    w2p = jnp.pad(w2, ((0, padc), (0, padc)))
    w3p = jnp.pad(w3, ((0, padc), (0, 0)))
    rows = 256
    fix = lambda i: (0, 0)
    row = lambda i: (i, 0)
    hs, hd = pl.pallas_call(
        _hy_filter_kernel,
        grid=(seq // rows,),
        in_specs=[pl.BlockSpec((rows, LANES), row), pl.BlockSpec((LANES, LANES), fix),
                  pl.BlockSpec((1, LANES), fix), pl.BlockSpec((2, LANES), fix),
                  pl.BlockSpec((LANES, LANES), fix), pl.BlockSpec((1, LANES), fix),
                  pl.BlockSpec((LANES, 2 * HYENA_ORDER * d), fix), pl.BlockSpec((rows, d), row)],
        out_specs=[pl.BlockSpec((rows, HYENA_ORDER * d), row)] * 2,
        out_shape=[jax.ShapeDtypeStruct((seq, HYENA_ORDER * d), BF16)] * 2,
        compiler_params=_cparams(("parallel",)),
        name="hy_filter",
    )(pe, w1p, _row2(jnp.pad(b1, (0, padc))), jnp.pad(sin_freq, ((0, 0), (0, padc))), w2p,
      _row2(jnp.pad(b2, (0, padc))), w3p, window)
    tf, tn = DFT_TILE, DFT_COLS
    plane = jax.ShapeDtypeStruct((seq, HYENA_ORDER * d), F32)
    return pl.pallas_call(
        _hy_spec_kernel,
        grid=(seq // tf, HYENA_ORDER * d // tn),
        in_specs=[pl.BlockSpec((tf, seq), lambda i, j: (i, 0)), pl.BlockSpec((tf, seq), lambda i, j: (i, 0)),
                  pl.BlockSpec((seq, tn), lambda i, j: (0, j)), pl.BlockSpec((seq, tn), lambda i, j: (0, j))],
        out_specs=[pl.BlockSpec((tf, tn), lambda i, j: (i, j))] * 3,
        out_shape=[plane] * 3,
        compiler_params=_cparams(("parallel", "arbitrary")),
        name="hy_spec",
    )(cos, fwd_i, hs, hd)


def _dft_fwd_kernel(c_ref, s_ref, z_ref, hr_ref, hi_ref, hn_ref, yr_ref, yi_ref):
    z = z_ref[0]
    zr = jnp.dot(c_ref[...], z, preferred_element_type=F32)
    zi = jnp.dot(s_ref[...], z, preferred_element_type=F32)
    hi = hi_ref[...]
    yr_ref[0] = (zr * hr_ref[...] - zi * hi).astype(BF16)
    yi_ref[0] = (zr * hi + zi * hn_ref[...]).astype(BF16)


def _dft_inv_kernel(gr_ref, gi_ref, yr_ref, yi_ref, z_ref, gate_ref, fb_ref, o_ref):
    y = (jnp.dot(gr_ref[...], yr_ref[0], preferred_element_type=F32)
         + jnp.dot(gi_ref[...], yi_ref[0], preferred_element_type=F32))
    z = z_ref[0].astype(F32)
    o_ref[0] = (gate_ref[0].astype(F32) * (y + z * fb_ref[...])).astype(BF16)


def _long_conv_gate(consts, planes, order, z, z_col, gate, gate_col, filt_bias):
    cos, fwd_i, inv_r, inv_i = consts
    batch, seq, _ = z.shape
    d = D_MODEL
    tf, tn = DFT_TILE, DFT_COLS
    nj = d // tn
    mat = pl.BlockSpec((tf, seq), lambda i, b, j: (i, 0))
    spec_plane = pl.BlockSpec((tf, tn), lambda i, b, j: (i, order * nj + j))
    full_cols = lambda off: pl.BlockSpec((1, seq, tn), lambda i, b, j: (b, 0, off + j))
    tile_cols = lambda off: pl.BlockSpec((1, tf, tn), lambda i, b, j: (b, i, off + j))
    spec_shape = jax.ShapeDtypeStruct((batch, seq, d), BF16)
    yr, yi = pl.pallas_call(
        _dft_fwd_kernel,
        grid=(seq // tf, batch, nj),
        in_specs=[mat, mat, full_cols(z_col), spec_plane, spec_plane, spec_plane],
        out_specs=[tile_cols(0), tile_cols(0)],
        out_shape=[spec_shape, spec_shape],
        compiler_params=_cparams(("parallel", "parallel", "arbitrary")),
        name="dft_fwd",
    )(cos, fwd_i, z, planes[0], planes[1], planes[2])
    return pl.pallas_call(
        _dft_inv_kernel,
        grid=(seq // tf, batch, nj),
        in_specs=[mat, mat, full_cols(0), full_cols(0), tile_cols(z_col), tile_cols(gate_col),
                  pl.BlockSpec((1, tn), lambda i, b, j: (0, j))],
        out_specs=tile_cols(0),
        out_shape=spec_shape,
        compiler_params=_cparams(("parallel", "parallel", "arbitrary")),
        name="dft_inv",
    )(inv_r, inv_i, yr, yi, z, gate, _row2(filt_bias))


def _hyena_mixer_ln(x, batch, w_in, b_in, conv_w, conv_b, ffn_w1, ffn_b1, sin_freq, ffn_w2, ffn_b2,
                    ffn_w3, filt_bias, w_out, b_out, ln_g, ln_b):
    n, d = x.shape
    seq = n // batch
    nj = d // DFT_COLS
    consts = _dft_consts(seq)
    planes = _hy_filters(seq, consts, ffn_w1, ffn_b1, sin_freq, ffn_w2, ffn_b2, ffn_w3)
    u = _hy_in(x.astype(BF16), batch, w_in, b_in, conv_w, conv_b).reshape(batch, seq, 3 * d)
    z = _long_conv_gate(consts, planes, 0, u, 0, u, nj, filt_bias[0])
    z = _long_conv_gate(consts, planes, 1, z, 0, u, 2 * nj, filt_bias[1])
    return _proj_ln(z.reshape(n, d), w_out.astype(BF16), b_out, x, ln_g, ln_b)


def _na_in_kernel(x_ref, w_ref, b_ref, q_ref, k_ref, v_ref):
    d = D_MODEL
    p = jnp.dot(x_ref[...].astype(BF16), w_ref[...], preferred_element_type=F32) + b_ref[...]
    q_ref[...] = (p[:, :d] * NA_HEAD_DIM ** -0.5).astype(BF16)
    k_ref[...] = p[:, d:2 * d].astype(BF16)
    v_ref[...] = p[:, 2 * d:].astype(BF16)


def _na_attn_kernel(q_ref, k_ref, vt_ref, bias_ref, o_ref):
    for i in range(NA_PAIRS):
        rp = pl.program_id(2) * NA_PAIRS + i
        o_ref[i * LANES:(i + 1) * LANES] = _na_pair(rp, q_ref[i * LANES:(i + 1) * LANES], k_ref, vt_ref, bias_ref)


def _na_pair(rp, q, k_ref, vt_ref, bias_ref):
    hpg = LANES // NA_HEAD_DIM
    n_rows = GRID_W
    r0 = 2 * rp
    rs0 = jnp.clip(r0 - NA_WIN_ROWS // 2, 0, n_rows - NA_WIN_ROWS)
    rs1 = jnp.clip(r0 + 1 - NA_WIN_ROWS // 2, 0, n_rows - NA_WIN_ROWS)
    start = jnp.minimum(rs0, n_rows - NA_KEY_ROWS)
    tok0 = pl.multiple_of(start * GRID_W, LANES)
    nkeys = NA_KEY_ROWS * GRID_W
    kw = k_ref[pl.ds(tok0, nkeys), :]
    vtw = vt_ref[0, :, pl.ds(tok0, nkeys)]
    lane = lax.broadcasted_iota(I32, (1, LANES), 1)
    sub = lax.broadcasted_iota(I32, (LANES, 1), 0)
    right = lane >= GRID_W
    acc = jnp.zeros((LANES, LANES), F32)
    for h in range(hpg):
        in_head = (lane >= h * NA_HEAD_DIM) & (lane < (h + 1) * NA_HEAD_DIM)
        qh = jnp.where(in_head, q, jnp.zeros_like(q))
        s = lax.dot_general(kw, qh, NT_DIMS, preferred_element_type=F32)
        tiles = []
        for a in range(NA_KEY_ROWS):
            kr = start + a
            dr0 = kr - r0 + (NA_WIN_ROWS - 1)
            bt = bias_ref[h, jnp.clip(dr0, 0, 2 * NA_WIN_ROWS - 1)]
            if a < 2 or a >= NA_WIN_ROWS:
                pen0 = jnp.where((kr >= rs0) & (kr < rs0 + NA_WIN_ROWS), 0.0, NEG)
                pen1 = jnp.where((kr >= rs1) & (kr < rs1 + NA_WIN_ROWS), 0.0, NEG)
                bt = bt + jnp.where(right, pen1, pen0)
            tiles.append(bt)
        s = s + jnp.concatenate(tiles, axis=0)
        m = jnp.max(s, axis=0, keepdims=True)
        p = jnp.exp(s - m)
        inv = 1.0 / jnp.sum(p, axis=0, keepdims=True)
        in_head_rows = (sub >= h * NA_HEAD_DIM) & (sub < (h + 1) * NA_HEAD_DIM)
        pv = jnp.dot(vtw, p.astype(BF16), preferred_element_type=F32)
        acc = acc + jnp.where(in_head_rows, pv * inv, 0.0)
    return acc.T.astype(BF16)


def _na_bias_table(rpb):
    nh = rpb.shape[0]
    kc = jnp.arange(GRID_W)[:, None]
    qc = jnp.arange(GRID_W)[None, :]
    cs = jnp.clip(qc - NA_WIN_COLS // 2, 0, GRID_W - NA_WIN_COLS)
    inwin = (kc >= cs) & (kc < cs + NA_WIN_COLS)
    dc = jnp.clip(kc - qc + (NA_WIN_COLS - 1), 0, 2 * NA_WIN_COLS - 2)
    colb = jnp.where(inwin[None, None], rpb.astype(F32)[:, :, dc], NEG)
    pad = jnp.full((nh, 1, GRID_W, GRID_W), NEG, F32)
    left = jnp.concatenate([colb, pad], axis=1)
    rgt = jnp.concatenate([pad, colb], axis=1)
    return jnp.concatenate([left, rgt], axis=-1)


def _na_mixer_ln(x, batch, w_in, b_in, rpb, w_out, b_out, ln_g, ln_b):
    n, d = x.shape
    seq = n // batch
    assert seq == GRID_W * GRID_W
    row = lambda i: (i, 0)
    fix = lambda i: (0, 0)
    qkv_shape = jax.ShapeDtypeStruct((n, d), BF16)
    q, k, v = pl.pallas_call(
        _na_in_kernel,
        grid=(n // ROW_TILE,),
        in_specs=[pl.BlockSpec((ROW_TILE, d), row), pl.BlockSpec((d, 3 * d), fix), pl.BlockSpec((1, 3 * d), fix)],
        out_specs=[pl.BlockSpec((ROW_TILE, d), row)] * 3,
        out_shape=[qkv_shape] * 3,
        compiler_params=_cparams(("parallel",)),
        name="na_in",
    )(x, w_in.astype(BF16), _row2(b_in))
    vt = jnp.swapaxes(v.reshape(batch, seq, d), 1, 2)
    bias = _na_bias_table(rpb)
    hpg = LANES // NA_HEAD_DIM
    n_pairs = GRID_W // 2
    n_steps = n_pairs // NA_PAIRS
    o = pl.pallas_call(
        _na_attn_kernel,
        grid=(batch, d // LANES, n_steps),
        in_specs=[pl.BlockSpec((NA_PAIRS * LANES, LANES), lambda b, g, r: (b * n_steps + r, g)),
                  pl.BlockSpec((seq, LANES), lambda b, g, r: (b, g)),
                  pl.BlockSpec((1, LANES, seq), lambda b, g, r: (b, g, 0)),
                  pl.BlockSpec((hpg, 2 * NA_WIN_ROWS, GRID_W, LANES), lambda b, g, r: (g, 0, 0, 0))],
        out_specs=pl.BlockSpec((NA_PAIRS * LANES, LANES), lambda b, g, r: (b * n_steps + r, g)),
        out_shape=qkv_shape,
        compiler_params=_cparams(("parallel", "parallel", "arbitrary")),
        name="na_attn",
    )(q, k, vt, bias)
    return _proj_ln(o, w_out.astype(BF16), b_out, x, ln_g, ln_b)


def _moe_route_kernel(x_ref, wr_ref, rb_ref, xs_ref, pos_ref, gate_ref, cnt_ref):
    ne, tb = N_EXPERTS, MOE_TB
    x = x_ref[...]
    logits = lax.dot_general(wr_ref[...], x, NT_DIMS, precision=HIGHEST,
                             preferred_element_type=F32) + rb_ref[...]
    eio = lax.broadcasted_iota(I32, (ne, tb), 0)
    work = logits
    vals, hots = [], []
    for _ in range(TOP_K):
        m = jnp.max(work, axis=0, keepdims=True)
        idx = jnp.min(jnp.where(work == m, eio, ne), axis=0, keepdims=True)
        hot = eio == idx
        vals.append(m)
        hots.append(hot)
        work = jnp.where(hot, -jnp.inf, work)
    exps = [jnp.exp(v - vals[0]) for v in vals]
    denom = exps[0] + exps[1] + exps[2] + exps[3]
    sel = jnp.zeros((ne, tb), F32)
    for hot in hots:
        sel = sel + jnp.where(hot, 1.0, 0.0)
    earlier = (lax.broadcasted_iota(I32, (tb, tb), 0) < lax.broadcasted_iota(I32, (tb, tb), 1))
    rank = jnp.dot(sel.astype(BF16), jnp.where(earlier, 1.0, 0.0).astype(BF16),
                   preferred_element_type=F32)
    cnt = jnp.sum(sel, axis=1, keepdims=True)
    cnt_pad = jnp.ceil(cnt * (1.0 / MOE_G)) * MOE_G
    below = (lax.broadcasted_iota(I32, (ne, ne), 1) < lax.broadcasted_iota(I32, (ne, ne), 0))
    seg_off = jnp.dot(jnp.where(below, 1.0, 0.0), jnp.broadcast_to(cnt_pad, (ne, tb)),
                      precision=HIGHEST, preferred_element_type=F32)
    slot = seg_off + rank
    pos = [jnp.sum(jnp.where(hot, slot, 0.0), axis=0, keepdims=True).astype(I32) for hot in hots]
    zero_row = jnp.zeros((1, tb), I32)
    pos_ref[0] = jnp.concatenate(pos + [zero_row] * (8 - TOP_K), axis=0)
    gate_ref[0] = jnp.concatenate([e / denom for e in exps] + [zero_row.astype(F32)] * (8 - TOP_K), axis=0)
    cnt_ref[0] = jnp.broadcast_to(cnt_pad, (ne, LANES)).astype(I32)
    xb = x.astype(BF16)
    local = lax.broadcasted_iota(I32, (MOE_PCH, tb), 0).astype(F32).astype(BF16)
    one, zero = jnp.ones((), BF16), jnp.zeros((), BF16)
    for r in range(MOE_RB // MOE_PCH):
        onehot = None
        for p in pos:
            hit = jnp.where(local == (p - r * MOE_PCH).astype(F32).astype(BF16), one, zero)
            onehot = hit if onehot is None else onehot + hit
        xs_ref[r * MOE_PCH:(r + 1) * MOE_PCH] = jnp.dot(
            onehot, xb, preferred_element_type=F32).astype(BF16)


def _moe_expert_kernel(te_ref, src_ref, nt_ref, xs_hbm, wgu_ref, bgu_ref, wd_ref, bd_ref,
                       ys_hbm, xbuf, ybuf, sem_in, sem_out, *, n_blocks):
    del te_ref
    t = pl.program_id(0)
    nt = nt_ref[0]
    s = t % 2

    def in_copy(tile, slot, c):
        chunk = src_ref[tile * MOE_CPT + c]
        return pltpu.make_async_copy(
            xs_hbm.at[pl.ds(pl.multiple_of(chunk * MOE_G, MOE_G), MOE_G)],
            xbuf.at[slot, pl.ds(c * MOE_G, MOE_G)], sem_in.at[slot])

    def out_copy(tile, slot, c):
        chunk = src_ref[tile * MOE_CPT + c]
        spare = ((slot * MOE_CPT + c) % n_blocks) * MOE_BLOCK_CHUNKS + MOE_SPARE_CHUNK
        chunk = jnp.where(chunk == MOE_ZERO_CHUNK, spare, chunk)
        return pltpu.make_async_copy(
            ybuf.at[slot, pl.ds(c * MOE_G, MOE_G)],
            ys_hbm.at[pl.ds(pl.multiple_of(chunk * MOE_G, MOE_G), MOE_G)], sem_out.at[slot])

    def start_in(tile, slot):
        for c in range(MOE_CPT):
            in_copy(tile, slot, c).start()

    def wait_out(tile, slot):
        for c in range(MOE_CPT):
            out_copy(tile, slot, c).wait()

    @pl.when(t == 0)
    def _():
        start_in(0, 0)

    @pl.when(t + 1 < nt)
    def _():
        start_in(t + 1, 1 - s)

    @pl.when(t < nt)
    def _():
        for c in range(MOE_CPT):
            in_copy(t, s, c).wait()
        f = wd_ref.shape[1]
        h = jnp.dot(xbuf[s], wgu_ref[0], preferred_element_type=F32) + bgu_ref[0]
        glu = jnp.minimum(h[:, :f], SWIGLU_LIMIT)
        lin = jnp.clip(h[:, f:], -SWIGLU_LIMIT, SWIGLU_LIMIT)
        act = (glu * jax.nn.sigmoid(SWIGLU_ALPHA * glu) * (lin + 1.0)).astype(BF16)
        y = jnp.dot(act, wd_ref[0], preferred_element_type=F32) + bd_ref[0]

        @pl.when(t >= 2)
        def _():
            wait_out(t - 2, s)

        ybuf[s] = y.astype(BF16)
        for c in range(MOE_CPT):
            out_copy(t, s, c).start()

        @pl.when(t == nt - 1)
        def _():
            wait_out(t, s)

            @pl.when(t >= 1)
            def _():
                wait_out(t - 1, 1 - s)


def _moe_combine_kernel(ys_ref, pos_ref, gate_ref, x_ref, g_ref, b_ref, o_ref):
    tb = MOE_TB
    pos = pos_ref[...]
    gate = gate_ref[...].astype(BF16)
    local = lax.broadcasted_iota(I32, (tb, MOE_PCH), 1).astype(F32).astype(BF16)
    zero = jnp.zeros((), BF16)
    f = jnp.zeros((tb, D_MODEL), F32)
    for r in range(MOE_RB // MOE_PCH):
        w = None
        for k in range(TOP_K):
            pk = (pos[:, k:k + 1] - r * MOE_PCH).astype(F32).astype(BF16)
            wk = jnp.where(local == pk, gate[:, k:k + 1], zero)
            w = wk if w is None else w + wk
        f = f + jnp.dot(w, ys_ref[r * MOE_PCH:(r + 1) * MOE_PCH], preferred_element_type=F32)
    o_ref[...] = _ln_rows(ALPHA * x_ref[...] + f, g_ref[...], b_ref[...])


def _moe_tile_tables(cnt_pad, n_tiles_max):
    nb, ne = cnt_pad.shape
    nch_eb = (cnt_pad // MOE_G).T
    seg_off = jnp.cumsum(cnt_pad, axis=1) - cnt_pad
    base_eb = ((jnp.arange(nb, dtype=I32)[:, None] * MOE_RB + seg_off) // MOE_G).T
    first_eb = jnp.cumsum(nch_eb, axis=1) - nch_eb
    tiles_e = (jnp.sum(nch_eb, axis=1) + MOE_CPT - 1) // MOE_CPT
    tile_end = jnp.cumsum(tiles_e)
    n_tiles = tile_end[-1]
    tau = jnp.arange(n_tiles_max, dtype=I32)
    tile_e = jnp.minimum(jnp.sum(tau[:, None] >= tile_end[None, :], axis=1), ne - 1).astype(I32)
    of_tile = tile_e[:, None] == jnp.arange(ne, dtype=I32)[None, :]
    pick = lambda tab: jnp.sum(jnp.where(of_tile[:, :, None], tab[None], 0), axis=1)
    tile_first = tau - jnp.sum(jnp.where(of_tile, (tile_end - tiles_e)[None, :], 0), axis=1)
    k = (tile_first * MOE_CPT)[:, None] + jnp.arange(MOE_CPT, dtype=I32)[None, :]
    first, count, base = pick(first_eb), pick(nch_eb), pick(base_eb)
    inside = (k[:, :, None] >= first[:, None, :]) & (k[:, :, None] < (first + count)[:, None, :])
    src = jnp.sum(jnp.where(inside, (base - first)[:, None, :] + k[:, :, None], 0), axis=2)
    src = jnp.where(jnp.any(inside, axis=2) & (tau < n_tiles)[:, None], src, MOE_ZERO_CHUNK)
    return tile_e, src.reshape(-1).astype(I32), n_tiles.astype(I32).reshape(1)


def _moe_ln(x, router_w, router_b, w_gu, b_gu, w_down, b_down, ln_g, ln_b):
    n, d = x.shape
    ne, f = N_EXPERTS, w_down.shape[1]
    nb = n // MOE_TB
    assert MOE_RB >= MOE_TB * TOP_K + ne * (MOE_G - 1) + 2 * MOE_G
    blk = lambda i: (i, 0)
    blk3 = lambda i: (i, 0, 0)
    fix = lambda i: (0, 0)
    slot_rows = nb * MOE_RB
    xs, pos, gates, cnt = pl.pallas_call(
        _moe_route_kernel,
        grid=(nb,),
        in_specs=[pl.BlockSpec((MOE_TB, d), blk), pl.BlockSpec((ne, d), fix), pl.BlockSpec((ne, 1), fix)],
        out_specs=[pl.BlockSpec((MOE_RB, d), blk), pl.BlockSpec((1, 8, MOE_TB), blk3),
                   pl.BlockSpec((1, 8, MOE_TB), blk3), pl.BlockSpec((1, ne, LANES), blk3)],
        out_shape=[jax.ShapeDtypeStruct((slot_rows, d), BF16), jax.ShapeDtypeStruct((nb, 8, MOE_TB), I32),
                   jax.ShapeDtypeStruct((nb, 8, MOE_TB), F32), jax.ShapeDtypeStruct((nb, ne, LANES), I32)],
        compiler_params=_cparams(("parallel",)),
        name="moe_route",
    )(x, router_w.T, router_b.reshape(ne, 1))

    n_tiles_max = nb * MOE_RB // MOE_TM + ne
    tile_e, src, n_tiles = _moe_tile_tables(cnt[:, :, 0], n_tiles_max)
    ys = pl.pallas_call(
        functools.partial(_moe_expert_kernel, n_blocks=nb),
        grid_spec=pltpu.PrefetchScalarGridSpec(
            num_scalar_prefetch=3,
            grid=(n_tiles_max,),
            in_specs=[pl.BlockSpec(memory_space=pl.ANY),
                      pl.BlockSpec((1, d, 2 * f), lambda t, te, sr, nt: (te[t], 0, 0)),
                      pl.BlockSpec((1, 1, 2 * f), lambda t, te, sr, nt: (te[t], 0, 0)),
                      pl.BlockSpec((1, f, d), lambda t, te, sr, nt: (te[t], 0, 0)),
                      pl.BlockSpec((1, 1, d), lambda t, te, sr, nt: (te[t], 0, 0))],
            out_specs=pl.BlockSpec(memory_space=pl.ANY),
            scratch_shapes=[pltpu.VMEM((2, MOE_TM, d), BF16), pltpu.VMEM((2, MOE_TM, d), BF16),
                            pltpu.SemaphoreType.DMA((2,)), pltpu.SemaphoreType.DMA((2,))]),
        out_shape=jax.ShapeDtypeStruct((slot_rows, d), BF16),
        input_output_aliases={3: 0},
        compiler_params=_cparams(("arbitrary",)),
        name="moe_expert",
    )(tile_e, src, n_tiles, xs, w_gu.astype(BF16), b_gu.reshape(ne, 1, 2 * f), w_down.astype(BF16),
      b_down.reshape(ne, 1, d))

    pos_t = jnp.swapaxes(pos, 1, 2).reshape(n, 8)
    gate_t = jnp.swapaxes(gates, 1, 2).reshape(n, 8)
    return pl.pallas_call(
        _moe_combine_kernel,
        grid=(nb,),
        in_specs=[pl.BlockSpec((MOE_RB, d), blk), pl.BlockSpec((MOE_TB, 8), blk), pl.BlockSpec((MOE_TB, 8), blk),
                  pl.BlockSpec((MOE_TB, d), blk), pl.BlockSpec((1, d), fix), pl.BlockSpec((1, d), fix)],
        out_specs=pl.BlockSpec((MOE_TB, d), blk),
        out_shape=jax.ShapeDtypeStruct((n, d), F32),
        compiler_params=_cparams(("parallel",)),
        name="moe_combine",
    )(ys, pos_t, gate_t, x, _row2(ln_g), _row2(ln_b))


def kernel(x, ln1_g, ln1_b, ln2_g, ln2_b, gla_w_in, gla_gate_w2, gla_gate_b, gla_norm_g, gla_w_out,
           hy_w_in, hy_b_in, hy_conv_w, hy_conv_b, hy_ffn_w1, hy_ffn_b1, hy_sin_freq, hy_ffn_w2,
           hy_ffn_b2, hy_ffn_w3, hy_filt_bias, hy_w_out, hy_b_out, na_w_in, na_b_in, na_rpb, na_w_out,
           na_b_out, moe_router_w, moe_router_b, moe_w_gu, moe_b_gu, moe_w_down, moe_b_down):
    batch, seq, d = x.shape
    xt = x.reshape(batch * seq, d)
    for i in range(DEPTH):
        m, j = i % N_MIXERS, i // N_MIXERS
        if m == 0:
            xt = _gla_mixer_ln(xt, batch, gla_w_in[j], gla_gate_w2[j], gla_gate_b[j], gla_norm_g[j],
                               gla_w_out[j], ln1_g[i], ln1_b[i])
        elif m == 1:
            xt = _hyena_mixer_ln(xt, batch, hy_w_in[j], hy_b_in[j], hy_conv_w[j], hy_conv_b[j],
                                 hy_ffn_w1[j], hy_ffn_b1[j], hy_sin_freq[j], hy_ffn_w2[j], hy_ffn_b2[j],
                                 hy_ffn_w3[j], hy_filt_bias[j], hy_w_out[j], hy_b_out[j],
                                 ln1_g[i], ln1_b[i])
        else:
            xt = _na_mixer_ln(xt, batch, na_w_in[j], na_b_in[j], na_rpb[j], na_w_out[j], na_b_out[j],
                              ln1_g[i], ln1_b[i])
        xt = _moe_ln(xt, moe_router_w[i], moe_router_b[i], moe_w_gu[i], moe_b_gu[i], moe_w_down[i],
                     moe_b_down[i], ln2_g[i], ln2_b[i])
    return xt.reshape(batch, seq, d)
```

```python
import functools
import math

import jax
import jax.numpy as jnp
from jax import lax
from jax.experimental import pallas as pl
from jax.experimental.pallas import tpu as pltpu

F32, BF16, I32 = jnp.float32, jnp.bfloat16, jnp.int32
HIGHEST = lax.Precision.HIGHEST
NT_DIMS = (((1,), (1,)), ((), ()))
TN_DIMS = (((0,), (0,)), ((), ()))

D_MODEL = 1024
DEPTH = 4
N_MIXERS = 3
GRID_W = 64
ALPHA = (2 * DEPTH) ** 0.25
LN_EPS = 1e-5

GLA_HEADS = 4
GLA_KEY_DIM = 512
GLA_VAL_DIM = 1024
GLA_HEAD_K = 128
GLA_HEAD_V = 256
GLA_GATE_RANK = 16
GLA_TAU = 16.0
GLA_NORM_EPS = 1e-6

HYENA_ORDER = 2
HYENA_POS_EMB = 33
HYENA_FILTER_HIDDEN = 64
HYENA_DECAY_TARGET = 1e-2
HYENA_FAST_DECAY = 0.3
HYENA_SLOW_DECAY = 1.5
HYENA_WINDOW_SHIFT = 0.05

NA_HEAD_DIM = 32
NA_HEADS = 32
NA_WIN_ROWS = 8
NA_WIN_COLS = 16

N_EXPERTS = 32
TOP_K = 4
SWIGLU_ALPHA = 1.702
SWIGLU_LIMIT = 7.0

LANES = 128
BF16_SUBLANES = 16
VMEM_LIMIT = 48 * 1024 * 1024
MOE_EXPERT_VMEM = 56 * 1024 * 1024

ROW_TILE = 512
GLA_CHUNK = 64
GLA_SUB = 16
GLA_ROWS = 512
GLA_EXP_CLAMP = 80.0
DFT_TILE = 512
DFT_COLS = 512
DFT_SUBCOLS = 256
NA_KEY_ROWS = 10
NA_PAIRS = 4
NEG = -1e30

MOE_TB = 512
MOE_G = BF16_SUBLANES
MOE_RB = MOE_TB * TOP_K + 512
MOE_PCH = 256
MOE_BLOCK_CHUNKS = MOE_RB // MOE_G
MOE_ZERO_CHUNK = MOE_BLOCK_CHUNKS - 1
MOE_SPARE_CHUNK = MOE_BLOCK_CHUNKS - 2
MOE_TM = 512
MOE_CPT = MOE_TM // MOE_G


def _cparams(sem):
    return pltpu.CompilerParams(dimension_semantics=sem, vmem_limit_bytes=VMEM_LIMIT)


def _ln_rows(y, g, b):
    mu = jnp.mean(y, axis=-1, keepdims=True)
    yc = y - mu
    var = jnp.mean(yc * yc, axis=-1, keepdims=True)
    return yc * lax.rsqrt(var + LN_EPS) * g + b


def _row2(v):
    return v.reshape(1, -1)


def _split_bf16(a):
    hi = a.astype(BF16)
    return hi, (a - hi.astype(F32)).astype(BF16)


def _dot_split(a, b_hi, b_lo, dims):
    a_hi, a_lo = _split_bf16(a)
    if dims is None:
        mm = lambda p, q: jnp.dot(p, q, preferred_element_type=F32)
    else:
        mm = lambda p, q: lax.dot_general(p, q, dims, preferred_element_type=F32)
    return mm(a_hi, b_hi) + mm(a_lo, b_hi) + mm(a_hi, b_lo)


def _proj_ln_kernel(a_ref, w_ref, bias_ref, x_ref, g_ref, b_ref, o_ref):
    h = jnp.dot(a_ref[...], w_ref[...], preferred_element_type=F32) + bias_ref[...]
    o_ref[...] = _ln_rows(ALPHA * x_ref[...] + h, g_ref[...], b_ref[...])


def _proj_ln(a, w, bias, x, g, b):
    n, d = x.shape
    row = lambda i: (i, 0)
    fix = lambda i: (0, 0)
    return pl.pallas_call(
        _proj_ln_kernel,
        grid=(n // ROW_TILE,),
        in_specs=[pl.BlockSpec((ROW_TILE, a.shape[1]), row), pl.BlockSpec(w.shape, fix),
                  pl.BlockSpec((1, d), fix), pl.BlockSpec((ROW_TILE, d), row),
                  pl.BlockSpec((1, d), fix), pl.BlockSpec((1, d), fix)],
        out_specs=pl.BlockSpec((ROW_TILE, d), row),
        out_shape=jax.ShapeDtypeStruct((n, d), F32),
        compiler_params=_cparams(("parallel",)),
        name="proj_ln",
    )(a, w, _row2(bias), x, _row2(g), _row2(b))


def _gla_in_kernel(x_ref, w_ref, w2_ref, gb_ref, q_ref, k_ref, v_ref, g_ref, la_ref):
    kd, vd = GLA_KEY_DIM, GLA_VAL_DIM
    xb = x_ref[...].astype(BF16)
    proj = lambda lo, hi: jnp.dot(xb, w_ref[:, lo:hi], preferred_element_type=F32)
    glow = proj(2 * kd + 2 * vd, 2 * kd + 2 * vd + LANES)
    z = _dot_split(glow, w2_ref[0], w2_ref[1], None) + gb_ref[...]
    la_ref[...] = (jnp.minimum(z, 0.0) - jnp.log(1.0 + jnp.exp(-jnp.abs(z)))) * (1.0 / GLA_TAU)
    og = proj(2 * kd + vd, 2 * kd + 2 * vd)
    g_ref[...] = (og * jax.nn.sigmoid(og)).astype(BF16)
    q_ref[...] = (proj(0, kd) * GLA_HEAD_K ** -0.5).astype(BF16)
    k_ref[...] = proj(kd, 2 * kd).astype(BF16)
    v_ref[...] = proj(2 * kd, 2 * kd + vd).astype(BF16)


def _gla_in(x, w_in, gate_w2, gate_b):
    n, d = x.shape
    kd, vd, r = GLA_KEY_DIM, GLA_VAL_DIM, GLA_GATE_RANK
    width = 2 * kd + 2 * vd
    w = jnp.pad(w_in, ((0, 0), (0, LANES - 2 * r))).astype(BF16)
    w2 = jnp.zeros((LANES, 2 * kd), F32)
    w2 = w2.at[:r, :kd].set(gate_w2[0]).at[r:2 * r, kd:].set(gate_w2[1])
    w2 = jnp.stack(_split_bf16(w2))
    gb = gate_b.reshape(1, 2 * kd)
    row = lambda i: (i, 0)
    fix = lambda i: (0, 0)
    return pl.pallas_call(
        _gla_in_kernel,
        grid=(n // ROW_TILE,),
        in_specs=[pl.BlockSpec((ROW_TILE, d), row), pl.BlockSpec((d, width + LANES), fix),
                  pl.BlockSpec((2, LANES, 2 * kd), lambda i: (0, 0, 0)), pl.BlockSpec((1, 2 * kd), fix)],
        out_specs=[pl.BlockSpec((ROW_TILE, kd), row), pl.BlockSpec((ROW_TILE, kd), row),
                   pl.BlockSpec((ROW_TILE, vd), row), pl.BlockSpec((ROW_TILE, vd), row),
                   pl.BlockSpec((ROW_TILE, 2 * kd), row)],
        out_shape=[jax.ShapeDtypeStruct((n, kd), BF16), jax.ShapeDtypeStruct((n, kd), BF16),
                   jax.ShapeDtypeStruct((n, vd), BF16), jax.ShapeDtypeStruct((n, vd), BF16),
                   jax.ShapeDtypeStruct((n, 2 * kd), F32)],
        compiler_params=_cparams(("parallel",)),
        name="gla_in",
    )(x, w, w2, gb)


def _gla_direction(q_ref, k_ref, v_ref, la_ref, st, reverse):
    rows, cr = q_ref.shape[0], GLA_CHUNK
    nch = rows // cr
    shift = cr.bit_length() - 1
    ri = lax.broadcasted_iota(I32, (rows, rows), 0)
    ci = lax.broadcasted_iota(I32, (rows, rows), 1)
    same_chunk = jnp.right_shift(ri, shift) == jnp.right_shift(ci, shift)
    within = (ci >= ri) if reverse else (ci <= ri)
    cum_mat = jnp.where(same_chunk & within, 1.0, 0.0).astype(BF16)
    la = la_ref[...]
    la_hi = la.astype(BF16)
    rest = la - la_hi.astype(F32)
    la_mid = rest.astype(BF16)
    la_lo = (rest - la_mid.astype(F32)).astype(BF16)
    c_all = (jnp.dot(cum_mat, la_hi, preferred_element_type=F32)
             + jnp.dot(cum_mat, la_mid, preferred_element_type=F32)
             + jnp.dot(cum_mat, la_lo, preferred_element_type=F32))
    rj = lax.broadcasted_iota(I32, (cr, cr), 0)
    cj = lax.broadcasted_iota(I32, (cr, cr), 1)
    mask = (cj > rj) if reverse else (cj <= rj)

    qes, decs, intra, upd = [], [], [], []
    for ch in range(nch):
        lo = ch * cr
        c = c_all[lo:lo + cr]
        qf = q_ref[lo:lo + cr].astype(F32)
        kf = k_ref[lo:lo + cr].astype(F32)
        v = v_ref[lo:lo + cr]
        ctot = c[0:1] if reverse else c[cr - 1:cr]
        qes.append((qf * jnp.exp(c)).astype(BF16))
        decs.append(jnp.exp(ctot))
        blocks = []
        for i in range(cr // GLA_SUB):
            blo, bhi = i * GLA_SUB, (i + 1) * GLA_SUB
            edge = bhi - 1 if reverse else blo
            ref = c[edge:edge + 1] - la[lo + edge:lo + edge + 1]
            qi = (qf[blo:bhi] * jnp.exp(c[blo:bhi] - ref)).astype(BF16)
            ki = (kf * jnp.exp(jnp.minimum(ref - c, GLA_EXP_CLAMP))).astype(BF16)
            blocks.append(lax.dot_general(qi, ki, NT_DIMS, preferred_element_type=F32))
        scores = jnp.where(mask, jnp.concatenate(blocks, axis=0), 0.0).astype(BF16)
        intra.append(jnp.dot(scores, v, preferred_element_type=F32))
        kd = (kf * jnp.exp(ctot - c)).astype(BF16)
        upd.append(lax.dot_general(v, kd, TN_DIMS, preferred_element_type=F32))

    states = [None] * nch
    for ch in (reversed(range(nch)) if reverse else range(nch)):
        states[ch] = st
        st = st * decs[ch] + upd[ch]
    outs = [intra[ch] + lax.dot_general(qes[ch], states[ch].astype(BF16), NT_DIMS, preferred_element_type=F32)
            for ch in range(nch)]
    return outs, st


def _gla_scan_kernel(qf_ref, kf_ref, vf_ref, laf_ref, qb_ref, kb_ref, vb_ref, lab_ref,
                     of_ref, ob_ref, sf_ref, sb_ref):
    @pl.when(pl.program_id(2) == 0)
    def _():
        sf_ref[...] = jnp.zeros_like(sf_ref)
        sb_ref[...] = jnp.zeros_like(sb_ref)

    outs_f, sf = _gla_direction(qf_ref, kf_ref, vf_ref, laf_ref, sf_ref[...], False)
    outs_b, sb = _gla_direction(qb_ref, kb_ref, vb_ref, lab_ref, sb_ref[...], True)
    for ch in range(GLA_ROWS // GLA_CHUNK):
        lo = ch * GLA_CHUNK
        of_ref[lo:lo + GLA_CHUNK] = outs_f[ch].astype(BF16)
        ob_ref[lo:lo + GLA_CHUNK] = outs_b[ch].astype(BF16)
    sf_ref[...] = sf
    sb_ref[...] = sb


def _gla_scan(q, k, v, la, batch):
    n = q.shape[0]
    nblk = n // batch // GLA_ROWS
    hk, hv, nh = GLA_HEAD_K, GLA_HEAD_V, GLA_HEADS
    fwd = lambda b, h, i: (b * nblk + i, h)
    bwd = lambda b, h, i: (b * nblk + nblk - 1 - i, h)
    bwd_la = lambda b, h, i: (b * nblk + nblk - 1 - i, nh + h)
    return pl.pallas_call(
        _gla_scan_kernel,
        grid=(batch, nh, nblk),
        in_specs=[pl.BlockSpec((GLA_ROWS, hk), fwd), pl.BlockSpec((GLA_ROWS, hk), fwd),
                  pl.BlockSpec((GLA_ROWS, hv), fwd), pl.BlockSpec((GLA_ROWS, hk), fwd),
                  pl.BlockSpec((GLA_ROWS, hk), bwd), pl.BlockSpec((GLA_ROWS, hk), bwd),
                  pl.BlockSpec((GLA_ROWS, hv), bwd), pl.BlockSpec((GLA_ROWS, hk), bwd_la)],
        out_specs=[pl.BlockSpec((GLA_ROWS, hv), fwd), pl.BlockSpec((GLA_ROWS, hv), bwd)],
        out_shape=[jax.ShapeDtypeStruct((n, GLA_VAL_DIM), BF16)] * 2,
        scratch_shapes=[pltpu.VMEM((hv, hk), F32), pltpu.VMEM((hv, hk), F32)],
        compiler_params=_cparams(("parallel", "parallel", "arbitrary")),
        name="gla_scan",
    )(q, k, v, la, q, k, v, la)


def _gla_out_kernel(of_ref, ob_ref, gate_ref, ng_ref, w_ref, x_ref, g_ref, b_ref, o_ref):
    o = of_ref[...].astype(F32) + ob_ref[...].astype(F32)
    parts = []
    for h in range(GLA_HEADS):
        oh = o[:, h * GLA_HEAD_V:(h + 1) * GLA_HEAD_V]
        ms = jnp.mean(oh * oh, axis=-1, keepdims=True)
        parts.append(oh * lax.rsqrt(ms + GLA_NORM_EPS))
    on = jnp.concatenate(parts, axis=-1) * ng_ref[...]
    a = (on * gate_ref[...].astype(F32)).astype(BF16)
    h = jnp.dot(a, w_ref[...], preferred_element_type=F32)
    o_ref[...] = _ln_rows(ALPHA * x_ref[...] + h, g_ref[...], b_ref[...])


def _gla_mixer_ln(x, batch, w_in, gate_w2, gate_b, norm_g, w_out, ln_g, ln_b):
    n, d = x.shape
    q, k, v, gate, la = _gla_in(x, w_in, gate_w2, gate_b)
    o_f, o_b = _gla_scan(q, k, v, la, batch)
    row = lambda i: (i, 0)
    fix = lambda i: (0, 0)
    vd = GLA_VAL_DIM
    return pl.pallas_call(
        _gla_out_kernel,
        grid=(n // ROW_TILE,),
        in_specs=[pl.BlockSpec((ROW_TILE, vd), row), pl.BlockSpec((ROW_TILE, vd), row),
                  pl.BlockSpec((ROW_TILE, vd), row), pl.BlockSpec((1, vd), fix),
                  pl.BlockSpec((vd, d), fix), pl.BlockSpec((ROW_TILE, d), row),
                  pl.BlockSpec((1, d), fix), pl.BlockSpec((1, d), fix)],
        out_specs=pl.BlockSpec((ROW_TILE, d), row),
        out_shape=jax.ShapeDtypeStruct((n, d), F32),
        compiler_params=_cparams(("parallel",)),
        name="gla_out",
    )(o_f, o_b, gate, _row2(jnp.tile(norm_g, GLA_HEADS)), w_out.astype(BF16), x,
      _row2(ln_g), _row2(ln_b))


def _hy_in_kernel(x_ref, w_ref, b_ref, cw_ref, cb_ref, o_ref):
    p = jnp.dot(x_ref[...], w_ref[...], preferred_element_type=F32) + b_ref[...]
    rows = p.shape[0]
    ri = lax.broadcasted_iota(I32, p.shape, 0)
    prev = jnp.where(ri == 0, 0.0, pltpu.roll(p, 1, axis=0))
    nxt = jnp.where(ri == rows - 1, 0.0, pltpu.roll(p, rows - 1, axis=0))
    cw = cw_ref[...]
    o_ref[...] = (prev * cw[0:1] + p * cw[1:2] + nxt * cw[2:3] + cb_ref[...]).astype(BF16)


def _hy_in(xb, batch, w_in, b_in, conv_w, conv_b):
    n, d = xb.shape
    seq = n // batch
    width = w_in.shape[1]
    cols = 256
    return pl.pallas_call(
        _hy_in_kernel,
        grid=(batch, width // cols),
        in_specs=[pl.BlockSpec((seq, d), lambda b, j: (b, 0)), pl.BlockSpec((d, cols), lambda b, j: (0, j)),
                  pl.BlockSpec((1, cols), lambda b, j: (0, j)), pl.BlockSpec((3, cols), lambda b, j: (0, j)),
                  pl.BlockSpec((1, cols), lambda b, j: (0, j))],
        out_specs=pl.BlockSpec((seq, cols), lambda b, j: (b, j)),
        out_shape=jax.ShapeDtypeStruct((n, width), BF16),
        compiler_params=_cparams(("parallel", "arbitrary")),
        name="hy_in",
    )(xb, w_in.astype(BF16), _row2(b_in), conv_w, _row2(conv_b))


def _hy_filter_kernel(pe_ref, w1_ref, b1_ref, f_ref, w2_ref, b2_ref, w3_ref, win_ref, hs_ref, hd_ref):
    f = f_ref[...]
    h = jnp.sin(f[0:1] * (jnp.dot(pe_ref[...], w1_ref[...], precision=HIGHEST,
                                  preferred_element_type=F32) + b1_ref[...]))
    h = jnp.sin(f[1:2] * (jnp.dot(h, w2_ref[...], precision=HIGHEST,
                                  preferred_element_type=F32) + b2_ref[...]))
    h = jnp.dot(h, w3_ref[...], precision=HIGHEST, preferred_element_type=F32)
    win = win_ref[...]
    d = win.shape[1]
    rows = win.shape[0]
    t = pl.program_id(0) * rows + lax.broadcasted_iota(I32, (rows, 1), 0)
    for o in range(HYENA_ORDER):
        hf = h[:, 2 * o * d:(2 * o + 1) * d] * win
        hb = jnp.where(t == 0, 0.0, h[:, (2 * o + 1) * d:(2 * o + 2) * d] * win)
        hs_ref[:, o * d:(o + 1) * d] = (hf + hb).astype(BF16)
        hd_ref[:, o * d:(o + 1) * d] = (hf - hb).astype(BF16)


def _hy_spec_kernel(c_ref, s_ref, hs_ref, hd_ref, hr_ref, hi_ref, hn_ref):
    hs = hs_ref[...]
    a1 = jnp.dot(c_ref[...], hs, preferred_element_type=F32)
    a2 = jnp.dot(s_ref[...], hd_ref[...], preferred_element_type=F32)
    a3 = jnp.dot(s_ref[...], hs, preferred_element_type=F32)
    ri = lax.broadcasted_iota(I32, a1.shape, 0) + pl.program_id(0) * a1.shape[0]
    hr_ref[...] = a1
    hi_ref[...] = jnp.where(ri == 0, 0.0, a2)
    hn_ref[...] = jnp.where(ri == 0, a3, a1)


def _dft_consts(seq):
    k = jnp.arange(seq, dtype=I32)
    step = LANES
    assert seq % step == 0
    phase = lambda t: ((k[:, None] * t[None, :]) % (2 * seq)).astype(F32) * (math.pi / seq)
    ang_hi = phase(jnp.arange(seq // step, dtype=I32) * step)[:, :, None]
    ang_lo = phase(jnp.arange(step, dtype=I32))[:, None, :]
    cos = (jnp.cos(ang_hi) * jnp.cos(ang_lo) - jnp.sin(ang_hi) * jnp.sin(ang_lo)).reshape(seq, seq)
    sin = (jnp.sin(ang_hi) * jnp.cos(ang_lo) + jnp.cos(ang_hi) * jnp.sin(ang_lo)).reshape(seq, seq)
    sgn = jnp.where(k % 2 == 0, 1.0, -1.0).astype(F32)
    fwd_i = jnp.where(k[:, None] == 0, sgn[None, :], -sin)
    return cos.astype(BF16), fwd_i.astype(BF16)


def _hy_filters(seq, consts, w1, b1, sin_freq, w2, b2, w3):
    cos, fwd_i = consts[0], consts[1]
    d = w3.shape[-1] // (2 * HYENA_ORDER)
    hid = HYENA_FILTER_HIDDEN
    t = jnp.arange(seq, dtype=F32)
    t_norm = t / max(seq - 1, 1)
    bands = (HYENA_POS_EMB - 1) // 2
    freqs = jnp.linspace(1e-4, bands - 1, bands, dtype=F32)
    ang = (2.0 * math.pi / seq) * t[:, None] * freqs[None, :]
    pe = jnp.concatenate([t_norm[:, None], jnp.cos(ang), -jnp.sin(ang)], axis=-1)
    pe = jnp.pad(pe, ((0, 0), (0, LANES - HYENA_POS_EMB)))
    min_decay = math.log(HYENA_DECAY_TARGET) / HYENA_SLOW_DECAY
    max_decay = math.log(HYENA_DECAY_TARGET) / HYENA_FAST_DECAY
    deltas = jnp.abs(jnp.linspace(min_decay, max_decay, d, dtype=F32))
    window = jnp.exp(-t_norm[:, None] * deltas[None, :]) + HYENA_WINDOW_SHIFT
    padc = LANES - hid
    w1p = jnp.pad(w1, ((0, LANES - HYENA_POS_EMB), (0, padc)))
    w2p = jnp.pad(w2, ((0, padc), (0, padc)))
    w3p = jnp.pad(w3, ((0, padc), (0, 0)))
    rows = 256
    fix = lambda i: (0, 0)
    row = lambda i: (i, 0)
    hs, hd = pl.pallas_call(
        _hy_filter_kernel,
        grid=(seq // rows,),
        in_specs=[pl.BlockSpec((rows, LANES), row), pl.BlockSpec((LANES, LANES), fix),
                  pl.BlockSpec((1, LANES), fix), pl.BlockSpec((2, LANES), fix),
                  pl.BlockSpec((LANES, LANES), fix), pl.BlockSpec((1, LANES), fix),
                  pl.BlockSpec((LANES, 2 * HYENA_ORDER * d), fix), pl.BlockSpec((rows, d), row)],
        out_specs=[pl.BlockSpec((rows, HYENA_ORDER * d), row)] * 2,
        out_shape=[jax.ShapeDtypeStruct((seq, HYENA_ORDER * d), BF16)] * 2,
        compiler_params=_cparams(("parallel",)),
        name="hy_filter",
    )(pe, w1p, _row2(jnp.pad(b1, (0, padc))), jnp.pad(sin_freq, ((0, 0), (0, padc))), w2p,
      _row2(jnp.pad(b2, (0, padc))), w3p, window)
    tf, tn = DFT_TILE, DFT_COLS
    plane = jax.ShapeDtypeStruct((seq, HYENA_ORDER * d), F32)
    hr, hi, hn = pl.pallas_call(
        _hy_spec_kernel,
        grid=(seq // tf, HYENA_ORDER * d // tn),
        in_specs=[pl.BlockSpec((tf, seq), lambda i, j: (i, 0)), pl.BlockSpec((tf, seq), lambda i, j: (i, 0)),
                  pl.BlockSpec((seq, tn), lambda i, j: (0, j)), pl.BlockSpec((seq, tn), lambda i, j: (0, j))],
        out_specs=[pl.BlockSpec((tf, tn), lambda i, j: (i, j))] * 3,
        out_shape=[plane] * 3,
        compiler_params=_cparams(("parallel", "arbitrary")),
        name="hy_spec",
    )(cos, fwd_i, hs, hd)
    half = seq // 2
    har = hr[:half]
    hai = jnp.concatenate([hr[half:half + 1], hi[1:half]], axis=0)
    hbr = jnp.concatenate([hn[0:1], jnp.flip(hr[half + 1:], axis=0)], axis=0)
    hbi = jnp.concatenate([hi[half:half + 1], jnp.flip(hi[half + 1:], axis=0)], axis=0)
    return har, hai, hbr, hbi


def _dft_fwd_kernel(c_ref, s_ref, e_ref, o_ref, tw_ref, har_ref, hai_ref, hbr_ref, hbi_ref,
                    v0r_ref, v0i_ref, v1r_ref, v1i_ref):
    cm, sm = c_ref[...], s_ref[...]
    tw = tw_ref[...]
    c, s = tw[:, 0:1], tw[:, 1:2]
    rows = cm.shape[0]
    row0 = (lax.broadcasted_iota(I32, (rows, 1), 0) + pl.program_id(0) * rows) == 0
    u = jnp.where(row0, 1.0, 2.0)
    subs = [slice(lo, lo + DFT_SUBCOLS) for lo in range(0, e_ref.shape[2], DFT_SUBCOLS)]
    prods = []
    for cs in subs:
        e, o = e_ref[0, :, cs], o_ref[0, :, cs]
        prods.append((jnp.dot(cm, e, preferred_element_type=F32), jnp.dot(sm, e, preferred_element_type=F32),
                      jnp.dot(cm, o, preferred_element_type=F32), jnp.dot(sm, o, preferred_element_type=F32)))
    for cs, (er, ei, orr, oi) in zip(subs, prods):
        tr = c * orr + s * oi
        ti = c * oi - s * orr
        ar, br = er + tr, er - tr
        ai = jnp.where(row0, ei, ei + ti)
        bi = jnp.where(row0, -oi, ti - ei)
        har, hai, hbr, hbi = har_ref[:, cs], hai_ref[:, cs], hbr_ref[:, cs], hbi_ref[:, cs]
        yar = jnp.where(row0, ar * har, ar * har - ai * hai)
        ybr = jnp.where(row0, br * hbr, br * hbr - bi * hbi)
        yai = jnp.where(row0, ai * hai - bi * hbi, ar * hai + ai * har)
        ybi = jnp.where(row0, ai * hbi + bi * hai, br * hbi + bi * hbr)
        dr, di = yar - ybr, yai + ybi
        v0r_ref[0, :, cs] = (u * (yar + ybr)).astype(BF16)
        v0i_ref[0, :, cs] = jnp.where(row0, 2.0 * yai, u * (yai - ybi)).astype(BF16)
        v1r_ref[0, :, cs] = (u * (c * dr - s * di)).astype(BF16)
        v1i_ref[0, :, cs] = jnp.where(row0, -2.0 * ybi, u * (c * di + s * dr)).astype(BF16)


def _dft_inv_kernel(gr_ref, gi_ref, v0r_ref, v0i_ref, v1r_ref, v1i_ref, ze_ref, zo_ref, ge_ref, go_ref,
                    fb_ref, ye_ref, yo_ref):
    gr, gi = gr_ref[...], gi_ref[...]
    fb = fb_ref[...]
    ye = jnp.dot(gr, v0r_ref[0], preferred_element_type=F32) + jnp.dot(gi, v0i_ref[0], preferred_element_type=F32)
    ye_ref[0] = (ge_ref[0].astype(F32) * (ye + ze_ref[0].astype(F32) * fb)).astype(BF16)
    yo = jnp.dot(gr, v1r_ref[0], preferred_element_type=F32) + jnp.dot(gi, v1i_ref[0], preferred_element_type=F32)
    yo_ref[0] = (go_ref[0].astype(F32) * (yo + zo_ref[0].astype(F32) * fb)).astype(BF16)


def _long_conv_gate(consts, planes, order, z_e, z_o, gate, gate_e, gate_o, filt_bias):
    cos, fwd_i, inv_r, inv_i, twiddle = consts
    (ze_arr, ze_col), (zo_arr, zo_col) = z_e, z_o
    batch, half, _ = ze_arr.shape
    d = D_MODEL
    tf, tn = DFT_TILE, DFT_COLS
    nj = d // tn
    plane_spec = pl.BlockSpec((tf, tn), lambda i, j, b: (i, order * nj + j))
    mat_f = pl.BlockSpec((tf, half), lambda i, j, b: (i, 0))
    cols_f = lambda off: pl.BlockSpec((1, half, tn), lambda i, j, b: (b, 0, off + j))
    tile_f = pl.BlockSpec((1, tf, tn), lambda i, j, b: (b, i, j))
    v_shape = jax.ShapeDtypeStruct((batch, half, d), BF16)
    v = pl.pallas_call(
        _dft_fwd_kernel,
        grid=(half // tf, nj, batch),
        in_specs=[mat_f, mat_f, cols_f(ze_col), cols_f(zo_col), pl.BlockSpec((tf, LANES), lambda i, j, b: (i, 0)),
                  plane_spec, plane_spec, plane_spec, plane_spec],
        out_specs=[tile_f] * 4,
        out_shape=[v_shape] * 4,
        compiler_params=_cparams(("parallel", "parallel", "arbitrary")),
        name="dft_fwd",
    )(cos, fwd_i, ze_arr, zo_arr, twiddle, *planes)
    mat_i = pl.BlockSpec((tf, half), lambda b, j, i: (i, 0))
    cols_i = pl.BlockSpec((1, half, tn), lambda b, j, i: (b, 0, j))
    tile_i = lambda off: pl.BlockSpec((1, tf, tn), lambda b, j, i: (b, i, off + j))
    return pl.pallas_call(
        _dft_inv_kernel,
        grid=(batch, nj, half // tf),
        in_specs=[mat_i, mat_i, cols_i, cols_i, cols_i, cols_i, tile_i(ze_col), tile_i(zo_col),
                  tile_i(gate_e), tile_i(gate_o), pl.BlockSpec((1, tn), lambda b, j, i: (0, j))],
        out_specs=[tile_i(0), tile_i(0)],
        out_shape=[v_shape, v_shape],
        compiler_params=_cparams(("parallel", "parallel", "arbitrary")),
        name="dft_inv",
    )(inv_r, inv_i, *v, ze_arr, zo_arr, gate, gate, _row2(filt_bias))


def _proj_ln_pair_kernel(ae_ref, ao_ref, w_ref, bias_ref, x_ref, g_ref, b_ref, o_ref):
    d = w_ref.shape[1]
    w, bias, g, b = w_ref[...], bias_ref[...], g_ref[...], b_ref[...]
    he = jnp.dot(ae_ref[...], w, preferred_element_type=F32) + bias
    o_ref[:, :d] = _ln_rows(ALPHA * x_ref[:, :d] + he, g, b)
    ho = jnp.dot(ao_ref[...], w, preferred_element_type=F32) + bias
    o_ref[:, d:] = _ln_rows(ALPHA * x_ref[:, d:] + ho, g, b)


def _proj_ln_pair(a_e, a_o, w, bias, x, g, b):
    n, d = x.shape
    rows = ROW_TILE // 2
    row = lambda i: (i, 0)
    fix = lambda i: (0, 0)
    out = pl.pallas_call(
        _proj_ln_pair_kernel,
        grid=(n // ROW_TILE,),
        in_specs=[pl.BlockSpec((rows, d), row), pl.BlockSpec((rows, d), row), pl.BlockSpec(w.shape, fix),
                  pl.BlockSpec((1, d), fix), pl.BlockSpec((rows, 2 * d), row),
                  pl.BlockSpec((1, d), fix), pl.BlockSpec((1, d), fix)],
        out_specs=pl.BlockSpec((rows, 2 * d), row),
        out_shape=jax.ShapeDtypeStruct((n // 2, 2 * d), F32),
        compiler_params=_cparams(("parallel",)),
        name="proj_ln_pair",
    )(a_e, a_o, w, _row2(bias), x.reshape(n // 2, 2 * d), _row2(g), _row2(b))
    return out.reshape(n, d)


def _hyena_mixer_ln(x, batch, w_in, b_in, conv_w, conv_b, ffn_w1, ffn_b1, sin_freq, ffn_w2, ffn_b2,
                    ffn_w3, filt_bias, w_out, b_out, ln_g, ln_b):
    n, d = x.shape
    seq = n // batch
    half = seq // 2
    nj = d // DFT_COLS
    planes = _hy_filters(seq, _dft_consts(seq), ffn_w1, ffn_b1, sin_freq, ffn_w2, ffn_b2, ffn_w3)
    cos, fwd_i = _dft_consts(half)
    scale = 0.5 / seq
    k = jnp.arange(half, dtype=F32)[:, None] * (math.pi / seq)
    twiddle = jnp.pad(jnp.concatenate([jnp.cos(k), jnp.sin(k)], axis=1), ((0, 0), (0, LANES - 2)))
    consts = (cos, fwd_i, cos * scale, fwd_i.T * scale, twiddle)
    u = _hy_in(x.astype(BF16), batch, w_in, b_in, conv_w, conv_b)
    u = u.reshape(batch, half, 6 * d)
    y_e, y_o = _long_conv_gate(consts, planes, 0, (u, 0), (u, 3 * nj), u, nj, 4 * nj, filt_bias[0])
    y_e, y_o = _long_conv_gate(consts, planes, 1, (y_e, 0), (y_o, 0), u, 2 * nj, 5 * nj, filt_bias[1])
    return _proj_ln_pair(y_e.reshape(n // 2, d), y_o.reshape(n // 2, d), w_out.astype(BF16), b_out, x,
                         ln_g, ln_b)


def _na_in_kernel(x_ref, w_ref, b_ref, q_ref, k_ref, v_ref):
    d = D_MODEL
    p = jnp.dot(x_ref[...].astype(BF16), w_ref[...], preferred_element_type=F32) + b_ref[...]
    q_ref[...] = (p[:, :d] * NA_HEAD_DIM ** -0.5).astype(BF16)
    k_ref[...] = p[:, d:2 * d].astype(BF16)
    v_ref[...] = p[:, 2 * d:].astype(BF16)


def _na_attn_kernel(q_ref, k_ref, vt_ref, bias_ref, o_ref):
    for i in range(NA_PAIRS):
        rp = pl.program_id(2) * NA_PAIRS + i
        o_ref[i * LANES:(i + 1) * LANES] = _na_pair(rp, q_ref[i * LANES:(i + 1) * LANES], k_ref, vt_ref, bias_ref)


def _na_pair(rp, q, k_ref, vt_ref, bias_ref):
    hpg = LANES // NA_HEAD_DIM
    n_rows = GRID_W
    r0 = 2 * rp
    rs0 = jnp.clip(r0 - NA_WIN_ROWS // 2, 0, n_rows - NA_WIN_ROWS)
    rs1 = jnp.clip(r0 + 1 - NA_WIN_ROWS // 2, 0, n_rows - NA_WIN_ROWS)
    start = jnp.minimum(rs0, n_rows - NA_KEY_ROWS)
    tok0 = pl.multiple_of(start * GRID_W, LANES)
    nkeys = NA_KEY_ROWS * GRID_W
    kw = k_ref[pl.ds(tok0, nkeys), :]
    vtw = vt_ref[0, :, pl.ds(tok0, nkeys)]
    lane = lax.broadcasted_iota(I32, (1, LANES), 1)
    sub = lax.broadcasted_iota(I32, (LANES, 1), 0)
    right = lane >= GRID_W
    acc = jnp.zeros((LANES, LANES), F32)
    for h in range(hpg):
        in_head = (lane >= h * NA_HEAD_DIM) & (lane < (h + 1) * NA_HEAD_DIM)
        qh = jnp.where(in_head, q, jnp.zeros_like(q))
        s = lax.dot_general(kw, qh, NT_DIMS, preferred_element_type=F32)
        tiles = []
        for a in range(NA_KEY_ROWS):
            kr = start + a
            dr0 = kr - r0 + (NA_WIN_ROWS - 1)
            bt = bias_ref[h, jnp.clip(dr0, 0, 2 * NA_WIN_ROWS - 1)]
            if a < 2 or a >= NA_WIN_ROWS:
                pen0 = jnp.where((kr >= rs0) & (kr < rs0 + NA_WIN_ROWS), 0.0, NEG)
                pen1 = jnp.where((kr >= rs1) & (kr < rs1 + NA_WIN_ROWS), 0.0, NEG)
                bt = bt + jnp.where(right, pen1, pen0)
            tiles.append(bt)
        s = s + jnp.concatenate(tiles, axis=0)
        m = jnp.max(s, axis=0, keepdims=True)
        p = jnp.exp(s - m)
        inv = 1.0 / jnp.sum(p, axis=0, keepdims=True)
        in_head_rows = (sub >= h * NA_HEAD_DIM) & (sub < (h + 1) * NA_HEAD_DIM)
        pv = jnp.dot(vtw, p.astype(BF16), preferred_element_type=F32)
        acc = acc + jnp.where(in_head_rows, pv * inv, 0.0)
    return acc.T.astype(BF16)


def _na_bias_table(rpb):
    nh = rpb.shape[0]
    kc = jnp.arange(GRID_W)[:, None]
    qc = jnp.arange(GRID_W)[None, :]
    cs = jnp.clip(qc - NA_WIN_COLS // 2, 0, GRID_W - NA_WIN_COLS)
    inwin = (kc >= cs) & (kc < cs + NA_WIN_COLS)
    dc = jnp.clip(kc - qc + (NA_WIN_COLS - 1), 0, 2 * NA_WIN_COLS - 2)
    colb = jnp.where(inwin[None, None], rpb.astype(F32)[:, :, dc], NEG)
    pad = jnp.full((nh, 1, GRID_W, GRID_W), NEG, F32)
    left = jnp.concatenate([colb, pad], axis=1)
    rgt = jnp.concatenate([pad, colb], axis=1)
    return jnp.concatenate([left, rgt], axis=-1)


def _na_mixer_ln(x, batch, w_in, b_in, rpb, w_out, b_out, ln_g, ln_b):
    n, d = x.shape
    seq = n // batch
    assert seq == GRID_W * GRID_W
    row = lambda i: (i, 0)
    fix = lambda i: (0, 0)
    qkv_shape = jax.ShapeDtypeStruct((n, d), BF16)
    q, k, v = pl.pallas_call(
        _na_in_kernel,
        grid=(n // ROW_TILE,),
        in_specs=[pl.BlockSpec((ROW_TILE, d), row), pl.BlockSpec((d, 3 * d), fix), pl.BlockSpec((1, 3 * d), fix)],
        out_specs=[pl.BlockSpec((ROW_TILE, d), row)] * 3,
        out_shape=[qkv_shape] * 3,
        compiler_params=_cparams(("parallel",)),
        name="na_in",
    )(x, w_in.astype(BF16), _row2(b_in))
    vt = jnp.swapaxes(v.reshape(batch, seq, d), 1, 2)
    bias = _na_bias_table(rpb)
    hpg = LANES // NA_HEAD_DIM
    n_pairs = GRID_W // 2
    n_steps = n_pairs // NA_PAIRS
    o = pl.pallas_call(
        _na_attn_kernel,
        grid=(batch, d // LANES, n_steps),
        in_specs=[pl.BlockSpec((NA_PAIRS * LANES, LANES), lambda b, g, r: (b * n_steps + r, g)),
                  pl.BlockSpec((seq, LANES), lambda b, g, r: (b, g)),
                  pl.BlockSpec((1, LANES, seq), lambda b, g, r: (b, g, 0)),
                  pl.BlockSpec((hpg, 2 * NA_WIN_ROWS, GRID_W, LANES), lambda b, g, r: (g, 0, 0, 0))],
        out_specs=pl.BlockSpec((NA_PAIRS * LANES, LANES), lambda b, g, r: (b * n_steps + r, g)),
        out_shape=qkv_shape,
        compiler_params=_cparams(("parallel", "parallel", "arbitrary")),
        name="na_attn",
    )(q, k, vt, bias)
    return _proj_ln(o, w_out.astype(BF16), b_out, x, ln_g, ln_b)


def _moe_route_kernel(x_ref, wr_ref, rb_ref, xs_ref, pos_ref, gate_ref, cnt_ref):
    ne, tb = N_EXPERTS, MOE_TB
    xb, x_lo = _split_bf16(x_ref[...])
    nt = lambda p, q: lax.dot_general(p, q, NT_DIMS, preferred_element_type=F32)
    logits = nt(wr_ref[0], xb) + nt(wr_ref[1], xb) + nt(wr_ref[0], x_lo) + rb_ref[...]
    eio = lax.broadcasted_iota(I32, (ne, tb), 0)
    work = logits
    vals, hots = [], []
    for _ in range(TOP_K):
        m = jnp.max(work, axis=0, keepdims=True)
        idx = jnp.min(jnp.where(work == m, eio, ne), axis=0, keepdims=True)
        hot = eio == idx
        vals.append(m)
        hots.append(hot)
        work = jnp.where(hot, -jnp.inf, work)
    exps = [jnp.exp(v - vals[0]) for v in vals]
    denom = exps[0] + exps[1] + exps[2] + exps[3]
    sel = jnp.zeros((ne, tb), F32)
    for hot in hots:
        sel = sel + jnp.where(hot, 1.0, 0.0)
    earlier = (lax.broadcasted_iota(I32, (tb, tb), 0) < lax.broadcasted_iota(I32, (tb, tb), 1))
    rank = jnp.dot(sel.astype(BF16), jnp.where(earlier, 1.0, 0.0).astype(BF16),
                   preferred_element_type=F32)
    cnt = jnp.sum(sel, axis=1, keepdims=True)
    cnt_pad = jnp.ceil(cnt * (1.0 / MOE_G)) * MOE_G
    below = (lax.broadcasted_iota(I32, (ne, ne), 1) < lax.broadcasted_iota(I32, (ne, ne), 0))
    seg_off = jnp.dot(jnp.where(below, 1.0, 0.0), jnp.broadcast_to(cnt_pad, (ne, tb)),
                      precision=HIGHEST, preferred_element_type=F32)
    slot = seg_off + rank
    pos = [jnp.sum(jnp.where(hot, slot, 0.0), axis=0, keepdims=True).astype(I32) for hot in hots]
    zero_row = jnp.zeros((1, tb), I32)
    pos_ref[0] = jnp.concatenate(pos + [zero_row] * (8 - TOP_K), axis=0)
    gate_ref[0] = jnp.concatenate([e / denom for e in exps] + [zero_row.astype(F32)] * (8 - TOP_K), axis=0)
    cnt_ref[0] = jnp.broadcast_to(cnt_pad, (ne, LANES)).astype(I32)
    local = lax.broadcasted_iota(I32, (MOE_PCH, tb), 0).astype(F32).astype(BF16)
    one, zero = jnp.ones((), BF16), jnp.zeros((), BF16)
    for r in range(MOE_RB // MOE_PCH):
        onehot = None
        for p in pos:
            hit = jnp.where(local == (p - r * MOE_PCH).astype(F32).astype(BF16), one, zero)
            onehot = hit if onehot is None else onehot + hit
        xs_ref[r * MOE_PCH:(r + 1) * MOE_PCH] = jnp.dot(
            onehot, xb, preferred_element_type=F32).astype(BF16)


def _moe_expert_kernel(te_ref, src_ref, nt_ref, xs_hbm, wgu_ref, bgu_ref, wd_ref, bd_ref,
                       ys_hbm, xbuf, ybuf, wgu_bf, wd_bf, sem_in, sem_out, *, n_blocks):
    t = pl.program_id(0)
    nt = nt_ref[0]
    s = t % 2

    def in_copy(chunk, slot, c):
        return pltpu.make_async_copy(
            xs_hbm.at[pl.ds(pl.multiple_of(chunk * MOE_G, MOE_G), MOE_G)],
            xbuf.at[slot, pl.ds(c * MOE_G, MOE_G)], sem_in.at[slot])

    def out_copy(chunk, slot, c):
        return pltpu.make_async_copy(
            ybuf.at[slot, pl.ds(c * MOE_G, MOE_G)],
            ys_hbm.at[pl.ds(pl.multiple_of(chunk * MOE_G, MOE_G), MOE_G)], sem_out.at[slot])

    def tile_chunks(tile):
        return [src_ref[tile * MOE_CPT + c] for c in range(MOE_CPT)]

    def start_in(tile, slot):
        for c, chunk in enumerate(tile_chunks(tile)):
            in_copy(chunk, slot, c).start()

    def start_out(tile, slot):
        chunks = tile_chunks(tile)
        for c, chunk in enumerate(chunks):
            spare = ((slot * MOE_CPT + c) % n_blocks) * MOE_BLOCK_CHUNKS + MOE_SPARE_CHUNK
            out_copy(jnp.where(chunk == MOE_ZERO_CHUNK, spare, chunk), slot, c).start()

    def wait_in(slot):
        for c in range(MOE_CPT):
            in_copy(0, slot, c).wait()

    def wait_out(slot):
        for c in range(MOE_CPT):
            out_copy(0, slot, c).wait()

    @pl.when(t == 0)
    def _():
        start_in(0, 0)

    @pl.when(t + 1 < nt)
    def _():
        start_in(t + 1, 1 - s)

    @pl.when(t < nt)
    def _():
        @pl.when(jnp.logical_or(t == 0, te_ref[t] != te_ref[jnp.maximum(t - 1, 0)]))
        def _():
            wgu_bf[...] = wgu_ref[0, 0].astype(BF16)
            wd_bf[...] = wd_ref[0, 0].astype(BF16)

        wait_in(s)
        f = wd_bf.shape[0]
        h = jnp.dot(xbuf[s], wgu_bf[...], preferred_element_type=F32) + bgu_ref[0, 0]
        glu = jnp.minimum(h[:, :f], SWIGLU_LIMIT)
        lin = jnp.clip(h[:, f:], -SWIGLU_LIMIT, SWIGLU_LIMIT)
        act = (glu * jax.nn.sigmoid(SWIGLU_ALPHA * glu) * (lin + 1.0)).astype(BF16)
        y = jnp.dot(act, wd_bf[...], preferred_element_type=F32) + bd_ref[0, 0]

        @pl.when(t >= 2)
        def _():
            wait_out(s)

        ybuf[s] = y.astype(BF16)
        start_out(t, s)

        @pl.when(t == nt - 1)
        def _():
            wait_out(s)

            @pl.when(t >= 1)
            def _():
                wait_out(1 - s)


def _moe_combine_kernel(ys_ref, pos_ref, gate_ref, x_ref, g_ref, b_ref, o_ref):
    tb = MOE_TB
    pos = pos_ref[...]
    gate = gate_ref[...].astype(BF16)
    local = lax.broadcasted_iota(I32, (tb, MOE_PCH), 1).astype(F32).astype(BF16)
    zero = jnp.zeros((), BF16)
    f = jnp.zeros((tb, D_MODEL), F32)
    for r in range(MOE_RB // MOE_PCH):
        w = None
        for k in range(TOP_K):
            pk = (pos[:, k:k + 1] - r * MOE_PCH).astype(F32).astype(BF16)
            wk = jnp.where(local == pk, gate[:, k:k + 1], zero)
            w = wk if w is None else w + wk
        f = f + jnp.dot(w, ys_ref[r * MOE_PCH:(r + 1) * MOE_PCH], preferred_element_type=F32)
    o_ref[...] = _ln_rows(ALPHA * x_ref[...] + f, g_ref[...], b_ref[...])


def _moe_tile_tables(cnt_pad, n_tiles_max):
    nb, ne = cnt_pad.shape
    nch_eb = (cnt_pad // MOE_G).T
    seg_off = jnp.cumsum(cnt_pad, axis=1) - cnt_pad
    base_eb = ((jnp.arange(nb, dtype=I32)[:, None] * MOE_RB + seg_off) // MOE_G).T
    first_eb = jnp.cumsum(nch_eb, axis=1) - nch_eb
    tiles_e = (jnp.sum(nch_eb, axis=1) + MOE_CPT - 1) // MOE_CPT
    tile_end = jnp.cumsum(tiles_e)
    n_tiles = tile_end[-1]
    tau = jnp.arange(n_tiles_max, dtype=I32)
    tile_e = jnp.minimum(jnp.sum(tau[:, None] >= tile_end[None, :], axis=1), ne - 1).astype(I32)
    of_tile = tile_e[:, None] == jnp.arange(ne, dtype=I32)[None, :]
    pick = lambda tab: jnp.sum(jnp.where(of_tile[:, :, None], tab[None], 0), axis=1)
    tile_first = tau - jnp.sum(jnp.where(of_tile, (tile_end - tiles_e)[None, :], 0), axis=1)
    k = (tile_first * MOE_CPT)[:, None] + jnp.arange(MOE_CPT, dtype=I32)[None, :]
    first, count, base = pick(first_eb), pick(nch_eb), pick(base_eb)
    inside = (k[:, :, None] >= first[:, None, :]) & (k[:, :, None] < (first + count)[:, None, :])
    src = jnp.sum(jnp.where(inside, (base - first)[:, None, :] + k[:, :, None], 0), axis=2)
    src = jnp.where(jnp.any(inside, axis=2) & (tau < n_tiles)[:, None], src, MOE_ZERO_CHUNK)
    return tile_e, src.reshape(-1).astype(I32), n_tiles.astype(I32).reshape(1)


def _moe_ln(x, layer, router_w, router_b, w_gu, b_gu, w_down, b_down, ln_g, ln_b):
    n, d = x.shape
    ne, f = N_EXPERTS, w_down.shape[2]
    depth = w_gu.shape[0]
    expert_block = lambda t, te, sr, nt: (layer, te[t], 0, 0)
    nb = n // MOE_TB
    assert MOE_RB >= MOE_TB * TOP_K + ne * (MOE_G - 1) + 2 * MOE_G
    blk = lambda i: (i, 0)
    blk3 = lambda i: (i, 0, 0)
    fix = lambda i: (0, 0)
    slot_rows = nb * MOE_RB
    xs, pos, gates, cnt = pl.pallas_call(
        _moe_route_kernel,
        grid=(nb,),
        in_specs=[pl.BlockSpec((MOE_TB, d), blk), pl.BlockSpec((2, ne, d), lambda i: (0, 0, 0)),
                  pl.BlockSpec((ne, 1), fix)],
        out_specs=[pl.BlockSpec((MOE_RB, d), blk), pl.BlockSpec((1, 8, MOE_TB), blk3),
                   pl.BlockSpec((1, 8, MOE_TB), blk3), pl.BlockSpec((1, ne, LANES), blk3)],
        out_shape=[jax.ShapeDtypeStruct((slot_rows, d), BF16), jax.ShapeDtypeStruct((nb, 8, MOE_TB), I32),
                   jax.ShapeDtypeStruct((nb, 8, MOE_TB), F32), jax.ShapeDtypeStruct((nb, ne, LANES), I32)],
        compiler_params=_cparams(("parallel",)),
        name="moe_route",
    )(x, jnp.stack(_split_bf16(router_w.T)), router_b.reshape(ne, 1))

    n_tiles_max = nb * MOE_RB // MOE_TM + ne
    tile_e, src, n_tiles = _moe_tile_tables(cnt[:, :, 0], n_tiles_max)
    ys = pl.pallas_call(
        functools.partial(_moe_expert_kernel, n_blocks=nb),
        grid_spec=pltpu.PrefetchScalarGridSpec(
            num_scalar_prefetch=3,
            grid=(n_tiles_max,),
            in_specs=[pl.BlockSpec(memory_space=pl.ANY),
                      pl.BlockSpec((1, 1, d, 2 * f), expert_block),
                      pl.BlockSpec((1, 1, 1, 2 * f), expert_block),
                      pl.BlockSpec((1, 1, f, d), expert_block),
                      pl.BlockSpec((1, 1, 1, d), expert_block)],
            out_specs=pl.BlockSpec(memory_space=pl.ANY),
            scratch_shapes=[pltpu.VMEM((2, MOE_TM, d), BF16), pltpu.VMEM((2, MOE_TM, d), BF16),
                            pltpu.VMEM((d, 2 * f), BF16), pltpu.VMEM((f, d), BF16),
                            pltpu.SemaphoreType.DMA((2,)), pltpu.SemaphoreType.DMA((2,))]),
        out_shape=jax.ShapeDtypeStruct((slot_rows, d), BF16),
        input_output_aliases={3: 0},
        compiler_params=pltpu.CompilerParams(dimension_semantics=("arbitrary",),
                                             vmem_limit_bytes=MOE_EXPERT_VMEM),
        name="moe_expert",
    )(tile_e, src, n_tiles, xs, w_gu, b_gu.reshape(depth, ne, 1, 2 * f), w_down,
      b_down.reshape(depth, ne, 1, d))

    pos_t = jnp.swapaxes(pos, 1, 2).reshape(n, 8)
    gate_t = jnp.swapaxes(gates, 1, 2).reshape(n, 8)
    return pl.pallas_call(
        _moe_combine_kernel,
        grid=(nb,),
        in_specs=[pl.BlockSpec((MOE_RB, d), blk), pl.BlockSpec((MOE_TB, 8), blk), pl.BlockSpec((MOE_TB, 8), blk),
                  pl.BlockSpec((MOE_TB, d), blk), pl.BlockSpec((1, d), fix), pl.BlockSpec((1, d), fix)],
        out_specs=pl.BlockSpec((MOE_TB, d), blk),
        out_shape=jax.ShapeDtypeStruct((n, d), F32),
        compiler_params=_cparams(("parallel",)),
        name="moe_combine",
    )(ys, pos_t, gate_t, x, _row2(ln_g), _row2(ln_b))


def kernel(x, ln1_g, ln1_b, ln2_g, ln2_b, gla_w_in, gla_gate_w2, gla_gate_b, gla_norm_g, gla_w_out,
           hy_w_in, hy_b_in, hy_conv_w, hy_conv_b, hy_ffn_w1, hy_ffn_b1, hy_sin_freq, hy_ffn_w2,
           hy_ffn_b2, hy_ffn_w3, hy_filt_bias, hy_w_out, hy_b_out, na_w_in, na_b_in, na_rpb, na_w_out,
           na_b_out, moe_router_w, moe_router_b, moe_w_gu, moe_b_gu, moe_w_down, moe_b_down):
    batch, seq, d = x.shape
    xt = x.reshape(batch * seq, d)
    for i in range(DEPTH):
        m, j = i % N_MIXERS, i // N_MIXERS
        if m == 0:
            xt = _gla_mixer_ln(xt, batch, gla_w_in[j], gla_gate_w2[j], gla_gate_b[j], gla_norm_g[j],
                               gla_w_out[j], ln1_g[i], ln1_b[i])
        elif m == 1:
            xt = _hyena_mixer_ln(xt, batch, hy_w_in[j], hy_b_in[j], hy_conv_w[j], hy_conv_b[j],
                                 hy_ffn_w1[j], hy_ffn_b1[j], hy_sin_freq[j], hy_ffn_w2[j], hy_ffn_b2[j],
                                 hy_ffn_w3[j], hy_filt_bias[j], hy_w_out[j], hy_b_out[j],
                                 ln1_g[i], ln1_b[i])
        else:
            xt = _na_mixer_ln(xt, batch, na_w_in[j], na_b_in[j], na_rpb[j], na_w_out[j], na_b_out[j],
                              ln1_g[i], ln1_b[i])
        xt = _moe_ln(xt, i, moe_router_w[i], moe_router_b[i], moe_w_gu, moe_b_gu, moe_w_down,
                     moe_b_down, ln2_g[i], ln2_b[i])
    return xt.reshape(batch, seq, d)
```

```python
import functools
import math

import jax
import jax.numpy as jnp
from jax import lax
from jax.experimental import pallas as pl
from jax.experimental.pallas import tpu as pltpu

F32, BF16, I32 = jnp.float32, jnp.bfloat16, jnp.int32
HIGHEST = lax.Precision.HIGHEST
NT_DIMS = (((1,), (1,)), ((), ()))
TN_DIMS = (((0,), (0,)), ((), ()))

D_MODEL = 1024
DEPTH = 4
N_MIXERS = 3
GRID_W = 64
ALPHA = (2 * DEPTH) ** 0.25
LN_EPS = 1e-5

GLA_HEADS = 4
GLA_KEY_DIM = 512
GLA_VAL_DIM = 1024
GLA_HEAD_K = 128
GLA_HEAD_V = 256
GLA_GATE_RANK = 16
GLA_TAU = 16.0
GLA_NORM_EPS = 1e-6

HYENA_ORDER = 2
HYENA_POS_EMB = 33
HYENA_FILTER_HIDDEN = 64
HYENA_DECAY_TARGET = 1e-2
HYENA_FAST_DECAY = 0.3
HYENA_SLOW_DECAY = 1.5
HYENA_WINDOW_SHIFT = 0.05

NA_HEAD_DIM = 32
NA_HEADS = 32
NA_WIN_ROWS = 8
NA_WIN_COLS = 16

N_EXPERTS = 32
TOP_K = 4
SWIGLU_ALPHA = 1.702
SWIGLU_LIMIT = 7.0

LANES = 128
BF16_SUBLANES = 16
VMEM_LIMIT = 48 * 1024 * 1024
MOE_EXPERT_VMEM = 56 * 1024 * 1024

ROW_TILE = 512
GLA_CHUNK = 64
GLA_SUB = 16
GLA_ROWS = 256
GLA_HEADS_PER_STEP = 4
GLA_EXP2_CLAMP = 115.0
DFT_TILE = 512
DFT_COLS = 512
DFT_SUBCOLS = 256
NA_KEY_ROWS = 10
NA_PAIRS = 4
NEG = -1e30
LOG2E = math.log2(math.e)

MOE_TB = 512
MOE_G = BF16_SUBLANES
MOE_RB = MOE_TB * TOP_K + 512
MOE_PCH = 256
MOE_BLOCK_CHUNKS = MOE_RB // MOE_G
MOE_ZERO_CHUNK = MOE_BLOCK_CHUNKS - 1
MOE_SPARE_CHUNK = MOE_BLOCK_CHUNKS - 2
MOE_TM = 512
MOE_CPT = MOE_TM // MOE_G


def _cparams(sem):
    return pltpu.CompilerParams(dimension_semantics=sem, vmem_limit_bytes=VMEM_LIMIT)


def _ln_rows(y, g, b):
    mu = jnp.mean(y, axis=-1, keepdims=True)
    yc = y - mu
    var = jnp.mean(yc * yc, axis=-1, keepdims=True)
    return yc * lax.rsqrt(var + LN_EPS) * g + b


def _row2(v):
    return v.reshape(1, -1)


def _split_bf16(a):
    hi = a.astype(BF16)
    return hi, (a - hi.astype(F32)).astype(BF16)


def _dot_split(a, b_hi, b_lo, dims):
    a_hi, a_lo = _split_bf16(a)
    if dims is None:
        mm = lambda p, q: jnp.dot(p, q, preferred_element_type=F32)
    else:
        mm = lambda p, q: lax.dot_general(p, q, dims, preferred_element_type=F32)
    return mm(a_hi, b_hi) + mm(a_lo, b_hi) + mm(a_hi, b_lo)


def _proj_ln_kernel(a_ref, w_ref, bias_ref, x_ref, g_ref, b_ref, o_ref):
    h = jnp.dot(a_ref[...], w_ref[...], preferred_element_type=F32) + bias_ref[...]
    o_ref[...] = _ln_rows(ALPHA * x_ref[...] + h, g_ref[...], b_ref[...])


def _proj_ln(a, w, bias, x, g, b):
    n, d = x.shape
    row = lambda i: (i, 0)
    fix = lambda i: (0, 0)
    return pl.pallas_call(
        _proj_ln_kernel,
        grid=(n // ROW_TILE,),
        in_specs=[pl.BlockSpec((ROW_TILE, a.shape[1]), row), pl.BlockSpec(w.shape, fix),
                  pl.BlockSpec((1, d), fix), pl.BlockSpec((ROW_TILE, d), row),
                  pl.BlockSpec((1, d), fix), pl.BlockSpec((1, d), fix)],
        out_specs=pl.BlockSpec((ROW_TILE, d), row),
        out_shape=jax.ShapeDtypeStruct((n, d), F32),
        compiler_params=_cparams(("parallel",)),
        name="proj_ln",
    )(a, w, _row2(bias), x, _row2(g), _row2(b))


def _gla_in_kernel(x_ref, w_ref, w2_ref, gb_ref, q_ref, k_ref, v_ref, g_ref, la_ref):
    kd, vd = GLA_KEY_DIM, GLA_VAL_DIM
    xb = x_ref[...].astype(BF16)
    proj = lambda lo, hi: jnp.dot(xb, w_ref[:, lo:hi], preferred_element_type=F32)
    glow = proj(2 * kd + 2 * vd, 2 * kd + 2 * vd + LANES)
    z = _dot_split(glow, w2_ref[0], w2_ref[1], None) + gb_ref[...]
    la_ref[...] = (jnp.minimum(z, 0.0) - jnp.log(1.0 + jnp.exp(-jnp.abs(z)))) * (LOG2E / GLA_TAU)
    og = proj(2 * kd + vd, 2 * kd + 2 * vd)
    g_ref[...] = (og * jax.nn.sigmoid(og)).astype(BF16)
    q_ref[...] = (proj(0, kd) * GLA_HEAD_K ** -0.5).astype(BF16)
    k_ref[...] = proj(kd, 2 * kd).astype(BF16)
    v_ref[...] = proj(2 * kd, 2 * kd + vd).astype(BF16)


def _gla_in(x, w_in, gate_w2, gate_b):
    n, d = x.shape
    kd, vd, r = GLA_KEY_DIM, GLA_VAL_DIM, GLA_GATE_RANK
    width = 2 * kd + 2 * vd
    w = jnp.pad(w_in, ((0, 0), (0, LANES - 2 * r))).astype(BF16)
    w2 = jnp.zeros((LANES, 2 * kd), F32)
    w2 = w2.at[:r, :kd].set(gate_w2[0]).at[r:2 * r, kd:].set(gate_w2[1])
    w2 = jnp.stack(_split_bf16(w2))
    gb = gate_b.reshape(1, 2 * kd)
    row = lambda i: (i, 0)
    fix = lambda i: (0, 0)
    return pl.pallas_call(
        _gla_in_kernel,
        grid=(n // ROW_TILE,),
        in_specs=[pl.BlockSpec((ROW_TILE, d), row), pl.BlockSpec((d, width + LANES), fix),
                  pl.BlockSpec((2, LANES, 2 * kd), lambda i: (0, 0, 0)), pl.BlockSpec((1, 2 * kd), fix)],
        out_specs=[pl.BlockSpec((ROW_TILE, kd), row), pl.BlockSpec((ROW_TILE, kd), row),
                   pl.BlockSpec((ROW_TILE, vd), row), pl.BlockSpec((ROW_TILE, vd), row),
                   pl.BlockSpec((ROW_TILE, 2 * kd), row)],
        out_shape=[jax.ShapeDtypeStruct((n, kd), BF16), jax.ShapeDtypeStruct((n, kd), BF16),
                   jax.ShapeDtypeStruct((n, vd), BF16), jax.ShapeDtypeStruct((n, vd), BF16),
                   jax.ShapeDtypeStruct((n, 2 * kd), F32)],
        compiler_params=_cparams(("parallel",)),
        name="gla_in",
    )(x, w, w2, gb)


def _gla_direction(q_ref, k_ref, v_ref, la_ref, st_ref, o_ref, reverse):
    rows, cr = q_ref.shape[0], GLA_CHUNK
    nch = rows // cr
    shift = cr.bit_length() - 1
    ri = lax.broadcasted_iota(I32, (rows, rows), 0)
    ci = lax.broadcasted_iota(I32, (rows, rows), 1)
    same_chunk = jnp.right_shift(ri, shift) == jnp.right_shift(ci, shift)
    within = (ci >= ri) if reverse else (ci <= ri)
    cum_mat = jnp.where(same_chunk & within, 1.0, 0.0).astype(BF16)
    la_all = la_ref[...]
    la_hi = la_all.astype(BF16)
    rest = la_all - la_hi.astype(F32)
    la_mid = rest.astype(BF16)
    la_lo = (rest - la_mid.astype(F32)).astype(BF16)
    c_heads = (jnp.dot(cum_mat, la_hi, preferred_element_type=F32)
               + jnp.dot(cum_mat, la_mid, preferred_element_type=F32)
               + jnp.dot(cum_mat, la_lo, preferred_element_type=F32))
    rj = lax.broadcasted_iota(I32, (cr, cr), 0)
    cj = lax.broadcasted_iota(I32, (cr, cr), 1)
    mask = (cj > rj) if reverse else (cj <= rj)

    for head in range(st_ref.shape[0]):
        kcols = slice(head * GLA_HEAD_K, (head + 1) * GLA_HEAD_K)
        vcols = slice(head * GLA_HEAD_V, (head + 1) * GLA_HEAD_V)
        la, c_all = la_all[:, kcols], c_heads[:, kcols]
        qes, decs, intra, upd = [], [], [], []
        for ch in range(nch):
            lo = ch * cr
            c = c_all[lo:lo + cr]
            qf = q_ref[lo:lo + cr, kcols].astype(F32)
            kf = k_ref[lo:lo + cr, kcols].astype(F32)
            v = v_ref[lo:lo + cr, vcols]
            ctot = c[0:1] if reverse else c[cr - 1:cr]
            qes.append((qf * jnp.exp2(c)).astype(BF16))
            decs.append(jnp.exp2(ctot))
            blocks = []
            for i in range(cr // GLA_SUB):
                blo, bhi = i * GLA_SUB, (i + 1) * GLA_SUB
                edge = bhi - 1 if reverse else blo
                ref = c[edge:edge + 1] - la[lo + edge:lo + edge + 1]
                qi = (qf[blo:bhi] * jnp.exp2(c[blo:bhi] - ref)).astype(BF16)
                ki = (kf * jnp.exp2(jnp.minimum(ref - c, GLA_EXP2_CLAMP))).astype(BF16)
                blocks.append(lax.dot_general(qi, ki, NT_DIMS, preferred_element_type=F32))
            scores = jnp.where(mask, jnp.concatenate(blocks, axis=0), 0.0).astype(BF16)
            intra.append(jnp.dot(scores, v, preferred_element_type=F32))
            kd = (kf * jnp.exp2(ctot - c)).astype(BF16)
            upd.append(lax.dot_general(v, kd, TN_DIMS, preferred_element_type=F32))

        st = st_ref[head]
        states = [None] * nch
        for ch in (reversed(range(nch)) if reverse else range(nch)):
            states[ch] = st
            st = st * decs[ch] + upd[ch]
        st_ref[head] = st
        for ch in range(nch):
            inter = lax.dot_general(qes[ch], states[ch].astype(BF16), NT_DIMS, preferred_element_type=F32)
            o_ref[ch * cr:(ch + 1) * cr, vcols] = (intra[ch] + inter).astype(BF16)


def _gla_scan_kernel(qf_ref, kf_ref, vf_ref, laf_ref, qb_ref, kb_ref, vb_ref, lab_ref,
                     of_ref, ob_ref, sf_ref, sb_ref):
    @pl.when(pl.program_id(2) == 0)
    def _():
        sf_ref[...] = jnp.zeros_like(sf_ref)
        sb_ref[...] = jnp.zeros_like(sb_ref)

    _gla_direction(qf_ref, kf_ref, vf_ref, laf_ref, sf_ref, of_ref, False)
    _gla_direction(qb_ref, kb_ref, vb_ref, lab_ref, sb_ref, ob_ref, True)


def _gla_scan(q, k, v, la, batch):
    n = q.shape[0]
    nblk = n // batch // GLA_ROWS
    hps = GLA_HEADS_PER_STEP
    hk, hv, ngrp = hps * GLA_HEAD_K, hps * GLA_HEAD_V, GLA_HEADS // hps
    fwd = lambda b, h, i: (b * nblk + i, h)
    bwd = lambda b, h, i: (b * nblk + nblk - 1 - i, h)
    bwd_la = lambda b, h, i: (b * nblk + nblk - 1 - i, ngrp + h)
    state = pltpu.VMEM((hps, GLA_HEAD_V, GLA_HEAD_K), F32)
    return pl.pallas_call(
        _gla_scan_kernel,
        grid=(batch, ngrp, nblk),
        in_specs=[pl.BlockSpec((GLA_ROWS, hk), fwd), pl.BlockSpec((GLA_ROWS, hk), fwd),
                  pl.BlockSpec((GLA_ROWS, hv), fwd), pl.BlockSpec((GLA_ROWS, hk), fwd),
                  pl.BlockSpec((GLA_ROWS, hk), bwd), pl.BlockSpec((GLA_ROWS, hk), bwd),
                  pl.BlockSpec((GLA_ROWS, hv), bwd), pl.BlockSpec((GLA_ROWS, hk), bwd_la)],
        out_specs=[pl.BlockSpec((GLA_ROWS, hv), fwd), pl.BlockSpec((GLA_ROWS, hv), bwd)],
        out_shape=[jax.ShapeDtypeStruct((n, GLA_VAL_DIM), BF16)] * 2,
        scratch_shapes=[state, state],
        compiler_params=_cparams(("parallel", "parallel", "arbitrary")),
        name="gla_scan",
    )(q, k, v, la, q, k, v, la)


def _gla_out_kernel(of_ref, ob_ref, gate_ref, ng_ref, w_ref, x_ref, g_ref, b_ref, o_ref):
    o = of_ref[...].astype(F32) + ob_ref[...].astype(F32)
    parts = []
    for h in range(GLA_HEADS):
        oh = o[:, h * GLA_HEAD_V:(h + 1) * GLA_HEAD_V]
        ms = jnp.mean(oh * oh, axis=-1, keepdims=True)
        parts.append(oh * lax.rsqrt(ms + GLA_NORM_EPS))
    on = jnp.concatenate(parts, axis=-1) * ng_ref[...]
    a = (on * gate_ref[...].astype(F32)).astype(BF16)
    h = jnp.dot(a, w_ref[...], preferred_element_type=F32)
    o_ref[...] = _ln_rows(ALPHA * x_ref[...] + h, g_ref[...], b_ref[...])


def _gla_mixer_ln(x, batch, w_in, gate_w2, gate_b, norm_g, w_out, ln_g, ln_b):
    n, d = x.shape
    q, k, v, gate, la = _gla_in(x, w_in, gate_w2, gate_b)
    o_f, o_b = _gla_scan(q, k, v, la, batch)
    row = lambda i: (i, 0)
    fix = lambda i: (0, 0)
    vd = GLA_VAL_DIM
    return pl.pallas_call(
        _gla_out_kernel,
        grid=(n // ROW_TILE,),
        in_specs=[pl.BlockSpec((ROW_TILE, vd), row), pl.BlockSpec((ROW_TILE, vd), row),
                  pl.BlockSpec((ROW_TILE, vd), row), pl.BlockSpec((1, vd), fix),
                  pl.BlockSpec((vd, d), fix), pl.BlockSpec((ROW_TILE, d), row),
                  pl.BlockSpec((1, d), fix), pl.BlockSpec((1, d), fix)],
        out_specs=pl.BlockSpec((ROW_TILE, d), row),
        out_shape=jax.ShapeDtypeStruct((n, d), F32),
        compiler_params=_cparams(("parallel",)),
        name="gla_out",
    )(o_f, o_b, gate, _row2(jnp.tile(norm_g, GLA_HEADS)), w_out.astype(BF16), x,
      _row2(ln_g), _row2(ln_b))


def _hy_in_kernel(xe_ref, xo_ref, w_ref, b_ref, cw_ref, cb_ref, ue_ref, uo_ref):
    w, b = w_ref[...], b_ref[...]
    pe = jnp.dot(xe_ref[...], w, preferred_element_type=F32) + b
    po = jnp.dot(xo_ref[...], w, preferred_element_type=F32) + b
    rows = pe.shape[0]
    ri = lax.broadcasted_iota(I32, (rows, 1), 0)
    po_prev = jnp.where(ri == 0, 0.0, pltpu.roll(po, 1, axis=0))
    pe_next = jnp.where(ri == rows - 1, 0.0, pltpu.roll(pe, rows - 1, axis=0))
    cw, cb = cw_ref[...], cb_ref[...]
    ue_ref[...] = (po_prev * cw[0:1] + pe * cw[1:2] + po * cw[2:3] + cb).astype(BF16)
    uo_ref[...] = (pe * cw[0:1] + po * cw[1:2] + pe_next * cw[2:3] + cb).astype(BF16)


def _hy_in(xe, xo, batch, w_in, b_in, conv_w, conv_b):
    n2, d = xe.shape
    half = n2 // batch
    width = w_in.shape[1]
    cols = 256
    seq_spec = pl.BlockSpec((half, d), lambda b, j: (b, 0))
    col_spec = lambda r: pl.BlockSpec((r, cols), lambda b, j: (0, j))
    out_spec = pl.BlockSpec((half, cols), lambda b, j: (b, j))
    return pl.pallas_call(
        _hy_in_kernel,
        grid=(batch, width // cols),
        in_specs=[seq_spec, seq_spec, pl.BlockSpec((d, cols), lambda b, j: (0, j)), col_spec(1), col_spec(3),
                  col_spec(1)],
        out_specs=[out_spec, out_spec],
        out_shape=[jax.ShapeDtypeStruct((n2, width), BF16)] * 2,
        compiler_params=_cparams(("parallel", "arbitrary")),
        name="hy_in",
    )(xe, xo, w_in.astype(BF16), _row2(b_in), conv_w, _row2(conv_b))


def _hy_filter_kernel(pe_ref, w1_ref, b1_ref, f_ref, w2_ref, b2_ref, w3_ref, win_ref, hs_ref, hd_ref):
    f = f_ref[...]
    h = jnp.sin(f[0:1] * (jnp.dot(pe_ref[...], w1_ref[...], precision=HIGHEST,
                                  preferred_element_type=F32) + b1_ref[...]))
    h = jnp.sin(f[1:2] * (jnp.dot(h, w2_ref[...], precision=HIGHEST,
                                  preferred_element_type=F32) + b2_ref[...]))
    h = jnp.dot(h, w3_ref[...], precision=HIGHEST, preferred_element_type=F32)
    win = win_ref[...]
    d = win.shape[1]
    rows = win.shape[0]
    t = pl.program_id(0) * rows + lax.broadcasted_iota(I32, (rows, 1), 0)
    for o in range(HYENA_ORDER):
        hf = h[:, 2 * o * d:(2 * o + 1) * d] * win
        hb = jnp.where(t == 0, 0.0, h[:, (2 * o + 1) * d:(2 * o + 2) * d] * win)
        hs_ref[:, o * d:(o + 1) * d] = (hf + hb).astype(BF16)
        hd_ref[:, o * d:(o + 1) * d] = (hf - hb).astype(BF16)


def _hy_spec_kernel(c_ref, s_ref, hs_ref, hd_ref, hr_ref, hi_ref, hn_ref):
    hs = hs_ref[...]
    a1 = jnp.dot(c_ref[...], hs, preferred_element_type=F32)
    a2 = jnp.dot(s_ref[...], hd_ref[...], preferred_element_type=F32)
    a3 = jnp.dot(s_ref[...], hs, preferred_element_type=F32)
    ri = lax.broadcasted_iota(I32, a1.shape, 0) + pl.program_id(0) * a1.shape[0]
    hr_ref[...] = a1
    hi_ref[...] = jnp.where(ri == 0, 0.0, a2)
    hn_ref[...] = jnp.where(ri == 0, a3, a1)


def _dft_consts(seq):
    k = jnp.arange(seq, dtype=I32)
    step = LANES
    assert seq % step == 0
    phase = lambda t: ((k[:, None] * t[None, :]) % (2 * seq)).astype(F32) * (math.pi / seq)
    ang_hi = phase(jnp.arange(seq // step, dtype=I32) * step)[:, :, None]
    ang_lo = phase(jnp.arange(step, dtype=I32))[:, None, :]
    cos = (jnp.cos(ang_hi) * jnp.cos(ang_lo) - jnp.sin(ang_hi) * jnp.sin(ang_lo)).reshape(seq, seq)
    sin = (jnp.sin(ang_hi) * jnp.cos(ang_lo) + jnp.cos(ang_hi) * jnp.sin(ang_lo)).reshape(seq, seq)
    sgn = jnp.where(k % 2 == 0, 1.0, -1.0).astype(F32)
    fwd_i = jnp.where(k[:, None] == 0, sgn[None, :], -sin)
    return cos.astype(BF16), fwd_i.astype(BF16)


def _hy_filters(seq, consts, w1, b1, sin_freq, w2, b2, w3):
    cos, fwd_i = consts[0], consts[1]
    d = w3.shape[-1] // (2 * HYENA_ORDER)
    hid = HYENA_FILTER_HIDDEN
    t = jnp.arange(seq, dtype=F32)
    t_norm = t / max(seq - 1, 1)
    bands = (HYENA_POS_EMB - 1) // 2
    freqs = jnp.linspace(1e-4, bands - 1, bands, dtype=F32)
    ang = (2.0 * math.pi / seq) * t[:, None] * freqs[None, :]
    pe = jnp.concatenate([t_norm[:, None], jnp.cos(ang), -jnp.sin(ang)], axis=-1)
    pe = jnp.pad(pe, ((0, 0), (0, LANES - HYENA_POS_EMB)))
    min_decay = math.log(HYENA_DECAY_TARGET) / HYENA_SLOW_DECAY
    max_decay = math.log(HYENA_DECAY_TARGET) / HYENA_FAST_DECAY
    deltas = jnp.abs(jnp.linspace(min_decay, max_decay, d, dtype=F32))
    window = jnp.exp(-t_norm[:, None] * deltas[None, :]) + HYENA_WINDOW_SHIFT
    padc = LANES - hid
    w1p = jnp.pad(w1, ((0, LANES - HYENA_POS_EMB), (0, padc)))
    w2p = jnp.pad(w2, ((0, padc), (0, padc)))
    w3p = jnp.pad(w3, ((0, padc), (0, 0)))
    rows = 256
    fix = lambda i: (0, 0)
    row = lambda i: (i, 0)
    hs, hd = pl.pallas_call(
        _hy_filter_kernel,
        grid=(seq // rows,),
        in_specs=[pl.BlockSpec((rows, LANES), row), pl.BlockSpec((LANES, LANES), fix),
                  pl.BlockSpec((1, LANES), fix), pl.BlockSpec((2, LANES), fix),
                  pl.BlockSpec((LANES, LANES), fix), pl.BlockSpec((1, LANES), fix),
                  pl.BlockSpec((LANES, 2 * HYENA_ORDER * d), fix), pl.BlockSpec((rows, d), row)],
        out_specs=[pl.BlockSpec((rows, HYENA_ORDER * d), row)] * 2,
        out_shape=[jax.ShapeDtypeStruct((seq, HYENA_ORDER * d), BF16)] * 2,
        compiler_params=_cparams(("parallel",)),
        name="hy_filter",
    )(pe, w1p, _row2(jnp.pad(b1, (0, padc))), jnp.pad(sin_freq, ((0, 0), (0, padc))), w2p,
      _row2(jnp.pad(b2, (0, padc))), w3p, window)
    tf, tn = DFT_TILE, DFT_COLS
    plane = jax.ShapeDtypeStruct((seq, HYENA_ORDER * d), F32)
    hr, hi, hn = pl.pallas_call(
        _hy_spec_kernel,
        grid=(seq // tf, HYENA_ORDER * d // tn),
        in_specs=[pl.BlockSpec((tf, seq), lambda i, j: (i, 0)), pl.BlockSpec((tf, seq), lambda i, j: (i, 0)),
                  pl.BlockSpec((seq, tn), lambda i, j: (0, j)), pl.BlockSpec((seq, tn), lambda i, j: (0, j))],
        out_specs=[pl.BlockSpec((tf, tn), lambda i, j: (i, j))] * 3,
        out_shape=[plane] * 3,
        compiler_params=_cparams(("parallel", "arbitrary")),
        name="hy_spec",
    )(cos, fwd_i, hs, hd)
    half = seq // 2
    har = hr[:half]
    hai = jnp.concatenate([hr[half:half + 1], hi[1:half]], axis=0)
    hbr = jnp.concatenate([hn[0:1], jnp.flip(hr[half + 1:], axis=0)], axis=0)
    hbi = jnp.concatenate([hi[half:half + 1], jnp.flip(hi[half + 1:], axis=0)], axis=0)
    return har, hai, hbr, hbi


def _dft_fwd_kernel(c_ref, s_ref, e_ref, o_ref, tw_ref, har_ref, hai_ref, hbr_ref, hbi_ref,
                    v0r_ref, v0i_ref, v1r_ref, v1i_ref):
    cm, sm = c_ref[...], s_ref[...]
    tw = tw_ref[...]
    c, s = tw[:, 0:1], tw[:, 1:2]
    rows = cm.shape[0]
    row0 = (lax.broadcasted_iota(I32, (rows, 1), 0) + pl.program_id(0) * rows) == 0
    u = jnp.where(row0, 1.0, 2.0)
    subs = [slice(lo, lo + DFT_SUBCOLS) for lo in range(0, e_ref.shape[2], DFT_SUBCOLS)]
    prods = []
    for cs in subs:
        e, o = e_ref[0, :, cs], o_ref[0, :, cs]
        prods.append((jnp.dot(cm, e, preferred_element_type=F32), jnp.dot(sm, e, preferred_element_type=F32),
                      jnp.dot(cm, o, preferred_element_type=F32), jnp.dot(sm, o, preferred_element_type=F32)))
    for cs, (er, ei, orr, oi) in zip(subs, prods):
        tr = c * orr + s * oi
        ti = c * oi - s * orr
        ar, br = er + tr, er - tr
        ai = jnp.where(row0, ei, ei + ti)
        bi = jnp.where(row0, -oi, ti - ei)
        har, hai, hbr, hbi = har_ref[:, cs], hai_ref[:, cs], hbr_ref[:, cs], hbi_ref[:, cs]
        yar = jnp.where(row0, ar * har, ar * har - ai * hai)
        ybr = jnp.where(row0, br * hbr, br * hbr - bi * hbi)
        yai = jnp.where(row0, ai * hai - bi * hbi, ar * hai + ai * har)
        ybi = jnp.where(row0, ai * hbi + bi * hai, br * hbi + bi * hbr)
        dr, di = yar - ybr, yai + ybi
        v0r_ref[0, :, cs] = (u * (yar + ybr)).astype(BF16)
        v0i_ref[0, :, cs] = jnp.where(row0, 2.0 * yai, u * (yai - ybi)).astype(BF16)
        v1r_ref[0, :, cs] = (u * (c * dr - s * di)).astype(BF16)
        v1i_ref[0, :, cs] = jnp.where(row0, -2.0 * ybi, u * (c * di + s * dr)).astype(BF16)


def _dft_inv_kernel(gr_ref, gi_ref, v0r_ref, v0i_ref, v1r_ref, v1i_ref, ze_ref, zo_ref, ge_ref, go_ref,
                    fb_ref, ye_ref, yo_ref):
    gr, gi = gr_ref[...], gi_ref[...]
    fb = fb_ref[...]
    ye = jnp.dot(gr, v0r_ref[0], preferred_element_type=F32) + jnp.dot(gi, v0i_ref[0], preferred_element_type=F32)
    ye_ref[0] = (ge_ref[0].astype(F32) * (ye + ze_ref[0].astype(F32) * fb)).astype(BF16)
    yo = jnp.dot(gr, v1r_ref[0], preferred_element_type=F32) + jnp.dot(gi, v1i_ref[0], preferred_element_type=F32)
    yo_ref[0] = (go_ref[0].astype(F32) * (yo + zo_ref[0].astype(F32) * fb)).astype(BF16)


def _long_conv_gate(consts, planes, order, z_e, z_o, gate_e, gate_o, filt_bias):
    cos, fwd_i, inv_r, inv_i, twiddle = consts
    (ze_arr, ze_col), (zo_arr, zo_col) = z_e, z_o
    (ge_arr, ge_col), (go_arr, go_col) = gate_e, gate_o
    batch, half, _ = ze_arr.shape
    d = D_MODEL
    tf, tn = DFT_TILE, DFT_COLS
    nj = d // tn
    plane_spec = pl.BlockSpec((tf, tn), lambda i, j, b: (i, order * nj + j))
    mat_f = pl.BlockSpec((tf, half), lambda i, j, b: (i, 0))
    cols_f = lambda off: pl.BlockSpec((1, half, tn), lambda i, j, b: (b, 0, off + j))
    tile_f = pl.BlockSpec((1, tf, tn), lambda i, j, b: (b, i, j))
    v_shape = jax.ShapeDtypeStruct((batch, half, d), BF16)
    v = pl.pallas_call(
        _dft_fwd_kernel,
        grid=(half // tf, nj, batch),
        in_specs=[mat_f, mat_f, cols_f(ze_col), cols_f(zo_col), pl.BlockSpec((tf, LANES), lambda i, j, b: (i, 0)),
                  plane_spec, plane_spec, plane_spec, plane_spec],
        out_specs=[tile_f] * 4,
        out_shape=[v_shape] * 4,
        compiler_params=_cparams(("parallel", "parallel", "arbitrary")),
        name="dft_fwd",
    )(cos, fwd_i, ze_arr, zo_arr, twiddle, *planes)
    mat_i = pl.BlockSpec((tf, half), lambda b, j, i: (i, 0))
    cols_i = pl.BlockSpec((1, half, tn), lambda b, j, i: (b, 0, j))
    tile_i = lambda off: pl.BlockSpec((1, tf, tn), lambda b, j, i: (b, i, off + j))
    return pl.pallas_call(
        _dft_inv_kernel,
        grid=(batch, nj, half // tf),
        in_specs=[mat_i, mat_i, cols_i, cols_i, cols_i, cols_i, tile_i(ze_col), tile_i(zo_col),
                  tile_i(ge_col), tile_i(go_col), pl.BlockSpec((1, tn), lambda b, j, i: (0, j))],
        out_specs=[tile_i(0), tile_i(0)],
        out_shape=[v_shape, v_shape],
        compiler_params=_cparams(("parallel", "parallel", "arbitrary")),
        name="dft_inv",
    )(inv_r, inv_i, *v, ze_arr, zo_arr, ge_arr, go_arr, _row2(filt_bias))


def _proj_ln_pair_kernel(ae_ref, ao_ref, w_ref, bias_ref, x_ref, g_ref, b_ref, o_ref, mix_ref):
    w, bias = w_ref[...], bias_ref[...]
    half = ae_ref.shape[0]
    for parity, a_ref in ((0, ae_ref), (1, ao_ref)):
        h = jnp.dot(a_ref[...], w, preferred_element_type=F32) + bias
        for c in range(h.shape[1] // LANES):
            mix_ref[c, pl.ds(parity, half, stride=2), :] = h[:, c * LANES:(c + 1) * LANES]
    h = jnp.concatenate([mix_ref[c] for c in range(mix_ref.shape[0])], axis=1)
    o_ref[...] = _ln_rows(ALPHA * x_ref[...] + h, g_ref[...], b_ref[...])


def _proj_ln_pair(a_e, a_o, w, bias, x, g, b):
    n, d = x.shape
    rows = ROW_TILE // 2
    row = lambda i: (i, 0)
    fix = lambda i: (0, 0)
    return pl.pallas_call(
        _proj_ln_pair_kernel,
        grid=(n // ROW_TILE,),
        in_specs=[pl.BlockSpec((rows, d), row), pl.BlockSpec((rows, d), row), pl.BlockSpec(w.shape, fix),
                  pl.BlockSpec((1, d), fix), pl.BlockSpec((ROW_TILE, d), row),
                  pl.BlockSpec((1, d), fix), pl.BlockSpec((1, d), fix)],
        out_specs=pl.BlockSpec((ROW_TILE, d), row),
        out_shape=jax.ShapeDtypeStruct((n, d), F32),
        scratch_shapes=[pltpu.VMEM((d // LANES, ROW_TILE, LANES), F32)],
        compiler_params=_cparams(("parallel",)),
        name="proj_ln_pair",
    )(a_e, a_o, w, _row2(bias), x, _row2(g), _row2(b))


def _hyena_mixer_ln(x, batch, w_in, b_in, conv_w, conv_b, ffn_w1, ffn_b1, sin_freq, ffn_w2, ffn_b2,
                    ffn_w3, filt_bias, w_out, b_out, ln_g, ln_b):
    n, d = x.shape
    seq = n // batch
    half = seq // 2
    nj = d // DFT_COLS
    planes = _hy_filters(seq, _dft_consts(seq), ffn_w1, ffn_b1, sin_freq, ffn_w2, ffn_b2, ffn_w3)
    cos, fwd_i = _dft_consts(half)
    scale = 0.5 / seq
    k = jnp.arange(half, dtype=F32)[:, None] * (math.pi / seq)
    twiddle = jnp.pad(jnp.concatenate([jnp.cos(k), jnp.sin(k)], axis=1), ((0, 0), (0, LANES - 2)))
    consts = (cos, fwd_i, cos * scale, fwd_i.T * scale, twiddle)
    xs = x.reshape(batch, half, 2, d).astype(BF16)
    u_e, u_o = _hy_in(xs[:, :, 0].reshape(n // 2, d), xs[:, :, 1].reshape(n // 2, d), batch, w_in, b_in,
                      conv_w, conv_b)
    u_e, u_o = u_e.reshape(batch, half, 3 * d), u_o.reshape(batch, half, 3 * d)
    y_e, y_o = _long_conv_gate(consts, planes, 0, (u_e, 0), (u_o, 0), (u_e, nj), (u_o, nj), filt_bias[0])
    y_e, y_o = _long_conv_gate(consts, planes, 1, (y_e, 0), (y_o, 0), (u_e, 2 * nj), (u_o, 2 * nj), filt_bias[1])
    return _proj_ln_pair(y_e.reshape(n // 2, d), y_o.reshape(n // 2, d), w_out.astype(BF16), b_out, x,
                         ln_g, ln_b)


def _na_in_kernel(x_ref, w_ref, b_ref, q_ref, k_ref, v_ref):
    d = D_MODEL
    p = jnp.dot(x_ref[...].astype(BF16), w_ref[...], preferred_element_type=F32) + b_ref[...]
    q_ref[...] = (p[:, :d] * (NA_HEAD_DIM ** -0.5 * LOG2E)).astype(BF16)
    k_ref[...] = p[:, d:2 * d].astype(BF16)
    v_ref[...] = p[:, 2 * d:].astype(BF16)


def _na_attn_kernel(q_ref, k_ref, vt_ref, bias_ref, o_ref):
    for i in range(NA_PAIRS):
        rp = pl.program_id(2) * NA_PAIRS + i
        o_ref[i * LANES:(i + 1) * LANES] = _na_pair(rp, q_ref[i * LANES:(i + 1) * LANES], k_ref, vt_ref, bias_ref)


def _na_pair(rp, q, k_ref, vt_ref, bias_ref):
    hpg = LANES // NA_HEAD_DIM
    n_rows = GRID_W
    r0 = 2 * rp
    rs0 = jnp.clip(r0 - NA_WIN_ROWS // 2, 0, n_rows - NA_WIN_ROWS)
    rs1 = jnp.clip(r0 + 1 - NA_WIN_ROWS // 2, 0, n_rows - NA_WIN_ROWS)
    start = jnp.minimum(rs0, n_rows - NA_KEY_ROWS)
    tok0 = pl.multiple_of(start * GRID_W, LANES)
    nkeys = NA_KEY_ROWS * GRID_W
    kw = k_ref[pl.ds(tok0, nkeys), :]
    vtw = vt_ref[0, :, pl.ds(tok0, nkeys)]
    lane = lax.broadcasted_iota(I32, (1, LANES), 1)
    sub = lax.broadcasted_iota(I32, (LANES, 1), 0)
    right = lane >= GRID_W
    acc = jnp.zeros((LANES, LANES), F32)
    for h in range(hpg):
        in_head = (lane >= h * NA_HEAD_DIM) & (lane < (h + 1) * NA_HEAD_DIM)
        qh = jnp.where(in_head, q, jnp.zeros_like(q))
        s = lax.dot_general(kw, qh, NT_DIMS, preferred_element_type=F32)
        tiles = []
        for a in range(NA_KEY_ROWS):
            kr = start + a
            dr0 = kr - r0 + (NA_WIN_ROWS - 1)
            bt = bias_ref[h, jnp.clip(dr0, 0, 2 * NA_WIN_ROWS - 1)]
            if a < 2 or a >= NA_WIN_ROWS:
                pen0 = jnp.where((kr >= rs0) & (kr < rs0 + NA_WIN_ROWS), 0.0, NEG)
                pen1 = jnp.where((kr >= rs1) & (kr < rs1 + NA_WIN_ROWS), 0.0, NEG)
                bt = bt + jnp.where(right, pen1, pen0)
            tiles.append(bt)
        s = s + jnp.concatenate(tiles, axis=0)
        m = jnp.max(s, axis=0, keepdims=True)
        p = jnp.exp2(s - m)
        inv = 1.0 / jnp.sum(p, axis=0, keepdims=True)
        in_head_rows = (sub >= h * NA_HEAD_DIM) & (sub < (h + 1) * NA_HEAD_DIM)
        pv = jnp.dot(vtw, p.astype(BF16), preferred_element_type=F32)
        acc = acc + jnp.where(in_head_rows, pv * inv, 0.0)
    return acc.T.astype(BF16)


def _na_bias_table(rpb):
    nh = rpb.shape[0]
    kc = jnp.arange(GRID_W)[:, None]
    qc = jnp.arange(GRID_W)[None, :]
    cs = jnp.clip(qc - NA_WIN_COLS // 2, 0, GRID_W - NA_WIN_COLS)
    inwin = (kc >= cs) & (kc < cs + NA_WIN_COLS)
    dc = jnp.clip(kc - qc + (NA_WIN_COLS - 1), 0, 2 * NA_WIN_COLS - 2)
    colb = jnp.where(inwin[None, None], rpb.astype(F32)[:, :, dc] * LOG2E, NEG)
    pad = jnp.full((nh, 1, GRID_W, GRID_W), NEG, F32)
    left = jnp.concatenate([colb, pad], axis=1)
    rgt = jnp.concatenate([pad, colb], axis=1)
    return jnp.concatenate([left, rgt], axis=-1)


def _na_mixer_ln(x, batch, w_in, b_in, rpb, w_out, b_out, ln_g, ln_b):
    n, d = x.shape
    seq = n // batch
    assert seq == GRID_W * GRID_W
    row = lambda i: (i, 0)
    fix = lambda i: (0, 0)
    qkv_shape = jax.ShapeDtypeStruct((n, d), BF16)
    q, k, v = pl.pallas_call(
        _na_in_kernel,
        grid=(n // ROW_TILE,),
        in_specs=[pl.BlockSpec((ROW_TILE, d), row), pl.BlockSpec((d, 3 * d), fix), pl.BlockSpec((1, 3 * d), fix)],
        out_specs=[pl.BlockSpec((ROW_TILE, d), row)] * 3,
        out_shape=[qkv_shape] * 3,
        compiler_params=_cparams(("parallel",)),
        name="na_in",
    )(x, w_in.astype(BF16), _row2(b_in))
    vt = jnp.swapaxes(v.reshape(batch, seq, d), 1, 2)
    bias = _na_bias_table(rpb)
    hpg = LANES // NA_HEAD_DIM
    n_pairs = GRID_W // 2
    n_steps = n_pairs // NA_PAIRS
    o = pl.pallas_call(
        _na_attn_kernel,
        grid=(batch, d // LANES, n_steps),
        in_specs=[pl.BlockSpec((NA_PAIRS * LANES, LANES), lambda b, g, r: (b * n_steps + r, g)),
                  pl.BlockSpec((seq, LANES), lambda b, g, r: (b, g)),
                  pl.BlockSpec((1, LANES, seq), lambda b, g, r: (b, g, 0)),
                  pl.BlockSpec((hpg, 2 * NA_WIN_ROWS, GRID_W, LANES), lambda b, g, r: (g, 0, 0, 0))],
        out_specs=pl.BlockSpec((NA_PAIRS * LANES, LANES), lambda b, g, r: (b * n_steps + r, g)),
        out_shape=qkv_shape,
        compiler_params=_cparams(("parallel", "parallel", "arbitrary")),
        name="na_attn",
    )(q, k, vt, bias)
    return _proj_ln(o, w_out.astype(BF16), b_out, x, ln_g, ln_b)


def _moe_route_kernel(x_ref, wr_ref, rb_ref, xs_ref, pos_ref, gate_ref, cnt_ref):
    ne, tb = N_EXPERTS, MOE_TB
    xb, x_lo = _split_bf16(x_ref[...])
    nt = lambda p, q: lax.dot_general(p, q, NT_DIMS, preferred_element_type=F32)
    logits = nt(wr_ref[0], xb) + nt(wr_ref[1], xb) + nt(wr_ref[0], x_lo) + rb_ref[...]
    eio = lax.broadcasted_iota(I32, (ne, tb), 0)
    work = logits
    vals, hots = [], []
    for _ in range(TOP_K):
        m = jnp.max(work, axis=0, keepdims=True)
        idx = jnp.min(jnp.where(work == m, eio, ne), axis=0, keepdims=True)
        hot = eio == idx
        vals.append(m)
        hots.append(hot)
        work = jnp.where(hot, -jnp.inf, work)
    exps = [jnp.exp(v - vals[0]) for v in vals]
    denom = exps[0] + exps[1] + exps[2] + exps[3]
    sel = jnp.zeros((ne, tb), F32)
    for hot in hots:
        sel = sel + jnp.where(hot, 1.0, 0.0)
    earlier = (lax.broadcasted_iota(I32, (tb, tb), 0) < lax.broadcasted_iota(I32, (tb, tb), 1))
    rank = jnp.dot(sel.astype(BF16), jnp.where(earlier, 1.0, 0.0).astype(BF16),
                   preferred_element_type=F32)
    cnt = jnp.sum(sel, axis=1, keepdims=True)
    cnt_pad = jnp.ceil(cnt * (1.0 / MOE_G)) * MOE_G
    below = (lax.broadcasted_iota(I32, (ne, ne), 1) < lax.broadcasted_iota(I32, (ne, ne), 0))
    seg_off = jnp.dot(jnp.where(below, 1.0, 0.0), jnp.broadcast_to(cnt_pad, (ne, tb)),
                      precision=HIGHEST, preferred_element_type=F32)
    slot = seg_off + rank
    pos = [jnp.sum(jnp.where(hot, slot, 0.0), axis=0, keepdims=True).astype(I32) for hot in hots]
    zero_row = jnp.zeros((1, tb), I32)
    pos_ref[0] = jnp.concatenate(pos + [zero_row] * (8 - TOP_K), axis=0)
    gate_ref[0] = jnp.concatenate([e / denom for e in exps] + [zero_row.astype(F32)] * (8 - TOP_K), axis=0)
    cnt_ref[0] = jnp.broadcast_to(cnt_pad, (ne, LANES)).astype(I32)
    local = lax.broadcasted_iota(I32, (MOE_PCH, tb), 0).astype(F32).astype(BF16)
    one, zero = jnp.ones((), BF16), jnp.zeros((), BF16)
    for r in range(MOE_RB // MOE_PCH):
        onehot = None
        for p in pos:
            hit = jnp.where(local == (p - r * MOE_PCH).astype(F32).astype(BF16), one, zero)
            onehot = hit if onehot is None else onehot + hit
        xs_ref[r * MOE_PCH:(r + 1) * MOE_PCH] = jnp.dot(
            onehot, xb, preferred_element_type=F32).astype(BF16)


def _moe_expert_kernel(te_ref, src_ref, nt_ref, xs_hbm, wgu_ref, bgu_ref, wd_ref, bd_ref,
                       ys_hbm, xbuf, ybuf, wgu_bf, wd_bf, sem_in, sem_out, *, n_blocks):
    t = pl.program_id(0)
    nt = nt_ref[0]
    s = t % 2

    def in_copy(chunk, slot, c):
        return pltpu.make_async_copy(
            xs_hbm.at[pl.ds(pl.multiple_of(chunk * MOE_G, MOE_G), MOE_G)],
            xbuf.at[slot, pl.ds(c * MOE_G, MOE_G)], sem_in.at[slot])

    def out_copy(chunk, slot, c):
        return pltpu.make_async_copy(
            ybuf.at[slot, pl.ds(c * MOE_G, MOE_G)],
            ys_hbm.at[pl.ds(pl.multiple_of(chunk * MOE_G, MOE_G), MOE_G)], sem_out.at[slot])

    def tile_chunks(tile):
        return [src_ref[tile * MOE_CPT + c] for c in range(MOE_CPT)]

    def start_in(tile, slot):
        for c, chunk in enumerate(tile_chunks(tile)):
            in_copy(chunk, slot, c).start()

    def start_out(tile, slot):
        chunks = tile_chunks(tile)
        for c, chunk in enumerate(chunks):
            spare = ((slot * MOE_CPT + c) % n_blocks) * MOE_BLOCK_CHUNKS + MOE_SPARE_CHUNK
            out_copy(jnp.where(chunk == MOE_ZERO_CHUNK, spare, chunk), slot, c).start()

    def wait_in(slot):
        for c in range(MOE_CPT):
            in_copy(0, slot, c).wait()

    def wait_out(slot):
        for c in range(MOE_CPT):
            out_copy(0, slot, c).wait()

    @pl.when(t == 0)
    def _():
        start_in(0, 0)

    @pl.when(t + 1 < nt)
    def _():
        start_in(t + 1, 1 - s)

    @pl.when(t < nt)
    def _():
        @pl.when(jnp.logical_or(t == 0, te_ref[t] != te_ref[jnp.maximum(t - 1, 0)]))
        def _():
            wgu_bf[...] = wgu_ref[0, 0].astype(BF16)
            wd_bf[...] = wd_ref[0, 0].astype(BF16)

        wait_in(s)
        f = wd_bf.shape[0]
        h = jnp.dot(xbuf[s], wgu_bf[...], preferred_element_type=F32) + bgu_ref[0, 0]
        glu = jnp.minimum(h[:, :f], SWIGLU_LIMIT)
        lin = jnp.clip(h[:, f:], -SWIGLU_LIMIT, SWIGLU_LIMIT)
        act = (glu * jax.nn.sigmoid(SWIGLU_ALPHA * glu) * (lin + 1.0)).astype(BF16)
        y = jnp.dot(act, wd_bf[...], preferred_element_type=F32) + bd_ref[0, 0]

        @pl.when(t >= 2)
        def _():
            wait_out(s)

        ybuf[s] = y.astype(BF16)
        start_out(t, s)

        @pl.when(t == nt - 1)
        def _():
            wait_out(s)

            @pl.when(t >= 1)
            def _():
                wait_out(1 - s)


def _moe_combine_kernel(ys_ref, pos_ref, gate_ref, x_ref, g_ref, b_ref, o_ref):
    tb = MOE_TB
    pos = pos_ref[...]
    gate = gate_ref[...].astype(BF16)
    local = lax.broadcasted_iota(I32, (tb, MOE_PCH), 1).astype(F32).astype(BF16)
    zero = jnp.zeros((), BF16)
    f = jnp.zeros((tb, D_MODEL), F32)
    for r in range(MOE_RB // MOE_PCH):
        w = None
        for k in range(TOP_K):
            pk = (pos[:, k:k + 1] - r * MOE_PCH).astype(F32).astype(BF16)
            wk = jnp.where(local == pk, gate[:, k:k + 1], zero)
            w = wk if w is None else w + wk
        f = f + jnp.dot(w, ys_ref[r * MOE_PCH:(r + 1) * MOE_PCH], preferred_element_type=F32)
    o_ref[...] = _ln_rows(ALPHA * x_ref[...] + f, g_ref[...], b_ref[...])


def _moe_tile_tables(cnt_pad, n_tiles_max):
    nb, ne = cnt_pad.shape
    nch_eb = (cnt_pad // MOE_G).T
    seg_off = jnp.cumsum(cnt_pad, axis=1) - cnt_pad
    base_eb = ((jnp.arange(nb, dtype=I32)[:, None] * MOE_RB + seg_off) // MOE_G).T
    first_eb = jnp.cumsum(nch_eb, axis=1) - nch_eb
    tiles_e = (jnp.sum(nch_eb, axis=1) + MOE_CPT - 1) // MOE_CPT
    tile_end = jnp.cumsum(tiles_e)
    n_tiles = tile_end[-1]
    tau = jnp.arange(n_tiles_max, dtype=I32)
    tile_e = jnp.minimum(jnp.sum(tau[:, None] >= tile_end[None, :], axis=1), ne - 1).astype(I32)
    of_tile = tile_e[:, None] == jnp.arange(ne, dtype=I32)[None, :]
    pick = lambda tab: jnp.sum(jnp.where(of_tile[:, :, None], tab[None], 0), axis=1)
    tile_first = tau - jnp.sum(jnp.where(of_tile, (tile_end - tiles_e)[None, :], 0), axis=1)
    k = (tile_first * MOE_CPT)[:, None] + jnp.arange(MOE_CPT, dtype=I32)[None, :]
    first, count, base = pick(first_eb), pick(nch_eb), pick(base_eb)
    inside = (k[:, :, None] >= first[:, None, :]) & (k[:, :, None] < (first + count)[:, None, :])
    src = jnp.sum(jnp.where(inside, (base - first)[:, None, :] + k[:, :, None], 0), axis=2)
    src = jnp.where(jnp.any(inside, axis=2) & (tau < n_tiles)[:, None], src, MOE_ZERO_CHUNK)
    return tile_e, src.reshape(-1).astype(I32), n_tiles.astype(I32).reshape(1)


def _moe_ln(x, layer, router_w, router_b, w_gu, b_gu, w_down, b_down, ln_g, ln_b):
    n, d = x.shape
    ne, f = N_EXPERTS, w_down.shape[2]
    depth = w_gu.shape[0]
    expert_block = lambda t, te, sr, nt: (layer, te[t], 0, 0)
    nb = n // MOE_TB
    assert MOE_RB >= MOE_TB * TOP_K + ne * (MOE_G - 1) + 2 * MOE_G
    blk = lambda i: (i, 0)
    blk3 = lambda i: (i, 0, 0)
    fix = lambda i: (0, 0)
    slot_rows = nb * MOE_RB
    xs, pos, gates, cnt = pl.pallas_call(
        _moe_route_kernel,
        grid=(nb,),
        in_specs=[pl.BlockSpec((MOE_TB, d), blk), pl.BlockSpec((2, ne, d), lambda i: (0, 0, 0)),
                  pl.BlockSpec((ne, 1), fix)],
        out_specs=[pl.BlockSpec((MOE_RB, d), blk), pl.BlockSpec((1, 8, MOE_TB), blk3),
                   pl.BlockSpec((1, 8, MOE_TB), blk3), pl.BlockSpec((1, ne, LANES), blk3)],
        out_shape=[jax.ShapeDtypeStruct((slot_rows, d), BF16), jax.ShapeDtypeStruct((nb, 8, MOE_TB), I32),
                   jax.ShapeDtypeStruct((nb, 8, MOE_TB), F32), jax.ShapeDtypeStruct((nb, ne, LANES), I32)],
        compiler_params=_cparams(("parallel",)),
        name="moe_route",
    )(x, jnp.stack(_split_bf16(router_w.T)), router_b.reshape(ne, 1))

    n_tiles_max = nb * MOE_RB // MOE_TM + ne
    tile_e, src, n_tiles = _moe_tile_tables(cnt[:, :, 0], n_tiles_max)
    ys = pl.pallas_call(
        functools.partial(_moe_expert_kernel, n_blocks=nb),
        grid_spec=pltpu.PrefetchScalarGridSpec(
            num_scalar_prefetch=3,
            grid=(n_tiles_max,),
            in_specs=[pl.BlockSpec(memory_space=pl.ANY),
                      pl.BlockSpec((1, 1, d, 2 * f), expert_block),
                      pl.BlockSpec((1, 1, 1, 2 * f), expert_block),
                      pl.BlockSpec((1, 1, f, d), expert_block),
                      pl.BlockSpec((1, 1, 1, d), expert_block)],
            out_specs=pl.BlockSpec(memory_space=pl.ANY),
            scratch_shapes=[pltpu.VMEM((2, MOE_TM, d), BF16), pltpu.VMEM((2, MOE_TM, d), BF16),
                            pltpu.VMEM((d, 2 * f), BF16), pltpu.VMEM((f, d), BF16),
                            pltpu.SemaphoreType.DMA((2,)), pltpu.SemaphoreType.DMA((2,))]),
        out_shape=jax.ShapeDtypeStruct((slot_rows, d), BF16),
        input_output_aliases={3: 0},
        compiler_params=pltpu.CompilerParams(dimension_semantics=("arbitrary",),
                                             vmem_limit_bytes=MOE_EXPERT_VMEM),
        name="moe_expert",
    )(tile_e, src, n_tiles, xs, w_gu, b_gu.reshape(depth, ne, 1, 2 * f), w_down,
      b_down.reshape(depth, ne, 1, d))

    pos_t = jnp.swapaxes(pos, 1, 2).reshape(n, 8)
    gate_t = jnp.swapaxes(gates, 1, 2).reshape(n, 8)
    return pl.pallas_call(
        _moe_combine_kernel,
        grid=(nb,),
        in_specs=[pl.BlockSpec((MOE_RB, d), blk), pl.BlockSpec((MOE_TB, 8), blk), pl.BlockSpec((MOE_TB, 8), blk),
                  pl.BlockSpec((MOE_TB, d), blk), pl.BlockSpec((1, d), fix), pl.BlockSpec((1, d), fix)],
        out_specs=pl.BlockSpec((MOE_TB, d), blk),
        out_shape=jax.ShapeDtypeStruct((n, d), F32),
        compiler_params=_cparams(("parallel",)),
        name="moe_combine",
    )(ys, pos_t, gate_t, x, _row2(ln_g), _row2(ln_b))


def kernel(x, ln1_g, ln1_b, ln2_g, ln2_b, gla_w_in, gla_gate_w2, gla_gate_b, gla_norm_g, gla_w_out,
           hy_w_in, hy_b_in, hy_conv_w, hy_conv_b, hy_ffn_w1, hy_ffn_b1, hy_sin_freq, hy_ffn_w2,
           hy_ffn_b2, hy_ffn_w3, hy_filt_bias, hy_w_out, hy_b_out, na_w_in, na_b_in, na_rpb, na_w_out,
           na_b_out, moe_router_w, moe_router_b, moe_w_gu, moe_b_gu, moe_w_down, moe_b_down):
    batch, seq, d = x.shape
    xt = x.reshape(batch * seq, d)
    for i in range(DEPTH):
        m, j = i % N_MIXERS, i // N_MIXERS
        if m == 0:
            xt = _gla_mixer_ln(xt, batch, gla_w_in[j], gla_gate_w2[j], gla_gate_b[j], gla_norm_g[j],
                               gla_w_out[j], ln1_g[i], ln1_b[i])
        elif m == 1:
            xt = _hyena_mixer_ln(xt, batch, hy_w_in[j], hy_b_in[j], hy_conv_w[j], hy_conv_b[j],
                                 hy_ffn_w1[j], hy_ffn_b1[j], hy_sin_freq[j], hy_ffn_w2[j], hy_ffn_b2[j],
                                 hy_ffn_w3[j], hy_filt_bias[j], hy_w_out[j], hy_b_out[j],
                                 ln1_g[i], ln1_b[i])
        else:
            xt = _na_mixer_ln(xt, batch, na_w_in[j], na_b_in[j], na_rpb[j], na_w_out[j], na_b_out[j],
                              ln1_g[i], ln1_b[i])
        xt = _moe_ln(xt, i, moe_router_w[i], moe_router_b[i], moe_w_gu, moe_b_gu, moe_w_down,
                     moe_b_down, ln2_g[i], ln2_b[i])
    return xt.reshape(batch, seq, d)
```

```python
import functools
import math

import jax
import jax.numpy as jnp
from jax import lax
from jax.experimental import pallas as pl
from jax.experimental.pallas import tpu as pltpu

F32, BF16, I32 = jnp.float32, jnp.bfloat16, jnp.int32
HIGHEST = lax.Precision.HIGHEST
NT_DIMS = (((1,), (1,)), ((), ()))
TN_DIMS = (((0,), (0,)), ((), ()))

D_MODEL = 1024
DEPTH = 4
N_MIXERS = 3
GRID_W = 64
ALPHA = (2 * DEPTH) ** 0.25
LN_EPS = 1e-5

GLA_HEADS = 4
GLA_KEY_DIM = 512
GLA_VAL_DIM = 1024
GLA_HEAD_K = 128
GLA_HEAD_V = 256
GLA_GATE_RANK = 16
GLA_TAU = 16.0
GLA_NORM_EPS = 1e-6

HYENA_ORDER = 2
HYENA_POS_EMB = 33
HYENA_FILTER_HIDDEN = 64
HYENA_DECAY_TARGET = 1e-2
HYENA_FAST_DECAY = 0.3
HYENA_SLOW_DECAY = 1.5
HYENA_WINDOW_SHIFT = 0.05

NA_HEAD_DIM = 32
NA_HEADS = 32
NA_WIN_ROWS = 8
NA_WIN_COLS = 16

N_EXPERTS = 32
TOP_K = 4
SWIGLU_ALPHA = 1.702
SWIGLU_LIMIT = 7.0

LANES = 128
BF16_SUBLANES = 16
VMEM_LIMIT = 48 * 1024 * 1024
MOE_EXPERT_VMEM = 56 * 1024 * 1024

ROW_TILE = 512
GLA_CHUNK = 64
GLA_SUB = 16
GLA_ROWS = 256
GLA_HEADS_PER_STEP = 4
GLA_EXP2_CLAMP = 115.0
DFT_TILE = 512
DFT_COLS = 512
DFT_SUBCOLS = 256
NA_KEY_ROWS = 10
NA_PAIRS = 8
NEG = -1e30
LOG2E = math.log2(math.e)

MOE_TB = 512
MOE_G = BF16_SUBLANES
MOE_RB = MOE_TB * TOP_K + 512
MOE_PCH = 256
MOE_BLOCK_CHUNKS = MOE_RB // MOE_G
MOE_ZERO_CHUNK = MOE_BLOCK_CHUNKS - 1
MOE_SPARE_CHUNK = MOE_BLOCK_CHUNKS - 2
MOE_TM = 512
MOE_CPT = MOE_TM // MOE_G


def _cparams(sem):
    return pltpu.CompilerParams(dimension_semantics=sem, vmem_limit_bytes=VMEM_LIMIT)


def _ln_rows(y, g, b):
    mu = jnp.mean(y, axis=-1, keepdims=True)
    yc = y - mu
    var = jnp.mean(yc * yc, axis=-1, keepdims=True)
    return yc * lax.rsqrt(var + LN_EPS) * g + b


def _row2(v):
    return v.reshape(1, -1)


def _split_bf16(a):
    hi = a.astype(BF16)
    return hi, (a - hi.astype(F32)).astype(BF16)


def _dot_split(a, b_hi, b_lo, dims):
    a_hi, a_lo = _split_bf16(a)
    if dims is None:
        mm = lambda p, q: jnp.dot(p, q, preferred_element_type=F32)
    else:
        mm = lambda p, q: lax.dot_general(p, q, dims, preferred_element_type=F32)
    return mm(a_hi, b_hi) + mm(a_lo, b_hi) + mm(a_hi, b_lo)


def _proj_ln_kernel(a_ref, w_ref, bias_ref, x_ref, g_ref, b_ref, o_ref):
    h = jnp.dot(a_ref[...], w_ref[...], preferred_element_type=F32) + bias_ref[...]
    o_ref[...] = _ln_rows(ALPHA * x_ref[...] + h, g_ref[...], b_ref[...])


def _proj_ln(a, w, bias, x, g, b):
    n, d = x.shape
    row = lambda i: (i, 0)
    fix = lambda i: (0, 0)
    return pl.pallas_call(
        _proj_ln_kernel,
        grid=(n // ROW_TILE,),
        in_specs=[pl.BlockSpec((ROW_TILE, a.shape[1]), row), pl.BlockSpec(w.shape, fix),
                  pl.BlockSpec((1, d), fix), pl.BlockSpec((ROW_TILE, d), row),
                  pl.BlockSpec((1, d), fix), pl.BlockSpec((1, d), fix)],
        out_specs=pl.BlockSpec((ROW_TILE, d), row),
        out_shape=jax.ShapeDtypeStruct((n, d), F32),
        compiler_params=_cparams(("parallel",)),
        name="proj_ln",
    )(a, w, _row2(bias), x, _row2(g), _row2(b))


def _gla_in_kernel(x_ref, w_ref, w2_ref, gb_ref, q_ref, k_ref, v_ref, g_ref, la_ref):
    kd, vd = GLA_KEY_DIM, GLA_VAL_DIM
    xb = x_ref[...].astype(BF16)
    proj = lambda lo, hi: jnp.dot(xb, w_ref[:, lo:hi], preferred_element_type=F32)
    glow = proj(2 * kd + 2 * vd, 2 * kd + 2 * vd + LANES)
    z = _dot_split(glow, w2_ref[0], w2_ref[1], None) + gb_ref[...]
    la_ref[...] = (jnp.minimum(z, 0.0) - jnp.log(1.0 + jnp.exp(-jnp.abs(z)))) * (LOG2E / GLA_TAU)
    og = proj(2 * kd + vd, 2 * kd + 2 * vd)
    g_ref[...] = (og * jax.nn.sigmoid(og)).astype(BF16)
    q_ref[...] = (proj(0, kd) * GLA_HEAD_K ** -0.5).astype(BF16)
    k_ref[...] = proj(kd, 2 * kd).astype(BF16)
    v_ref[...] = proj(2 * kd, 2 * kd + vd).astype(BF16)


def _gla_in(x, w_in, gate_w2, gate_b):
    n, d = x.shape
    kd, vd, r = GLA_KEY_DIM, GLA_VAL_DIM, GLA_GATE_RANK
    width = 2 * kd + 2 * vd
    w = jnp.pad(w_in, ((0, 0), (0, LANES - 2 * r))).astype(BF16)
    w2 = jnp.zeros((LANES, 2 * kd), F32)
    w2 = w2.at[:r, :kd].set(gate_w2[0]).at[r:2 * r, kd:].set(gate_w2[1])
    w2 = jnp.stack(_split_bf16(w2))
    gb = gate_b.reshape(1, 2 * kd)
    row = lambda i: (i, 0)
    fix = lambda i: (0, 0)
    return pl.pallas_call(
        _gla_in_kernel,
        grid=(n // ROW_TILE,),
        in_specs=[pl.BlockSpec((ROW_TILE, d), row), pl.BlockSpec((d, width + LANES), fix),
                  pl.BlockSpec((2, LANES, 2 * kd), lambda i: (0, 0, 0)), pl.BlockSpec((1, 2 * kd), fix)],
        out_specs=[pl.BlockSpec((ROW_TILE, kd), row), pl.BlockSpec((ROW_TILE, kd), row),
                   pl.BlockSpec((ROW_TILE, vd), row), pl.BlockSpec((ROW_TILE, vd), row),
                   pl.BlockSpec((ROW_TILE, 2 * kd), row)],
        out_shape=[jax.ShapeDtypeStruct((n, kd), BF16), jax.ShapeDtypeStruct((n, kd), BF16),
                   jax.ShapeDtypeStruct((n, vd), BF16), jax.ShapeDtypeStruct((n, vd), BF16),
                   jax.ShapeDtypeStruct((n, 2 * kd), F32)],
        compiler_params=_cparams(("parallel",)),
        name="gla_in",
    )(x, w, w2, gb)


def _gla_direction(q_ref, k_ref, v_ref, la_ref, st_ref, o_ref, reverse):
    rows, cr = q_ref.shape[0], GLA_CHUNK
    nch = rows // cr
    shift = cr.bit_length() - 1
    ri = lax.broadcasted_iota(I32, (rows, rows), 0)
    ci = lax.broadcasted_iota(I32, (rows, rows), 1)
    same_chunk = jnp.right_shift(ri, shift) == jnp.right_shift(ci, shift)
    within = (ci >= ri) if reverse else (ci <= ri)
    cum_mat = jnp.where(same_chunk & within, 1.0, 0.0).astype(BF16)
    la_all = la_ref[...]
    la_hi = la_all.astype(BF16)
    rest = la_all - la_hi.astype(F32)
    la_mid = rest.astype(BF16)
    la_lo = (rest - la_mid.astype(F32)).astype(BF16)
    c_heads = (jnp.dot(cum_mat, la_hi, preferred_element_type=F32)
               + jnp.dot(cum_mat, la_mid, preferred_element_type=F32)
               + jnp.dot(cum_mat, la_lo, preferred_element_type=F32))
    rj = lax.broadcasted_iota(I32, (cr, cr), 0)
    cj = lax.broadcasted_iota(I32, (cr, cr), 1)
    mask = (cj > rj) if reverse else (cj <= rj)

    for head in range(st_ref.shape[0]):
        kcols = slice(head * GLA_HEAD_K, (head + 1) * GLA_HEAD_K)
        vcols = slice(head * GLA_HEAD_V, (head + 1) * GLA_HEAD_V)
        la, c_all = la_all[:, kcols], c_heads[:, kcols]
        qes, decs, intra, upd = [], [], [], []
        for ch in range(nch):
            lo = ch * cr
            c = c_all[lo:lo + cr]
            qf = q_ref[lo:lo + cr, kcols].astype(F32)
            kf = k_ref[lo:lo + cr, kcols].astype(F32)
            v = v_ref[lo:lo + cr, vcols]
            ctot = c[0:1] if reverse else c[cr - 1:cr]
            qes.append((qf * jnp.exp2(c)).astype(BF16))
            decs.append(jnp.exp2(ctot))
            blocks = []
            for i in range(cr // GLA_SUB):
                blo, bhi = i * GLA_SUB, (i + 1) * GLA_SUB
                edge = bhi - 1 if reverse else blo
                ref = c[edge:edge + 1] - la[lo + edge:lo + edge + 1]
                qi = (qf[blo:bhi] * jnp.exp2(c[blo:bhi] - ref)).astype(BF16)
                ki = (kf * jnp.exp2(jnp.minimum(ref - c, GLA_EXP2_CLAMP))).astype(BF16)
                blocks.append(lax.dot_general(qi, ki, NT_DIMS, preferred_element_type=F32))
            scores = jnp.where(mask, jnp.concatenate(blocks, axis=0), 0.0).astype(BF16)
            intra.append(jnp.dot(scores, v, preferred_element_type=F32))
            kd = (kf * jnp.exp2(ctot - c)).astype(BF16)
            upd.append(lax.dot_general(v, kd, TN_DIMS, preferred_element_type=F32))

        st = st_ref[head]
        states = [None] * nch
        for ch in (reversed(range(nch)) if reverse else range(nch)):
            states[ch] = st
            st = st * decs[ch] + upd[ch]
        st_ref[head] = st
        for ch in range(nch):
            inter = lax.dot_general(qes[ch], states[ch].astype(BF16), NT_DIMS, preferred_element_type=F32)
            o_ref[ch * cr:(ch + 1) * cr, vcols] = (intra[ch] + inter).astype(BF16)


def _gla_scan_kernel(qf_ref, kf_ref, vf_ref, laf_ref, qb_ref, kb_ref, vb_ref, lab_ref,
                     of_ref, ob_ref, sf_ref, sb_ref):
    @pl.when(pl.program_id(2) == 0)
    def _():
        sf_ref[...] = jnp.zeros_like(sf_ref)
        sb_ref[...] = jnp.zeros_like(sb_ref)

    _gla_direction(qf_ref, kf_ref, vf_ref, laf_ref, sf_ref, of_ref, False)
    _gla_direction(qb_ref, kb_ref, vb_ref, lab_ref, sb_ref, ob_ref, True)


def _gla_scan(q, k, v, la, batch):
    n = q.shape[0]
    nblk = n // batch // GLA_ROWS
    hps = GLA_HEADS_PER_STEP
    hk, hv, ngrp = hps * GLA_HEAD_K, hps * GLA_HEAD_V, GLA_HEADS // hps
    fwd = lambda b, h, i: (b * nblk + i, h)
    bwd = lambda b, h, i: (b * nblk + nblk - 1 - i, h)
    bwd_la = lambda b, h, i: (b * nblk + nblk - 1 - i, ngrp + h)
    state = pltpu.VMEM((hps, GLA_HEAD_V, GLA_HEAD_K), F32)
    return pl.pallas_call(
        _gla_scan_kernel,
        grid=(batch, ngrp, nblk),
        in_specs=[pl.BlockSpec((GLA_ROWS, hk), fwd), pl.BlockSpec((GLA_ROWS, hk), fwd),
                  pl.BlockSpec((GLA_ROWS, hv), fwd), pl.BlockSpec((GLA_ROWS, hk), fwd),
                  pl.BlockSpec((GLA_ROWS, hk), bwd), pl.BlockSpec((GLA_ROWS, hk), bwd),
                  pl.BlockSpec((GLA_ROWS, hv), bwd), pl.BlockSpec((GLA_ROWS, hk), bwd_la)],
        out_specs=[pl.BlockSpec((GLA_ROWS, hv), fwd), pl.BlockSpec((GLA_ROWS, hv), bwd)],
        out_shape=[jax.ShapeDtypeStruct((n, GLA_VAL_DIM), BF16)] * 2,
        scratch_shapes=[state, state],
        compiler_params=_cparams(("parallel", "parallel", "arbitrary")),
        name="gla_scan",
    )(q, k, v, la, q, k, v, la)


def _gla_out_kernel(of_ref, ob_ref, gate_ref, ng_ref, w_ref, x_ref, g_ref, b_ref, o_ref):
    o = of_ref[...].astype(F32) + ob_ref[...].astype(F32)
    parts = []
    for h in range(GLA_HEADS):
        oh = o[:, h * GLA_HEAD_V:(h + 1) * GLA_HEAD_V]
        ms = jnp.mean(oh * oh, axis=-1, keepdims=True)
        parts.append(oh * lax.rsqrt(ms + GLA_NORM_EPS))
    on = jnp.concatenate(parts, axis=-1) * ng_ref[...]
    a = (on * gate_ref[...].astype(F32)).astype(BF16)
    h = jnp.dot(a, w_ref[...], preferred_element_type=F32)
    o_ref[...] = _ln_rows(ALPHA * x_ref[...] + h, g_ref[...], b_ref[...])


def _gla_mixer_ln(x, batch, w_in, gate_w2, gate_b, norm_g, w_out, ln_g, ln_b):
    n, d = x.shape
    q, k, v, gate, la = _gla_in(x, w_in, gate_w2, gate_b)
    o_f, o_b = _gla_scan(q, k, v, la, batch)
    row = lambda i: (i, 0)
    fix = lambda i: (0, 0)
    vd = GLA_VAL_DIM
    return pl.pallas_call(
        _gla_out_kernel,
        grid=(n // ROW_TILE,),
        in_specs=[pl.BlockSpec((ROW_TILE, vd), row), pl.BlockSpec((ROW_TILE, vd), row),
                  pl.BlockSpec((ROW_TILE, vd), row), pl.BlockSpec((1, vd), fix),
                  pl.BlockSpec((vd, d), fix), pl.BlockSpec((ROW_TILE, d), row),
                  pl.BlockSpec((1, d), fix), pl.BlockSpec((1, d), fix)],
        out_specs=pl.BlockSpec((ROW_TILE, d), row),
        out_shape=jax.ShapeDtypeStruct((n, d), F32),
        compiler_params=_cparams(("parallel",)),
        name="gla_out",
    )(o_f, o_b, gate, _row2(jnp.tile(norm_g, GLA_HEADS)), w_out.astype(BF16), x,
      _row2(ln_g), _row2(ln_b))


def _hy_in_kernel(xe_ref, xo_ref, w_ref, b_ref, cw_ref, cb_ref, ue_ref, uo_ref):
    w, b = w_ref[...], b_ref[...]
    pe = jnp.dot(xe_ref[...], w, preferred_element_type=F32) + b
    po = jnp.dot(xo_ref[...], w, preferred_element_type=F32) + b
    rows = pe.shape[0]
    ri = lax.broadcasted_iota(I32, (rows, 1), 0)
    po_prev = jnp.where(ri == 0, 0.0, pltpu.roll(po, 1, axis=0))
    pe_next = jnp.where(ri == rows - 1, 0.0, pltpu.roll(pe, rows - 1, axis=0))
    cw, cb = cw_ref[...], cb_ref[...]
    ue_ref[...] = (po_prev * cw[0:1] + pe * cw[1:2] + po * cw[2:3] + cb).astype(BF16)
    uo_ref[...] = (pe * cw[0:1] + po * cw[1:2] + pe_next * cw[2:3] + cb).astype(BF16)


def _hy_in(xe, xo, batch, w_in, b_in, conv_w, conv_b):
    n2, d = xe.shape
    half = n2 // batch
    width = w_in.shape[1]
    cols = 256
    seq_spec = pl.BlockSpec((half, d), lambda b, j: (b, 0))
    col_spec = lambda r: pl.BlockSpec((r, cols), lambda b, j: (0, j))
    out_spec = pl.BlockSpec((half, cols), lambda b, j: (b, j))
    return pl.pallas_call(
        _hy_in_kernel,
        grid=(batch, width // cols),
        in_specs=[seq_spec, seq_spec, pl.BlockSpec((d, cols), lambda b, j: (0, j)), col_spec(1), col_spec(3),
                  col_spec(1)],
        out_specs=[out_spec, out_spec],
        out_shape=[jax.ShapeDtypeStruct((n2, width), BF16)] * 2,
        compiler_params=_cparams(("parallel", "arbitrary")),
        name="hy_in",
    )(xe, xo, w_in.astype(BF16), _row2(b_in), conv_w, _row2(conv_b))


def _hy_filter_kernel(pe_ref, w1_ref, b1_ref, f_ref, w2_ref, b2_ref, w3_ref, win_ref, hs_ref, hd_ref):
    f = f_ref[...]
    h = jnp.sin(f[0:1] * (jnp.dot(pe_ref[...], w1_ref[...], precision=HIGHEST,
                                  preferred_element_type=F32) + b1_ref[...]))
    h = jnp.sin(f[1:2] * (jnp.dot(h, w2_ref[...], precision=HIGHEST,
                                  preferred_element_type=F32) + b2_ref[...]))
    h = jnp.dot(h, w3_ref[...], precision=HIGHEST, preferred_element_type=F32)
    win = win_ref[...]
    d = win.shape[1]
    rows = win.shape[0]
    t = pl.program_id(0) * rows + lax.broadcasted_iota(I32, (rows, 1), 0)
    for o in range(HYENA_ORDER):
        hf = h[:, 2 * o * d:(2 * o + 1) * d] * win
        hb = jnp.where(t == 0, 0.0, h[:, (2 * o + 1) * d:(2 * o + 2) * d] * win)
        hs_ref[:, o * d:(o + 1) * d] = (hf + hb).astype(BF16)
        hd_ref[:, o * d:(o + 1) * d] = (hf - hb).astype(BF16)


def _hy_spec_kernel(c_ref, s_ref, cmid_ref, smid_ref, hs_ref, hd_ref, har_ref, hai_ref, hbr_ref, hbi_ref):
    c, s = c_ref[...], s_ref[...]
    hs, hd = hs_ref[...], hd_ref[...]
    n = lax.broadcasted_iota(I32, (hs.shape[0], 1), 0)
    sgn = jnp.where(jnp.bitwise_and(n, 1) == 0, 1.0, -1.0).astype(BF16)
    ri = lax.broadcasted_iota(I32, (c.shape[0], 1), 0) + pl.program_id(0) * c.shape[0]
    mid_r = jnp.dot(cmid_ref[...], hs, preferred_element_type=F32)[0:1]
    mid_i = jnp.dot(smid_ref[...], hd, preferred_element_type=F32)[0:1]
    har_ref[...] = jnp.dot(c, hs, preferred_element_type=F32)
    hai_ref[...] = jnp.where(ri == 0, mid_r, jnp.dot(s, hd, preferred_element_type=F32))
    hbr_ref[...] = jnp.dot(c, hs * sgn, preferred_element_type=F32)
    hbi_ref[...] = jnp.where(ri == 0, mid_i, -jnp.dot(s, hd * sgn, preferred_element_type=F32))


def _dft_consts(seq):
    k = jnp.arange(seq, dtype=I32)
    step = LANES
    assert seq % step == 0
    phase = lambda t: ((k[:, None] * t[None, :]) % (2 * seq)).astype(F32) * (math.pi / seq)
    ang_hi = phase(jnp.arange(seq // step, dtype=I32) * step)[:, :, None]
    ang_lo = phase(jnp.arange(step, dtype=I32))[:, None, :]
    cos = (jnp.cos(ang_hi) * jnp.cos(ang_lo) - jnp.sin(ang_hi) * jnp.sin(ang_lo)).reshape(seq, seq)
    sin = (jnp.sin(ang_hi) * jnp.cos(ang_lo) + jnp.cos(ang_hi) * jnp.sin(ang_lo)).reshape(seq, seq)
    sgn = jnp.where(k % 2 == 0, 1.0, -1.0).astype(F32)
    fwd_i = jnp.where(k[:, None] == 0, sgn[None, :], -sin)
    return cos.astype(BF16), fwd_i.astype(BF16)


def _hy_filters(seq, consts, w1, b1, sin_freq, w2, b2, w3):
    cos, fwd_i = consts[0], consts[1]
    d = w3.shape[-1] // (2 * HYENA_ORDER)
    hid = HYENA_FILTER_HIDDEN
    t = jnp.arange(seq, dtype=F32)
    t_norm = t / max(seq - 1, 1)
    bands = (HYENA_POS_EMB - 1) // 2
    freqs = jnp.linspace(1e-4, bands - 1, bands, dtype=F32)
    ang = (2.0 * math.pi / seq) * t[:, None] * freqs[None, :]
    pe = jnp.concatenate([t_norm[:, None], jnp.cos(ang), -jnp.sin(ang)], axis=-1)
    pe = jnp.pad(pe, ((0, 0), (0, LANES - HYENA_POS_EMB)))
    min_decay = math.log(HYENA_DECAY_TARGET) / HYENA_SLOW_DECAY
    max_decay = math.log(HYENA_DECAY_TARGET) / HYENA_FAST_DECAY
    deltas = jnp.abs(jnp.linspace(min_decay, max_decay, d, dtype=F32))
    window = jnp.exp(-t_norm[:, None] * deltas[None, :]) + HYENA_WINDOW_SHIFT
    padc = LANES - hid
    w1p = jnp.pad(w1, ((0, LANES - HYENA_POS_EMB), (0, padc)))
    w2p = jnp.pad(w2, ((0, padc), (0, padc)))
    w3p = jnp.pad(w3, ((0, padc), (0, 0)))
    rows = 256
    fix = lambda i: (0, 0)
    row = lambda i: (i, 0)
    hs, hd = pl.pallas_call(
        _hy_filter_kernel,
        grid=(seq // rows,),
        in_specs=[pl.BlockSpec((rows, LANES), row), pl.BlockSpec((LANES, LANES), fix),
                  pl.BlockSpec((1, LANES), fix), pl.BlockSpec((2, LANES), fix),
                  pl.BlockSpec((LANES, LANES), fix), pl.BlockSpec((1, LANES), fix),
                  pl.BlockSpec((LANES, 2 * HYENA_ORDER * d), fix), pl.BlockSpec((rows, d), row)],
        out_specs=[pl.BlockSpec((rows, HYENA_ORDER * d), row)] * 2,
        out_shape=[jax.ShapeDtypeStruct((seq, HYENA_ORDER * d), BF16)] * 2,
        compiler_params=_cparams(("parallel",)),
        name="hy_filter",
    )(pe, w1p, _row2(jnp.pad(b1, (0, padc))), jnp.pad(sin_freq, ((0, 0), (0, padc))), w2p,
      _row2(jnp.pad(b2, (0, padc))), w3p, window)
    tf, tn = DFT_TILE, DFT_COLS
    half = seq // 2
    tn = DFT_SUBCOLS
    plane = jax.ShapeDtypeStruct((half, HYENA_ORDER * d), F32)
    mat = pl.BlockSpec((tf, seq), lambda i, j: (i, 0))
    mid = pl.BlockSpec((BF16_SUBLANES, seq), lambda i, j: (half // BF16_SUBLANES, 0))
    col = pl.BlockSpec((seq, tn), lambda i, j: (0, j))
    return pl.pallas_call(
        _hy_spec_kernel,
        grid=(half // tf, HYENA_ORDER * d // tn),
        in_specs=[mat, mat, mid, mid, col, col],
        out_specs=[pl.BlockSpec((tf, tn), lambda i, j: (i, j))] * 4,
        out_shape=[plane] * 4,
        compiler_params=_cparams(("parallel", "arbitrary")),
        name="hy_spec",
    )(cos, fwd_i, cos, fwd_i, hs, hd)


def _dft_fwd_kernel(c_ref, s_ref, e_ref, o_ref, tw_ref, har_ref, hai_ref, hbr_ref, hbi_ref,
                    v0r_ref, v0i_ref, v1r_ref, v1i_ref):
    cm, sm = c_ref[...], s_ref[...]
    tw = tw_ref[...]
    c, s = tw[:, 0:1], tw[:, 1:2]
    rows = cm.shape[0]
    row0 = (lax.broadcasted_iota(I32, (rows, 1), 0) + pl.program_id(0) * rows) == 0
    u = jnp.where(row0, 1.0, 2.0)
    subs = [slice(lo, lo + DFT_SUBCOLS) for lo in range(0, e_ref.shape[2], DFT_SUBCOLS)]
    prods = []
    for cs in subs:
        e, o = e_ref[0, :, cs], o_ref[0, :, cs]
        prods.append((jnp.dot(cm, e, preferred_element_type=F32), jnp.dot(sm, e, preferred_element_type=F32),
                      jnp.dot(cm, o, preferred_element_type=F32), jnp.dot(sm, o, preferred_element_type=F32)))
    for cs, (er, ei, orr, oi) in zip(subs, prods):
        tr = c * orr + s * oi
        ti = c * oi - s * orr
        ar, br = er + tr, er - tr
        ai = jnp.where(row0, ei, ei + ti)
        bi = jnp.where(row0, -oi, ti - ei)
        har, hai, hbr, hbi = har_ref[:, cs], hai_ref[:, cs], hbr_ref[:, cs], hbi_ref[:, cs]
        yar = jnp.where(row0, ar * har, ar * har - ai * hai)
        ybr = jnp.where(row0, br * hbr, br * hbr - bi * hbi)
        yai = jnp.where(row0, ai * hai - bi * hbi, ar * hai + ai * har)
        ybi = jnp.where(row0, ai * hbi + bi * hai, br * hbi + bi * hbr)
        dr, di = yar - ybr, yai + ybi
        v0r_ref[0, :, cs] = (u * (yar + ybr)).astype(BF16)
        v0i_ref[0, :, cs] = jnp.where(row0, 2.0 * yai, u * (yai - ybi)).astype(BF16)
        v1r_ref[0, :, cs] = (u * (c * dr - s * di)).astype(BF16)
        v1i_ref[0, :, cs] = jnp.where(row0, -2.0 * ybi, u * (c * di + s * dr)).astype(BF16)


def _dft_inv_kernel(gr_ref, gi_ref, v0r_ref, v0i_ref, v1r_ref, v1i_ref, ze_ref, zo_ref, ge_ref, go_ref,
                    fb_ref, ye_ref, yo_ref):
    gr, gi = gr_ref[...], gi_ref[...]
    fb = fb_ref[...]
    ye = jnp.dot(gr, v0r_ref[0], preferred_element_type=F32) + jnp.dot(gi, v0i_ref[0], preferred_element_type=F32)
    ye_ref[0] = (ge_ref[0].astype(F32) * (ye + ze_ref[0].astype(F32) * fb)).astype(BF16)
    yo = jnp.dot(gr, v1r_ref[0], preferred_element_type=F32) + jnp.dot(gi, v1i_ref[0], preferred_element_type=F32)
    yo_ref[0] = (go_ref[0].astype(F32) * (yo + zo_ref[0].astype(F32) * fb)).astype(BF16)


def _long_conv_gate(consts, planes, order, z_e, z_o, gate_e, gate_o, filt_bias):
    cos, fwd_i, inv_r, inv_i, twiddle = consts
    (ze_arr, ze_col), (zo_arr, zo_col) = z_e, z_o
    (ge_arr, ge_col), (go_arr, go_col) = gate_e, gate_o
    batch, half, _ = ze_arr.shape
    d = D_MODEL
    tf, tn = DFT_TILE, DFT_COLS
    nj = d // tn
    plane_spec = pl.BlockSpec((tf, tn), lambda i, j, b: (i, order * nj + j))
    mat_f = pl.BlockSpec((tf, half), lambda i, j, b: (i, 0))
    cols_f = lambda off: pl.BlockSpec((1, half, tn), lambda i, j, b: (b, 0, off + j))
    tile_f = pl.BlockSpec((1, tf, tn), lambda i, j, b: (b, i, j))
    v_shape = jax.ShapeDtypeStruct((batch, half, d), BF16)
    v = pl.pallas_call(
        _dft_fwd_kernel,
        grid=(half // tf, nj, batch),
        in_specs=[mat_f, mat_f, cols_f(ze_col), cols_f(zo_col), pl.BlockSpec((tf, LANES), lambda i, j, b: (i, 0)),
                  plane_spec, plane_spec, plane_spec, plane_spec],
        out_specs=[tile_f] * 4,
        out_shape=[v_shape] * 4,
        compiler_params=_cparams(("parallel", "parallel", "arbitrary")),
        name="dft_fwd",
    )(cos, fwd_i, ze_arr, zo_arr, twiddle, *planes)
    mat_i = pl.BlockSpec((tf, half), lambda b, j, i: (i, 0))
    cols_i = pl.BlockSpec((1, half, tn), lambda b, j, i: (b, 0, j))
    tile_i = lambda off: pl.BlockSpec((1, tf, tn), lambda b, j, i: (b, i, off + j))
    return pl.pallas_call(
        _dft_inv_kernel,
        grid=(batch, nj, half // tf),
        in_specs=[mat_i, mat_i, cols_i, cols_i, cols_i, cols_i, tile_i(ze_col), tile_i(zo_col),
                  tile_i(ge_col), tile_i(go_col), pl.BlockSpec((1, tn), lambda b, j, i: (0, j))],
        out_specs=[tile_i(0), tile_i(0)],
        out_shape=[v_shape, v_shape],
        compiler_params=_cparams(("parallel", "parallel", "arbitrary")),
        name="dft_inv",
    )(inv_r, inv_i, *v, ze_arr, zo_arr, ge_arr, go_arr, _row2(filt_bias))


def _proj_ln_pair_kernel(ae_ref, ao_ref, w_ref, bias_ref, x_ref, g_ref, b_ref, o_ref, mix_ref):
    w, bias = w_ref[...], bias_ref[...]
    half = ae_ref.shape[0]
    for parity, a_ref in ((0, ae_ref), (1, ao_ref)):
        h = jnp.dot(a_ref[...], w, preferred_element_type=F32) + bias
        for c in range(h.shape[1] // LANES):
            mix_ref[c, pl.ds(parity, half, stride=2), :] = h[:, c * LANES:(c + 1) * LANES]
    h = jnp.concatenate([mix_ref[c] for c in range(mix_ref.shape[0])], axis=1)
    o_ref[...] = _ln_rows(ALPHA * x_ref[...] + h, g_ref[...], b_ref[...])


def _proj_ln_pair(a_e, a_o, w, bias, x, g, b):
    n, d = x.shape
    rows = ROW_TILE // 2
    row = lambda i: (i, 0)
    fix = lambda i: (0, 0)
    return pl.pallas_call(
        _proj_ln_pair_kernel,
        grid=(n // ROW_TILE,),
        in_specs=[pl.BlockSpec((rows, d), row), pl.BlockSpec((rows, d), row), pl.BlockSpec(w.shape, fix),
                  pl.BlockSpec((1, d), fix), pl.BlockSpec((ROW_TILE, d), row),
                  pl.BlockSpec((1, d), fix), pl.BlockSpec((1, d), fix)],
        out_specs=pl.BlockSpec((ROW_TILE, d), row),
        out_shape=jax.ShapeDtypeStruct((n, d), F32),
        scratch_shapes=[pltpu.VMEM((d // LANES, ROW_TILE, LANES), F32)],
        compiler_params=_cparams(("parallel",)),
        name="proj_ln_pair",
    )(a_e, a_o, w, _row2(bias), x, _row2(g), _row2(b))


def _hyena_mixer_ln(x, batch, w_in, b_in, conv_w, conv_b, ffn_w1, ffn_b1, sin_freq, ffn_w2, ffn_b2,
                    ffn_w3, filt_bias, w_out, b_out, ln_g, ln_b):
    n, d = x.shape
    seq = n // batch
    half = seq // 2
    nj = d // DFT_COLS
    planes = _hy_filters(seq, _dft_consts(seq), ffn_w1, ffn_b1, sin_freq, ffn_w2, ffn_b2, ffn_w3)
    cos, fwd_i = _dft_consts(half)
    scale = 0.5 / seq
    k = jnp.arange(half, dtype=F32)[:, None] * (math.pi / seq)
    twiddle = jnp.pad(jnp.concatenate([jnp.cos(k), jnp.sin(k)], axis=1), ((0, 0), (0, LANES - 2)))
    consts = (cos, fwd_i, cos * scale, fwd_i.T * scale, twiddle)
    x3 = x.reshape(batch, seq, d)
    u_e, u_o = _hy_in(x3[:, 0::2].astype(BF16).reshape(n // 2, d), x3[:, 1::2].astype(BF16).reshape(n // 2, d),
                      batch, w_in, b_in, conv_w, conv_b)
    u_e, u_o = u_e.reshape(batch, half, 3 * d), u_o.reshape(batch, half, 3 * d)
    y_e, y_o = _long_conv_gate(consts, planes, 0, (u_e, 0), (u_o, 0), (u_e, nj), (u_o, nj), filt_bias[0])
    y_e, y_o = _long_conv_gate(consts, planes, 1, (y_e, 0), (y_o, 0), (u_e, 2 * nj), (u_o, 2 * nj), filt_bias[1])
    return _proj_ln_pair(y_e.reshape(n // 2, d), y_o.reshape(n // 2, d), w_out.astype(BF16), b_out, x,
                         ln_g, ln_b)


def _na_in_kernel(x_ref, w_ref, b_ref, q_ref, k_ref, v_ref):
    d = D_MODEL
    p = jnp.dot(x_ref[...].astype(BF16), w_ref[...], preferred_element_type=F32) + b_ref[...]
    q_ref[...] = (p[:, :d] * (NA_HEAD_DIM ** -0.5 * LOG2E)).astype(BF16)
    k_ref[...] = p[:, d:2 * d].astype(BF16)
    v_ref[...] = p[:, 2 * d:].astype(BF16)


def _na_attn_kernel(q_ref, k_ref, vt_ref, bias_ref, o_ref):
    for i in range(NA_PAIRS):
        rp = pl.program_id(2) * NA_PAIRS + i
        o_ref[i * LANES:(i + 1) * LANES] = _na_pair(rp, q_ref[i * LANES:(i + 1) * LANES], k_ref, vt_ref, bias_ref)


def _na_pair(rp, q, k_ref, vt_ref, bias_ref):
    hpg = LANES // NA_HEAD_DIM
    n_rows = GRID_W
    r0 = 2 * rp
    rs0 = jnp.clip(r0 - NA_WIN_ROWS // 2, 0, n_rows - NA_WIN_ROWS)
    rs1 = jnp.clip(r0 + 1 - NA_WIN_ROWS // 2, 0, n_rows - NA_WIN_ROWS)
    start = jnp.minimum(rs0, n_rows - NA_KEY_ROWS)
    tok0 = pl.multiple_of(start * GRID_W, LANES)
    nkeys = NA_KEY_ROWS * GRID_W
    kw = k_ref[pl.ds(tok0, nkeys), :]
    vtw = vt_ref[0, :, pl.ds(tok0, nkeys)]
    lane = lax.broadcasted_iota(I32, (1, LANES), 1)
    sub = lax.broadcasted_iota(I32, (LANES, 1), 0)
    right = lane >= GRID_W
    acc = jnp.zeros((LANES, LANES), F32)
    for h in range(hpg):
        in_head = (lane >= h * NA_HEAD_DIM) & (lane < (h + 1) * NA_HEAD_DIM)
        qh = jnp.where(in_head, q, jnp.zeros_like(q))
        s = lax.dot_general(kw, qh, NT_DIMS, preferred_element_type=F32)
        tiles = []
        for a in range(NA_KEY_ROWS):
            kr = start + a
            dr0 = kr - r0 + (NA_WIN_ROWS - 1)
            bt = bias_ref[h, jnp.clip(dr0, 0, 2 * NA_WIN_ROWS - 1)]
            if a < 2 or a >= NA_WIN_ROWS:
                pen0 = jnp.where((kr >= rs0) & (kr < rs0 + NA_WIN_ROWS), 0.0, NEG)
                pen1 = jnp.where((kr >= rs1) & (kr < rs1 + NA_WIN_ROWS), 0.0, NEG)
                bt = bt + jnp.where(right, pen1, pen0)
            tiles.append(bt)
        s = s + jnp.concatenate(tiles, axis=0)
        m = jnp.max(s, axis=0, keepdims=True)
        p = jnp.exp2(s - m)
        inv = 1.0 / jnp.sum(p, axis=0, keepdims=True)
        in_head_rows = (sub >= h * NA_HEAD_DIM) & (sub < (h + 1) * NA_HEAD_DIM)
        pv = jnp.dot(vtw, p.astype(BF16), preferred_element_type=F32)
        acc = acc + jnp.where(in_head_rows, pv * inv, 0.0)
    return acc.T.astype(BF16)


def _na_bias_table(rpb):
    nh = rpb.shape[0]
    kc = jnp.arange(GRID_W)[:, None]
    qc = jnp.arange(GRID_W)[None, :]
    cs = jnp.clip(qc - NA_WIN_COLS // 2, 0, GRID_W - NA_WIN_COLS)
    inwin = (kc >= cs) & (kc < cs + NA_WIN_COLS)
    dc = jnp.clip(kc - qc + (NA_WIN_COLS - 1), 0, 2 * NA_WIN_COLS - 2)
    colb = jnp.where(inwin[None, None], rpb.astype(F32)[:, :, dc] * LOG2E, NEG)
    pad = jnp.full((nh, 1, GRID_W, GRID_W), NEG, F32)
    left = jnp.concatenate([colb, pad], axis=1)
    rgt = jnp.concatenate([pad, colb], axis=1)
    return jnp.concatenate([left, rgt], axis=-1)


def _na_mixer_ln(x, batch, w_in, b_in, rpb, w_out, b_out, ln_g, ln_b):
    n, d = x.shape
    seq = n // batch
    assert seq == GRID_W * GRID_W
    row = lambda i: (i, 0)
    fix = lambda i: (0, 0)
    qkv_shape = jax.ShapeDtypeStruct((n, d), BF16)
    q, k, v = pl.pallas_call(
        _na_in_kernel,
        grid=(n // ROW_TILE,),
        in_specs=[pl.BlockSpec((ROW_TILE, d), row), pl.BlockSpec((d, 3 * d), fix), pl.BlockSpec((1, 3 * d), fix)],
        out_specs=[pl.BlockSpec((ROW_TILE, d), row)] * 3,
        out_shape=[qkv_shape] * 3,
        compiler_params=_cparams(("parallel",)),
        name="na_in",
    )(x, w_in.astype(BF16), _row2(b_in))
    vt = jnp.swapaxes(v.reshape(batch, seq, d), 1, 2)
    bias = _na_bias_table(rpb)
    hpg = LANES // NA_HEAD_DIM
    n_pairs = GRID_W // 2
    n_steps = n_pairs // NA_PAIRS
    o = pl.pallas_call(
        _na_attn_kernel,
        grid=(batch, d // LANES, n_steps),
        in_specs=[pl.BlockSpec((NA_PAIRS * LANES, LANES), lambda b, g, r: (b * n_steps + r, g)),
                  pl.BlockSpec((seq, LANES), lambda b, g, r: (b, g)),
                  pl.BlockSpec((1, LANES, seq), lambda b, g, r: (b, g, 0)),
                  pl.BlockSpec((hpg, 2 * NA_WIN_ROWS, GRID_W, LANES), lambda b, g, r: (g, 0, 0, 0))],
        out_specs=pl.BlockSpec((NA_PAIRS * LANES, LANES), lambda b, g, r: (b * n_steps + r, g)),
        out_shape=qkv_shape,
        compiler_params=_cparams(("parallel", "parallel", "arbitrary")),
        name="na_attn",
    )(q, k, vt, bias)
    return _proj_ln(o, w_out.astype(BF16), b_out, x, ln_g, ln_b)


def _moe_route_kernel(x_ref, wr_ref, rb_ref, xs_ref, pos_ref, gate_ref, cnt_ref):
    ne, tb = N_EXPERTS, MOE_TB
    xb, x_lo = _split_bf16(x_ref[...])
    nt = lambda p, q: lax.dot_general(p, q, NT_DIMS, preferred_element_type=F32)
    logits = nt(wr_ref[0], xb) + nt(wr_ref[1], xb) + nt(wr_ref[0], x_lo) + rb_ref[...]
    eio = lax.broadcasted_iota(I32, (ne, tb), 0)
    work = logits
    vals, hots = [], []
    for _ in range(TOP_K):
        m = jnp.max(work, axis=0, keepdims=True)
        idx = jnp.min(jnp.where(work == m, eio, ne), axis=0, keepdims=True)
        hot = eio == idx
        vals.append(m)
        hots.append(hot)
        work = jnp.where(hot, -jnp.inf, work)
    exps = [jnp.exp(v - vals[0]) for v in vals]
    denom = exps[0] + exps[1] + exps[2] + exps[3]
    sel = jnp.zeros((ne, tb), F32)
    for hot in hots:
        sel = sel + jnp.where(hot, 1.0, 0.0)
    earlier = (lax.broadcasted_iota(I32, (tb, tb), 0) < lax.broadcasted_iota(I32, (tb, tb), 1))
    rank = jnp.dot(sel.astype(BF16), jnp.where(earlier, 1.0, 0.0).astype(BF16),
                   preferred_element_type=F32)
    cnt = jnp.sum(sel, axis=1, keepdims=True)
    cnt_pad = jnp.ceil(cnt * (1.0 / MOE_G)) * MOE_G
    below = (lax.broadcasted_iota(I32, (ne, ne), 1) < lax.broadcasted_iota(I32, (ne, ne), 0))
    seg_off = jnp.dot(jnp.where(below, 1.0, 0.0), jnp.broadcast_to(cnt_pad, (ne, tb)),
                      precision=HIGHEST, preferred_element_type=F32)
    slot = seg_off + rank
    pos = [jnp.sum(jnp.where(hot, slot, 0.0), axis=0, keepdims=True).astype(I32) for hot in hots]
    zero_row = jnp.zeros((1, tb), I32)
    pos_ref[0] = jnp.concatenate(pos + [zero_row] * (8 - TOP_K), axis=0)
    gate_ref[0] = jnp.concatenate([e / denom for e in exps] + [zero_row.astype(F32)] * (8 - TOP_K), axis=0)
    cnt_ref[0] = jnp.broadcast_to(cnt_pad, (ne, LANES)).astype(I32)
    local = lax.broadcasted_iota(I32, (MOE_PCH, tb), 0).astype(F32).astype(BF16)
    one, zero = jnp.ones((), BF16), jnp.zeros((), BF16)
    for r in range(MOE_RB // MOE_PCH):
        onehot = None
        for p in pos:
            hit = jnp.where(local == (p - r * MOE_PCH).astype(F32).astype(BF16), one, zero)
            onehot = hit if onehot is None else onehot + hit
        xs_ref[r * MOE_PCH:(r + 1) * MOE_PCH] = jnp.dot(
            onehot, xb, preferred_element_type=F32).astype(BF16)


def _moe_expert_kernel(te_ref, src_ref, nt_ref, xs_hbm, wgu_ref, bgu_ref, wd_ref, bd_ref,
                       ys_hbm, xbuf, ybuf, wgu_bf, wd_bf, sem_in, sem_out, *, n_blocks):
    t = pl.program_id(0)
    nt = nt_ref[0]
    s = t % 2

    def in_copy(chunk, slot, c):
        return pltpu.make_async_copy(
            xs_hbm.at[pl.ds(pl.multiple_of(chunk * MOE_G, MOE_G), MOE_G)],
            xbuf.at[slot, pl.ds(c * MOE_G, MOE_G)], sem_in.at[slot])

    def out_copy(chunk, slot, c):
        return pltpu.make_async_copy(
            ybuf.at[slot, pl.ds(c * MOE_G, MOE_G)],
            ys_hbm.at[pl.ds(pl.multiple_of(chunk * MOE_G, MOE_G), MOE_G)], sem_out.at[slot])

    def tile_chunks(tile):
        return [src_ref[tile * MOE_CPT + c] for c in range(MOE_CPT)]

    def start_in(tile, slot):
        for c, chunk in enumerate(tile_chunks(tile)):
            in_copy(chunk, slot, c).start()

    def start_out(tile, slot):
        chunks = tile_chunks(tile)
        for c, chunk in enumerate(chunks):
            spare = ((slot * MOE_CPT + c) % n_blocks) * MOE_BLOCK_CHUNKS + MOE_SPARE_CHUNK
            out_copy(jnp.where(chunk == MOE_ZERO_CHUNK, spare, chunk), slot, c).start()

    def wait_in(slot):
        for c in range(MOE_CPT):
            in_copy(0, slot, c).wait()

    def wait_out(slot):
        for c in range(MOE_CPT):
            out_copy(0, slot, c).wait()

    @pl.when(t == 0)
    def _():
        start_in(0, 0)

    @pl.when(t + 1 < nt)
    def _():
        start_in(t + 1, 1 - s)

    @pl.when(t < nt)
    def _():
        @pl.when(jnp.logical_or(t == 0, te_ref[t] != te_ref[jnp.maximum(t - 1, 0)]))
        def _():
            wgu_bf[...] = wgu_ref[0, 0].astype(BF16)
            wd_bf[...] = wd_ref[0, 0].astype(BF16)

        wait_in(s)
        f = wd_bf.shape[0]
        h = jnp.dot(xbuf[s], wgu_bf[...], preferred_element_type=F32) + bgu_ref[0, 0]
        glu = jnp.minimum(h[:, :f], SWIGLU_LIMIT)
        lin = jnp.clip(h[:, f:], -SWIGLU_LIMIT, SWIGLU_LIMIT)
        act = (glu * jax.nn.sigmoid(SWIGLU_ALPHA * glu) * (lin + 1.0)).astype(BF16)
        y = jnp.dot(act, wd_bf[...], preferred_element_type=F32) + bd_ref[0, 0]

        @pl.when(t >= 2)
        def _():
            wait_out(s)

        ybuf[s] = y.astype(BF16)
        start_out(t, s)

        @pl.when(t == nt - 1)
        def _():
            wait_out(s)

            @pl.when(t >= 1)
            def _():
                wait_out(1 - s)


def _moe_combine_kernel(ys_ref, pos_ref, gate_ref, x_ref, g_ref, b_ref, o_ref):
    tb = MOE_TB
    pos = pos_ref[...]
    gate = gate_ref[...].astype(BF16)
    local = lax.broadcasted_iota(I32, (tb, MOE_PCH), 1).astype(F32).astype(BF16)
    zero = jnp.zeros((), BF16)
    f = jnp.zeros((tb, D_MODEL), F32)
    for r in range(MOE_RB // MOE_PCH):
        w = None
        for k in range(TOP_K):
            pk = (pos[:, k:k + 1] - r * MOE_PCH).astype(F32).astype(BF16)
            wk = jnp.where(local == pk, gate[:, k:k + 1], zero)
            w = wk if w is None else w + wk
        f = f + jnp.dot(w, ys_ref[r * MOE_PCH:(r + 1) * MOE_PCH], preferred_element_type=F32)
    o_ref[...] = _ln_rows(ALPHA * x_ref[...] + f, g_ref[...], b_ref[...])


def _moe_tile_tables(cnt_pad, n_tiles_max):
    nb, ne = cnt_pad.shape
    nch_eb = (cnt_pad // MOE_G).T
    seg_off = jnp.cumsum(cnt_pad, axis=1) - cnt_pad
    base_eb = ((jnp.arange(nb, dtype=I32)[:, None] * MOE_RB + seg_off) // MOE_G).T
    first_eb = jnp.cumsum(nch_eb, axis=1) - nch_eb
    tiles_e = (jnp.sum(nch_eb, axis=1) + MOE_CPT - 1) // MOE_CPT
    tile_end = jnp.cumsum(tiles_e)
    n_tiles = tile_end[-1]
    tau = jnp.arange(n_tiles_max, dtype=I32)
    tile_e = jnp.minimum(jnp.sum(tau[:, None] >= tile_end[None, :], axis=1), ne - 1).astype(I32)
    of_tile = tile_e[:, None] == jnp.arange(ne, dtype=I32)[None, :]
    pick = lambda tab: jnp.sum(jnp.where(of_tile[:, :, None], tab[None], 0), axis=1)
    tile_first = tau - jnp.sum(jnp.where(of_tile, (tile_end - tiles_e)[None, :], 0), axis=1)
    k = (tile_first * MOE_CPT)[:, None] + jnp.arange(MOE_CPT, dtype=I32)[None, :]
    first, count, base = pick(first_eb), pick(nch_eb), pick(base_eb)
    inside = (k[:, :, None] >= first[:, None, :]) & (k[:, :, None] < (first + count)[:, None, :])
    src = jnp.sum(jnp.where(inside, (base - first)[:, None, :] + k[:, :, None], 0), axis=2)
    src = jnp.where(jnp.any(inside, axis=2) & (tau < n_tiles)[:, None], src, MOE_ZERO_CHUNK)
    return tile_e, src.reshape(-1).astype(I32), n_tiles.astype(I32).reshape(1)


def _moe_ln(x, layer, router_w, router_b, w_gu, b_gu, w_down, b_down, ln_g, ln_b):
    n, d = x.shape
    ne, f = N_EXPERTS, w_down.shape[2]
    depth = w_gu.shape[0]
    expert_block = lambda t, te, sr, nt: (layer, te[t], 0, 0)
    nb = n // MOE_TB
    assert MOE_RB >= MOE_TB * TOP_K + ne * (MOE_G - 1) + 2 * MOE_G
    blk = lambda i: (i, 0)
    blk3 = lambda i: (i, 0, 0)
    fix = lambda i: (0, 0)
    slot_rows = nb * MOE_RB
    xs, pos, gates, cnt = pl.pallas_call(
        _moe_route_kernel,
        grid=(nb,),
        in_specs=[pl.BlockSpec((MOE_TB, d), blk), pl.BlockSpec((2, ne, d), lambda i: (0, 0, 0)),
                  pl.BlockSpec((ne, 1), fix)],
        out_specs=[pl.BlockSpec((MOE_RB, d), blk), pl.BlockSpec((1, 8, MOE_TB), blk3),
                   pl.BlockSpec((1, 8, MOE_TB), blk3), pl.BlockSpec((1, ne, LANES), blk3)],
        out_shape=[jax.ShapeDtypeStruct((slot_rows, d), BF16), jax.ShapeDtypeStruct((nb, 8, MOE_TB), I32),
                   jax.ShapeDtypeStruct((nb, 8, MOE_TB), F32), jax.ShapeDtypeStruct((nb, ne, LANES), I32)],
        compiler_params=_cparams(("parallel",)),
        name="moe_route",
    )(x, jnp.stack(_split_bf16(router_w.T)), router_b.reshape(ne, 1))

    n_tiles_max = nb * MOE_RB // MOE_TM + ne
    tile_e, src, n_tiles = _moe_tile_tables(cnt[:, :, 0], n_tiles_max)
    ys = pl.pallas_call(
        functools.partial(_moe_expert_kernel, n_blocks=nb),
        grid_spec=pltpu.PrefetchScalarGridSpec(
            num_scalar_prefetch=3,
            grid=(n_tiles_max,),
            in_specs=[pl.BlockSpec(memory_space=pl.ANY),
                      pl.BlockSpec((1, 1, d, 2 * f), expert_block),
                      pl.BlockSpec((1, 1, 1, 2 * f), expert_block),
                      pl.BlockSpec((1, 1, f, d), expert_block),
                      pl.BlockSpec((1, 1, 1, d), expert_block)],
            out_specs=pl.BlockSpec(memory_space=pl.ANY),
            scratch_shapes=[pltpu.VMEM((2, MOE_TM, d), BF16), pltpu.VMEM((2, MOE_TM, d), BF16),
                            pltpu.VMEM((d, 2 * f), BF16), pltpu.VMEM((f, d), BF16),
                            pltpu.SemaphoreType.DMA((2,)), pltpu.SemaphoreType.DMA((2,))]),
        out_shape=jax.ShapeDtypeStruct((slot_rows, d), BF16),
        input_output_aliases={3: 0},
        compiler_params=pltpu.CompilerParams(dimension_semantics=("arbitrary",),
                                             vmem_limit_bytes=MOE_EXPERT_VMEM),
        name="moe_expert",
    )(tile_e, src, n_tiles, xs, w_gu, b_gu.reshape(depth, ne, 1, 2 * f), w_down,
      b_down.reshape(depth, ne, 1, d))

    pos_t = jnp.swapaxes(pos, 1, 2).reshape(n, 8)
    gate_t = jnp.swapaxes(gates, 1, 2).reshape(n, 8)
    return pl.pallas_call(
        _moe_combine_kernel,
        grid=(nb,),
        in_specs=[pl.BlockSpec((MOE_RB, d), blk), pl.BlockSpec((MOE_TB, 8), blk), pl.BlockSpec((MOE_TB, 8), blk),
                  pl.BlockSpec((MOE_TB, d), blk), pl.BlockSpec((1, d), fix), pl.BlockSpec((1, d), fix)],
        out_specs=pl.BlockSpec((MOE_TB, d), blk),
        out_shape=jax.ShapeDtypeStruct((n, d), F32),
        compiler_params=_cparams(("parallel",)),
        name="moe_combine",
    )(ys, pos_t, gate_t, x, _row2(ln_g), _row2(ln_b))


def kernel(x, ln1_g, ln1_b, ln2_g, ln2_b, gla_w_in, gla_gate_w2, gla_gate_b, gla_norm_g, gla_w_out,
           hy_w_in, hy_b_in, hy_conv_w, hy_conv_b, hy_ffn_w1, hy_ffn_b1, hy_sin_freq, hy_ffn_w2,
           hy_ffn_b2, hy_ffn_w3, hy_filt_bias, hy_w_out, hy_b_out, na_w_in, na_b_in, na_rpb, na_w_out,
           na_b_out, moe_router_w, moe_router_b, moe_w_gu, moe_b_gu, moe_w_down, moe_b_down):
    batch, seq, d = x.shape
    xt = x.reshape(batch * seq, d)
    for i in range(DEPTH):
        m, j = i % N_MIXERS, i // N_MIXERS
        if m == 0:
            xt = _gla_mixer_ln(xt, batch, gla_w_in[j], gla_gate_w2[j], gla_gate_b[j], gla_norm_g[j],
                               gla_w_out[j], ln1_g[i], ln1_b[i])
        elif m == 1:
            xt = _hyena_mixer_ln(xt, batch, hy_w_in[j], hy_b_in[j], hy_conv_w[j], hy_conv_b[j],
                                 hy_ffn_w1[j], hy_ffn_b1[j], hy_sin_freq[j], hy_ffn_w2[j], hy_ffn_b2[j],
                                 hy_ffn_w3[j], hy_filt_bias[j], hy_w_out[j], hy_b_out[j],
                                 ln1_g[i], ln1_b[i])
        else:
            xt = _na_mixer_ln(xt, batch, na_w_in[j], na_b_in[j], na_rpb[j], na_w_out[j], na_b_out[j],
                              ln1_g[i], ln1_b[i])
        xt = _moe_ln(xt, i, moe_router_w[i], moe_router_b[i], moe_w_gu, moe_b_gu, moe_w_down,
                     moe_b_down, ln2_g[i], ln2_b[i])
    return xt.reshape(batch, seq, d)
```

```python
import functools
import math

import jax
import jax.numpy as jnp
from jax import lax
from jax.experimental import pallas as pl
from jax.experimental.pallas import tpu as pltpu

F32, BF16, I32 = jnp.float32, jnp.bfloat16, jnp.int32
HIGHEST = lax.Precision.HIGHEST
NT_DIMS = (((1,), (1,)), ((), ()))
TN_DIMS = (((0,), (0,)), ((), ()))

D_MODEL = 1024
DEPTH = 4
N_MIXERS = 3
GRID_W = 64
ALPHA = (2 * DEPTH) ** 0.25
LN_EPS = 1e-5

GLA_HEADS = 4
GLA_KEY_DIM = 512
GLA_VAL_DIM = 1024
GLA_HEAD_K = 128
GLA_HEAD_V = 256
GLA_GATE_RANK = 16
GLA_TAU = 16.0
GLA_NORM_EPS = 1e-6

HYENA_ORDER = 2
HYENA_POS_EMB = 33
HYENA_FILTER_HIDDEN = 64
HYENA_DECAY_TARGET = 1e-2
HYENA_FAST_DECAY = 0.3
HYENA_SLOW_DECAY = 1.5
HYENA_WINDOW_SHIFT = 0.05

NA_HEAD_DIM = 32
NA_HEADS = 32
NA_WIN_ROWS = 8
NA_WIN_COLS = 16

N_EXPERTS = 32
TOP_K = 4
SWIGLU_ALPHA = 1.702
SWIGLU_LIMIT = 7.0

LANES = 128
BF16_SUBLANES = 16
VMEM_LIMIT = 48 * 1024 * 1024
MOE_EXPERT_VMEM = 56 * 1024 * 1024

ROW_TILE = 512
GLA_CHUNK = 64
GLA_SUB = 16
GLA_ROWS = 256
GLA_HEADS_PER_STEP = 4
GLA_EXP2_CLAMP = 115.0
DFT_TILE = 512
DFT_COLS = 512
DFT_SUBCOLS = 256
NA_KEY_ROWS = 10
NA_PAIRS = 16
NEG = -1e30
LOG2E = math.log2(math.e)

MOE_TB = 512
MOE_G = BF16_SUBLANES
MOE_RB = MOE_TB * TOP_K + 512
MOE_PCH = 256
MOE_BLOCK_CHUNKS = MOE_RB // MOE_G
MOE_ZERO_CHUNK = MOE_BLOCK_CHUNKS - 1
MOE_SPARE_CHUNK = MOE_BLOCK_CHUNKS - 2
MOE_TM = 512
MOE_CPT = MOE_TM // MOE_G


def _cparams(sem):
    return pltpu.CompilerParams(dimension_semantics=sem, vmem_limit_bytes=VMEM_LIMIT)


def _ln_rows(y, g, b):
    mu = jnp.mean(y, axis=-1, keepdims=True)
    yc = y - mu
    var = jnp.mean(yc * yc, axis=-1, keepdims=True)
    return yc * lax.rsqrt(var + LN_EPS) * g + b


def _row2(v):
    return v.reshape(1, -1)


def _split_bf16(a):
    hi = a.astype(BF16)
    return hi, (a - hi.astype(F32)).astype(BF16)


def _dot_split(a, b_hi, b_lo, dims):
    a_hi, a_lo = _split_bf16(a)
    if dims is None:
        mm = lambda p, q: jnp.dot(p, q, preferred_element_type=F32)
    else:
        mm = lambda p, q: lax.dot_general(p, q, dims, preferred_element_type=F32)
    return mm(a_hi, b_hi) + mm(a_lo, b_hi) + mm(a_hi, b_lo)


def _proj_ln_kernel(a_ref, w_ref, bias_ref, x_ref, g_ref, b_ref, o_ref):
    h = jnp.dot(a_ref[...], w_ref[...], preferred_element_type=F32) + bias_ref[...]
    o_ref[...] = _ln_rows(ALPHA * x_ref[...] + h, g_ref[...], b_ref[...])


def _proj_ln(a, w, bias, x, g, b):
    n, d = x.shape
    row = lambda i: (i, 0)
    fix = lambda i: (0, 0)
    return pl.pallas_call(
        _proj_ln_kernel,
        grid=(n // ROW_TILE,),
        in_specs=[pl.BlockSpec((ROW_TILE, a.shape[1]), row), pl.BlockSpec(w.shape, fix),
                  pl.BlockSpec((1, d), fix), pl.BlockSpec((ROW_TILE, d), row),
                  pl.BlockSpec((1, d), fix), pl.BlockSpec((1, d), fix)],
        out_specs=pl.BlockSpec((ROW_TILE, d), row),
        out_shape=jax.ShapeDtypeStruct((n, d), F32),
        compiler_params=_cparams(("parallel",)),
        name="proj_ln",
    )(a, w, _row2(bias), x, _row2(g), _row2(b))


def _gla_in_kernel(x_ref, w_ref, w2_ref, gb_ref, q_ref, k_ref, v_ref, g_ref, la_ref):
    kd, vd = GLA_KEY_DIM, GLA_VAL_DIM
    xb = x_ref[...].astype(BF16)
    proj = lambda lo, hi: jnp.dot(xb, w_ref[:, lo:hi], preferred_element_type=F32)
    glow = proj(2 * kd + 2 * vd, 2 * kd + 2 * vd + LANES)
    z = _dot_split(glow, w2_ref[0], w2_ref[1], None) + gb_ref[...]
    la_ref[...] = (jnp.minimum(z, 0.0) - jnp.log(1.0 + jnp.exp(-jnp.abs(z)))) * (LOG2E / GLA_TAU)
    og = proj(2 * kd + vd, 2 * kd + 2 * vd)
    g_ref[...] = (og * jax.nn.sigmoid(og)).astype(BF16)
    q_ref[...] = (proj(0, kd) * GLA_HEAD_K ** -0.5).astype(BF16)
    k_ref[...] = proj(kd, 2 * kd).astype(BF16)
    v_ref[...] = proj(2 * kd, 2 * kd + vd).astype(BF16)


def _gla_in(x, w_in, gate_w2, gate_b):
    n, d = x.shape
    kd, vd, r = GLA_KEY_DIM, GLA_VAL_DIM, GLA_GATE_RANK
    width = 2 * kd + 2 * vd
    w = jnp.pad(w_in, ((0, 0), (0, LANES - 2 * r))).astype(BF16)
    w2 = jnp.zeros((LANES, 2 * kd), F32)
    w2 = w2.at[:r, :kd].set(gate_w2[0]).at[r:2 * r, kd:].set(gate_w2[1])
    w2 = jnp.stack(_split_bf16(w2))
    gb = gate_b.reshape(1, 2 * kd)
    row = lambda i: (i, 0)
    fix = lambda i: (0, 0)
    return pl.pallas_call(
        _gla_in_kernel,
        grid=(n // ROW_TILE,),
        in_specs=[pl.BlockSpec((ROW_TILE, d), row), pl.BlockSpec((d, width + LANES), fix),
                  pl.BlockSpec((2, LANES, 2 * kd), lambda i: (0, 0, 0)), pl.BlockSpec((1, 2 * kd), fix)],
        out_specs=[pl.BlockSpec((ROW_TILE, kd), row), pl.BlockSpec((ROW_TILE, kd), row),
                   pl.BlockSpec((ROW_TILE, vd), row), pl.BlockSpec((ROW_TILE, vd), row),
                   pl.BlockSpec((ROW_TILE, 2 * kd), row)],
        out_shape=[jax.ShapeDtypeStruct((n, kd), BF16), jax.ShapeDtypeStruct((n, kd), BF16),
                   jax.ShapeDtypeStruct((n, vd), BF16), jax.ShapeDtypeStruct((n, vd), BF16),
                   jax.ShapeDtypeStruct((n, 2 * kd), F32)],
        compiler_params=_cparams(("parallel",)),
        name="gla_in",
    )(x, w, w2, gb)


def _gla_direction(q_ref, k_ref, v_ref, la_ref, st_ref, o_ref, reverse):
    rows, cr = q_ref.shape[0], GLA_CHUNK
    nch = rows // cr
    shift = cr.bit_length() - 1
    ri = lax.broadcasted_iota(I32, (rows, rows), 0)
    ci = lax.broadcasted_iota(I32, (rows, rows), 1)
    same_chunk = jnp.right_shift(ri, shift) == jnp.right_shift(ci, shift)
    within = (ci >= ri) if reverse else (ci <= ri)
    cum_mat = jnp.where(same_chunk & within, 1.0, 0.0).astype(BF16)
    la_all = la_ref[...]
    la_hi = la_all.astype(BF16)
    rest = la_all - la_hi.astype(F32)
    la_mid = rest.astype(BF16)
    la_lo = (rest - la_mid.astype(F32)).astype(BF16)
    c_heads = (jnp.dot(cum_mat, la_hi, preferred_element_type=F32)
               + jnp.dot(cum_mat, la_mid, preferred_element_type=F32)
               + jnp.dot(cum_mat, la_lo, preferred_element_type=F32))
    rj = lax.broadcasted_iota(I32, (cr, cr), 0)
    cj = lax.broadcasted_iota(I32, (cr, cr), 1)
    mask = (cj > rj) if reverse else (cj <= rj)

    for head in range(st_ref.shape[0]):
        kcols = slice(head * GLA_HEAD_K, (head + 1) * GLA_HEAD_K)
        vcols = slice(head * GLA_HEAD_V, (head + 1) * GLA_HEAD_V)
        la, c_all = la_all[:, kcols], c_heads[:, kcols]
        qes, decs, intra, upd = [], [], [], []
        for ch in range(nch):
            lo = ch * cr
            c = c_all[lo:lo + cr]
            qf = q_ref[lo:lo + cr, kcols].astype(F32)
            kf = k_ref[lo:lo + cr, kcols].astype(F32)
            v = v_ref[lo:lo + cr, vcols]
            ctot = c[0:1] if reverse else c[cr - 1:cr]
            qes.append((qf * jnp.exp2(c)).astype(BF16))
            decs.append(jnp.exp2(ctot))
            blocks = []
            for i in range(cr // GLA_SUB):
                blo, bhi = i * GLA_SUB, (i + 1) * GLA_SUB
                edge = bhi - 1 if reverse else blo
                ref = c[edge:edge + 1] - la[lo + edge:lo + edge + 1]
                qi = (qf[blo:bhi] * jnp.exp2(c[blo:bhi] - ref)).astype(BF16)
                ki = (kf * jnp.exp2(jnp.minimum(ref - c, GLA_EXP2_CLAMP))).astype(BF16)
                blocks.append(lax.dot_general(qi, ki, NT_DIMS, preferred_element_type=F32))
            scores = jnp.where(mask, jnp.concatenate(blocks, axis=0), 0.0).astype(BF16)
            intra.append(jnp.dot(scores, v, preferred_element_type=F32))
            kd = (kf * jnp.exp2(ctot - c)).astype(BF16)
            upd.append(lax.dot_general(v, kd, TN_DIMS, preferred_element_type=F32))

        st = st_ref[head]
        states = [None] * nch
        for ch in (reversed(range(nch)) if reverse else range(nch)):
            states[ch] = st
            st = st * decs[ch] + upd[ch]
        st_ref[head] = st
        for ch in range(nch):
            inter = lax.dot_general(qes[ch], states[ch].astype(BF16), NT_DIMS, preferred_element_type=F32)
            o_ref[ch * cr:(ch + 1) * cr, vcols] = (intra[ch] + inter).astype(BF16)


def _gla_scan_kernel(qf_ref, kf_ref, vf_ref, laf_ref, qb_ref, kb_ref, vb_ref, lab_ref,
                     of_ref, ob_ref, sf_ref, sb_ref):
    @pl.when(pl.program_id(2) == 0)
    def _():
        sf_ref[...] = jnp.zeros_like(sf_ref)
        sb_ref[...] = jnp.zeros_like(sb_ref)

    _gla_direction(qf_ref, kf_ref, vf_ref, laf_ref, sf_ref, of_ref, False)
    _gla_direction(qb_ref, kb_ref, vb_ref, lab_ref, sb_ref, ob_ref, True)


def _gla_scan(q, k, v, la, batch):
    n = q.shape[0]
    nblk = n // batch // GLA_ROWS
    hps = GLA_HEADS_PER_STEP
    hk, hv, ngrp = hps * GLA_HEAD_K, hps * GLA_HEAD_V, GLA_HEADS // hps
    fwd = lambda b, h, i: (b * nblk + i, h)
    bwd = lambda b, h, i: (b * nblk + nblk - 1 - i, h)
    bwd_la = lambda b, h, i: (b * nblk + nblk - 1 - i, ngrp + h)
    state = pltpu.VMEM((hps, GLA_HEAD_V, GLA_HEAD_K), F32)
    return pl.pallas_call(
        _gla_scan_kernel,
        grid=(batch, ngrp, nblk),
        in_specs=[pl.BlockSpec((GLA_ROWS, hk), fwd), pl.BlockSpec((GLA_ROWS, hk), fwd),
                  pl.BlockSpec((GLA_ROWS, hv), fwd), pl.BlockSpec((GLA_ROWS, hk), fwd),
                  pl.BlockSpec((GLA_ROWS, hk), bwd), pl.BlockSpec((GLA_ROWS, hk), bwd),
                  pl.BlockSpec((GLA_ROWS, hv), bwd), pl.BlockSpec((GLA_ROWS, hk), bwd_la)],
        out_specs=[pl.BlockSpec((GLA_ROWS, hv), fwd), pl.BlockSpec((GLA_ROWS, hv), bwd)],
        out_shape=[jax.ShapeDtypeStruct((n, GLA_VAL_DIM), BF16)] * 2,
        scratch_shapes=[state, state],
        compiler_params=_cparams(("parallel", "parallel", "arbitrary")),
        name="gla_scan",
    )(q, k, v, la, q, k, v, la)


def _gla_out_kernel(of_ref, ob_ref, gate_ref, ng_ref, w_ref, x_ref, g_ref, b_ref, o_ref):
    o = of_ref[...].astype(F32) + ob_ref[...].astype(F32)
    parts = []
    for h in range(GLA_HEADS):
        oh = o[:, h * GLA_HEAD_V:(h + 1) * GLA_HEAD_V]
        ms = jnp.mean(oh * oh, axis=-1, keepdims=True)
        parts.append(oh * lax.rsqrt(ms + GLA_NORM_EPS))
    on = jnp.concatenate(parts, axis=-1) * ng_ref[...]
    a = (on * gate_ref[...].astype(F32)).astype(BF16)
    h = jnp.dot(a, w_ref[...], preferred_element_type=F32)
    o_ref[...] = _ln_rows(ALPHA * x_ref[...] + h, g_ref[...], b_ref[...])


def _gla_mixer_ln(x, batch, w_in, gate_w2, gate_b, norm_g, w_out, ln_g, ln_b):
    n, d = x.shape
    q, k, v, gate, la = _gla_in(x, w_in, gate_w2, gate_b)
    o_f, o_b = _gla_scan(q, k, v, la, batch)
    row = lambda i: (i, 0)
    fix = lambda i: (0, 0)
    vd = GLA_VAL_DIM
    return pl.pallas_call(
        _gla_out_kernel,
        grid=(n // ROW_TILE,),
        in_specs=[pl.BlockSpec((ROW_TILE, vd), row), pl.BlockSpec((ROW_TILE, vd), row),
                  pl.BlockSpec((ROW_TILE, vd), row), pl.BlockSpec((1, vd), fix),
                  pl.BlockSpec((vd, d), fix), pl.BlockSpec((ROW_TILE, d), row),
                  pl.BlockSpec((1, d), fix), pl.BlockSpec((1, d), fix)],
        out_specs=pl.BlockSpec((ROW_TILE, d), row),
        out_shape=jax.ShapeDtypeStruct((n, d), F32),
        compiler_params=_cparams(("parallel",)),
        name="gla_out",
    )(o_f, o_b, gate, _row2(jnp.tile(norm_g, GLA_HEADS)), w_out.astype(BF16), x,
      _row2(ln_g), _row2(ln_b))


def _split_even_odd_kernel(x_ref, xe_ref, xo_ref, mix_ref):
    x = x_ref[...]
    half = x.shape[0] // 2
    nchunk = mix_ref.shape[0]
    for c in range(nchunk):
        mix_ref[c] = x[:, c * LANES:(c + 1) * LANES]
    for parity, o_ref in ((0, xe_ref), (1, xo_ref)):
        rows = pl.ds(parity, half, stride=2)
        o_ref[...] = jnp.concatenate([mix_ref[c, rows, :] for c in range(nchunk)], axis=1).astype(BF16)


def _split_even_odd(x):
    n, d = x.shape
    row = lambda i: (i, 0)
    half_shape = jax.ShapeDtypeStruct((n // 2, d), BF16)
    return pl.pallas_call(
        _split_even_odd_kernel,
        grid=(n // ROW_TILE,),
        in_specs=[pl.BlockSpec((ROW_TILE, d), row)],
        out_specs=[pl.BlockSpec((ROW_TILE // 2, d), row)] * 2,
        out_shape=[half_shape, half_shape],
        scratch_shapes=[pltpu.VMEM((d // LANES, ROW_TILE, LANES), F32)],
        compiler_params=_cparams(("parallel",)),
        name="split_even_odd",
    )(x)


def _hy_in_kernel(xe_ref, xo_ref, w_ref, b_ref, cw_ref, cb_ref, ue_ref, uo_ref):
    w, b = w_ref[...], b_ref[...]
    pe = jnp.dot(xe_ref[...], w, preferred_element_type=F32) + b
    po = jnp.dot(xo_ref[...], w, preferred_element_type=F32) + b
    rows = pe.shape[0]
    ri = lax.broadcasted_iota(I32, (rows, 1), 0)
    po_prev = jnp.where(ri == 0, 0.0, pltpu.roll(po, 1, axis=0))
    pe_next = jnp.where(ri == rows - 1, 0.0, pltpu.roll(pe, rows - 1, axis=0))
    cw, cb = cw_ref[...], cb_ref[...]
    ue_ref[...] = (po_prev * cw[0:1] + pe * cw[1:2] + po * cw[2:3] + cb).astype(BF16)
    uo_ref[...] = (pe * cw[0:1] + po * cw[1:2] + pe_next * cw[2:3] + cb).astype(BF16)


def _hy_in(xe, xo, batch, w_in, b_in, conv_w, conv_b):
    n2, d = xe.shape
    half = n2 // batch
    width = w_in.shape[1]
    cols = 256
    seq_spec = pl.BlockSpec((half, d), lambda b, j: (b, 0))
    col_spec = lambda r: pl.BlockSpec((r, cols), lambda b, j: (0, j))
    out_spec = pl.BlockSpec((half, cols), lambda b, j: (b, j))
    return pl.pallas_call(
        _hy_in_kernel,
        grid=(batch, width // cols),
        in_specs=[seq_spec, seq_spec, pl.BlockSpec((d, cols), lambda b, j: (0, j)), col_spec(1), col_spec(3),
                  col_spec(1)],
        out_specs=[out_spec, out_spec],
        out_shape=[jax.ShapeDtypeStruct((n2, width), BF16)] * 2,
        compiler_params=_cparams(("parallel", "arbitrary")),
        name="hy_in",
    )(xe, xo, w_in.astype(BF16), _row2(b_in), conv_w, _row2(conv_b))


def _hy_filter_kernel(pe_ref, w1_ref, b1_ref, f_ref, w2_ref, b2_ref, w3_ref, win_ref, hs_ref, hd_ref):
    f = f_ref[...]
    h = jnp.sin(f[0:1] * (jnp.dot(pe_ref[...], w1_ref[...], precision=HIGHEST,
                                  preferred_element_type=F32) + b1_ref[...]))
    h = jnp.sin(f[1:2] * (jnp.dot(h, w2_ref[...], precision=HIGHEST,
                                  preferred_element_type=F32) + b2_ref[...]))
    h = jnp.dot(h, w3_ref[...], precision=HIGHEST, preferred_element_type=F32)
    win = win_ref[...]
    d = win.shape[1]
    rows = win.shape[0]
    t = pl.program_id(0) * rows + lax.broadcasted_iota(I32, (rows, 1), 0)
    for o in range(HYENA_ORDER):
        hf = h[:, 2 * o * d:(2 * o + 1) * d] * win
        hb = jnp.where(t == 0, 0.0, h[:, (2 * o + 1) * d:(2 * o + 2) * d] * win)
        hs_ref[:, o * d:(o + 1) * d] = (hf + hb).astype(BF16)
        hd_ref[:, o * d:(o + 1) * d] = (hf - hb).astype(BF16)


def _hy_spec_kernel(c_ref, s_ref, cmid_ref, smid_ref, hs_ref, hd_ref, har_ref, hai_ref, hbr_ref, hbi_ref):
    c, s = c_ref[...], s_ref[...]
    hs, hd = hs_ref[...], hd_ref[...]
    n = lax.broadcasted_iota(I32, (hs.shape[0], 1), 0)
    sgn = jnp.where(jnp.bitwise_and(n, 1) == 0, 1.0, -1.0).astype(BF16)
    ri = lax.broadcasted_iota(I32, (c.shape[0], 1), 0) + pl.program_id(0) * c.shape[0]
    mid_r = jnp.dot(cmid_ref[...], hs, preferred_element_type=F32)[0:1]
    mid_i = jnp.dot(smid_ref[...], hd, preferred_element_type=F32)[0:1]
    har_ref[...] = jnp.dot(c, hs, preferred_element_type=F32)
    hai_ref[...] = jnp.where(ri == 0, mid_r, jnp.dot(s, hd, preferred_element_type=F32))
    hbr_ref[...] = jnp.dot(c, hs * sgn, preferred_element_type=F32)
    hbi_ref[...] = jnp.where(ri == 0, mid_i, -jnp.dot(s, hd * sgn, preferred_element_type=F32))


def _dft_consts(seq):
    k = jnp.arange(seq, dtype=I32)
    step = LANES
    assert seq % step == 0
    phase = lambda t: ((k[:, None] * t[None, :]) % (2 * seq)).astype(F32) * (math.pi / seq)
    ang_hi = phase(jnp.arange(seq // step, dtype=I32) * step)[:, :, None]
    ang_lo = phase(jnp.arange(step, dtype=I32))[:, None, :]
    cos = (jnp.cos(ang_hi) * jnp.cos(ang_lo) - jnp.sin(ang_hi) * jnp.sin(ang_lo)).reshape(seq, seq)
    sin = (jnp.sin(ang_hi) * jnp.cos(ang_lo) + jnp.cos(ang_hi) * jnp.sin(ang_lo)).reshape(seq, seq)
    sgn = jnp.where(k % 2 == 0, 1.0, -1.0).astype(F32)
    fwd_i = jnp.where(k[:, None] == 0, sgn[None, :], -sin)
    return cos.astype(BF16), fwd_i.astype(BF16)


def _hy_filters(seq, consts, w1, b1, sin_freq, w2, b2, w3):
    cos, fwd_i = consts[0], consts[1]
    d = w3.shape[-1] // (2 * HYENA_ORDER)
    hid = HYENA_FILTER_HIDDEN
    t = jnp.arange(seq, dtype=F32)
    t_norm = t / max(seq - 1, 1)
    bands = (HYENA_POS_EMB - 1) // 2
    freqs = jnp.linspace(1e-4, bands - 1, bands, dtype=F32)
    ang = (2.0 * math.pi / seq) * t[:, None] * freqs[None, :]
    pe = jnp.concatenate([t_norm[:, None], jnp.cos(ang), -jnp.sin(ang)], axis=-1)
    pe = jnp.pad(pe, ((0, 0), (0, LANES - HYENA_POS_EMB)))
    min_decay = math.log(HYENA_DECAY_TARGET) / HYENA_SLOW_DECAY
    max_decay = math.log(HYENA_DECAY_TARGET) / HYENA_FAST_DECAY
    deltas = jnp.abs(jnp.linspace(min_decay, max_decay, d, dtype=F32))
    window = jnp.exp(-t_norm[:, None] * deltas[None, :]) + HYENA_WINDOW_SHIFT
    padc = LANES - hid
    w1p = jnp.pad(w1, ((0, LANES - HYENA_POS_EMB), (0, padc)))
    w2p = jnp.pad(w2, ((0, padc), (0, padc)))
    w3p = jnp.pad(w3, ((0, padc), (0, 0)))
    rows = 256
    fix = lambda i: (0, 0)
    row = lambda i: (i, 0)
    hs, hd = pl.pallas_call(
        _hy_filter_kernel,
        grid=(seq // rows,),
        in_specs=[pl.BlockSpec((rows, LANES), row), pl.BlockSpec((LANES, LANES), fix),
                  pl.BlockSpec((1, LANES), fix), pl.BlockSpec((2, LANES), fix),
                  pl.BlockSpec((LANES, LANES), fix), pl.BlockSpec((1, LANES), fix),
                  pl.BlockSpec((LANES, 2 * HYENA_ORDER * d), fix), pl.BlockSpec((rows, d), row)],
        out_specs=[pl.BlockSpec((rows, HYENA_ORDER * d), row)] * 2,
        out_shape=[jax.ShapeDtypeStruct((seq, HYENA_ORDER * d), BF16)] * 2,
        compiler_params=_cparams(("parallel",)),
        name="hy_filter",
    )(pe, w1p, _row2(jnp.pad(b1, (0, padc))), jnp.pad(sin_freq, ((0, 0), (0, padc))), w2p,
      _row2(jnp.pad(b2, (0, padc))), w3p, window)
    tf, tn = DFT_TILE, DFT_COLS
    half = seq // 2
    tn = DFT_SUBCOLS
    plane = jax.ShapeDtypeStruct((half, HYENA_ORDER * d), F32)
    mat = pl.BlockSpec((tf, seq), lambda i, j: (i, 0))
    mid = pl.BlockSpec((BF16_SUBLANES, seq), lambda i, j: (half // BF16_SUBLANES, 0))
    col = pl.BlockSpec((seq, tn), lambda i, j: (0, j))
    return pl.pallas_call(
        _hy_spec_kernel,
        grid=(half // tf, HYENA_ORDER * d // tn),
        in_specs=[mat, mat, mid, mid, col, col],
        out_specs=[pl.BlockSpec((tf, tn), lambda i, j: (i, j))] * 4,
        out_shape=[plane] * 4,
        compiler_params=_cparams(("parallel", "arbitrary")),
        name="hy_spec",
    )(cos, fwd_i, cos, fwd_i, hs, hd)


def _dft_fwd_kernel(c_ref, s_ref, e_ref, o_ref, tw_ref, har_ref, hai_ref, hbr_ref, hbi_ref,
                    v0r_ref, v0i_ref, v1r_ref, v1i_ref):
    cm, sm = c_ref[...], s_ref[...]
    tw = tw_ref[...]
    c, s = tw[:, 0:1], tw[:, 1:2]
    out_refs = (v0r_ref, v0i_ref, v1r_ref, v1i_ref)

    def combine(er, ei, orr, oi, planes, c, s, first_row):
        har, hai, hbr, hbi = planes
        tr = c * orr + s * oi
        ti = c * oi - s * orr
        ar, br = er + tr, er - tr
        if first_row:
            ai, bi = ei, -oi
            yar, ybr = ar * har, br * hbr
            yai = ai * hai - bi * hbi
            ybi = ai * hbi + bi * hai
            return 0.5 * (yar + ybr), yai, 0.5 * (yar - ybr), -ybi
        ai, bi = ei + ti, ti - ei
        yar, yai = ar * har - ai * hai, ar * hai + ai * har
        ybr, ybi = br * hbr - bi * hbi, br * hbi + bi * hbr
        dr, di = yar - ybr, yai + ybi
        return yar + ybr, yai - ybi, c * dr - s * di, c * di + s * dr

    subs = [slice(lo, lo + DFT_SUBCOLS) for lo in range(0, e_ref.shape[2], DFT_SUBCOLS)]
    prods = []
    for cs in subs:
        e, o = e_ref[0, :, cs], o_ref[0, :, cs]
        prods.append((jnp.dot(cm, e, preferred_element_type=F32), jnp.dot(sm, e, preferred_element_type=F32),
                      jnp.dot(cm, o, preferred_element_type=F32), jnp.dot(sm, o, preferred_element_type=F32)))
    for cs, prod in zip(subs, prods):
        planes = (har_ref[:, cs], hai_ref[:, cs], hbr_ref[:, cs], hbi_ref[:, cs])
        for ref, val in zip(out_refs, combine(*prod, planes, c, s, False)):
            ref[0, :, cs] = val.astype(BF16)

    @pl.when(pl.program_id(0) == 0)
    def _():
        top = slice(0, BF16_SUBLANES)
        is_row0 = lax.broadcasted_iota(I32, (BF16_SUBLANES, 1), 0) == 0
        for cs, prod in zip(subs, prods):
            planes = (har_ref[top, cs], hai_ref[top, cs], hbr_ref[top, cs], hbi_ref[top, cs])
            head = [p[top] for p in prod]
            general = combine(*head, planes, c[top], s[top], False)
            special = combine(*head, planes, c[top], s[top], True)
            for ref, g, sp in zip(out_refs, general, special):
                ref[0, top, cs] = jnp.where(is_row0, sp, g).astype(BF16)


def _dft_inv_kernel(gr_ref, gi_ref, v0r_ref, v0i_ref, v1r_ref, v1i_ref, ze_ref, zo_ref, ge_ref, go_ref,
                    fb_ref, ye_ref, yo_ref):
    gr, gi = gr_ref[...], gi_ref[...]
    fb = fb_ref[...]
    ye = jnp.dot(gr, v0r_ref[0], preferred_element_type=F32) + jnp.dot(gi, v0i_ref[0], preferred_element_type=F32)
    ye_ref[0] = (ge_ref[0].astype(F32) * (ye + ze_ref[0].astype(F32) * fb)).astype(BF16)
    yo = jnp.dot(gr, v1r_ref[0], preferred_element_type=F32) + jnp.dot(gi, v1i_ref[0], preferred_element_type=F32)
    yo_ref[0] = (go_ref[0].astype(F32) * (yo + zo_ref[0].astype(F32) * fb)).astype(BF16)


def _long_conv_gate(consts, planes, order, z_e, z_o, gate_e, gate_o, filt_bias):
    cos, fwd_i, inv_r, inv_i, twiddle = consts
    (ze_arr, ze_col), (zo_arr, zo_col) = z_e, z_o
    (ge_arr, ge_col), (go_arr, go_col) = gate_e, gate_o
    batch, half, _ = ze_arr.shape
    d = D_MODEL
    tf, tn = DFT_TILE, DFT_COLS
    nj = d // tn
    plane_spec = pl.BlockSpec((tf, tn), lambda i, j, b: (i, order * nj + j))
    mat_f = pl.BlockSpec((tf, half), lambda i, j, b: (i, 0))
    cols_f = lambda off: pl.BlockSpec((1, half, tn), lambda i, j, b: (b, 0, off + j))
    tile_f = pl.BlockSpec((1, tf, tn), lambda i, j, b: (b, i, j))
    v_shape = jax.ShapeDtypeStruct((batch, half, d), BF16)
    v = pl.pallas_call(
        _dft_fwd_kernel,
        grid=(half // tf, nj, batch),
        in_specs=[mat_f, mat_f, cols_f(ze_col), cols_f(zo_col), pl.BlockSpec((tf, LANES), lambda i, j, b: (i, 0)),
                  plane_spec, plane_spec, plane_spec, plane_spec],
        out_specs=[tile_f] * 4,
        out_shape=[v_shape] * 4,
        compiler_params=_cparams(("parallel", "parallel", "arbitrary")),
        name="dft_fwd",
    )(cos, fwd_i, ze_arr, zo_arr, twiddle, *planes)
    mat_i = pl.BlockSpec((tf, half), lambda b, j, i: (i, 0))
    cols_i = pl.BlockSpec((1, half, tn), lambda b, j, i: (b, 0, j))
    tile_i = lambda off: pl.BlockSpec((1, tf, tn), lambda b, j, i: (b, i, off + j))
    return pl.pallas_call(
        _dft_inv_kernel,
        grid=(batch, nj, half // tf),
        in_specs=[mat_i, mat_i, cols_i, cols_i, cols_i, cols_i, tile_i(ze_col), tile_i(zo_col),
                  tile_i(ge_col), tile_i(go_col), pl.BlockSpec((1, tn), lambda b, j, i: (0, j))],
        out_specs=[tile_i(0), tile_i(0)],
        out_shape=[v_shape, v_shape],
        compiler_params=_cparams(("parallel", "parallel", "arbitrary")),
        name="dft_inv",
    )(inv_r, inv_i, *v, ze_arr, zo_arr, ge_arr, go_arr, _row2(filt_bias))


def _proj_ln_pair_kernel(ae_ref, ao_ref, w_ref, bias_ref, x_ref, g_ref, b_ref, o_ref, mix_ref):
    w, bias = w_ref[...], bias_ref[...]
    half = ae_ref.shape[0]
    for parity, a_ref in ((0, ae_ref), (1, ao_ref)):
        h = jnp.dot(a_ref[...], w, preferred_element_type=F32) + bias
        for c in range(h.shape[1] // LANES):
            mix_ref[c, pl.ds(parity, half, stride=2), :] = h[:, c * LANES:(c + 1) * LANES]
    h = jnp.concatenate([mix_ref[c] for c in range(mix_ref.shape[0])], axis=1)
    o_ref[...] = _ln_rows(ALPHA * x_ref[...] + h, g_ref[...], b_ref[...])


def _proj_ln_pair(a_e, a_o, w, bias, x, g, b):
    n, d = x.shape
    rows = ROW_TILE // 2
    row = lambda i: (i, 0)
    fix = lambda i: (0, 0)
    return pl.pallas_call(
        _proj_ln_pair_kernel,
        grid=(n // ROW_TILE,),
        in_specs=[pl.BlockSpec((rows, d), row), pl.BlockSpec((rows, d), row), pl.BlockSpec(w.shape, fix),
                  pl.BlockSpec((1, d), fix), pl.BlockSpec((ROW_TILE, d), row),
                  pl.BlockSpec((1, d), fix), pl.BlockSpec((1, d), fix)],
        out_specs=pl.BlockSpec((ROW_TILE, d), row),
        out_shape=jax.ShapeDtypeStruct((n, d), F32),
        scratch_shapes=[pltpu.VMEM((d // LANES, ROW_TILE, LANES), F32)],
        compiler_params=_cparams(("parallel",)),
        name="proj_ln_pair",
    )(a_e, a_o, w, _row2(bias), x, _row2(g), _row2(b))


def _hyena_mixer_ln(x, batch, w_in, b_in, conv_w, conv_b, ffn_w1, ffn_b1, sin_freq, ffn_w2, ffn_b2,
                    ffn_w3, filt_bias, w_out, b_out, ln_g, ln_b):
    n, d = x.shape
    seq = n // batch
    half = seq // 2
    nj = d // DFT_COLS
    planes = _hy_filters(seq, _dft_consts(seq), ffn_w1, ffn_b1, sin_freq, ffn_w2, ffn_b2, ffn_w3)
    cos, fwd_i = _dft_consts(half)
    scale = 1.0 / seq
    k = jnp.arange(half, dtype=F32)[:, None] * (math.pi / seq)
    twiddle = jnp.pad(jnp.concatenate([jnp.cos(k), jnp.sin(k)], axis=1), ((0, 0), (0, LANES - 2)))
    consts = (cos, fwd_i, cos * scale, fwd_i.T * scale, twiddle)
    x_e, x_o = _split_even_odd(x)
    u_e, u_o = _hy_in(x_e, x_o, batch, w_in, b_in, conv_w, conv_b)
    u_e, u_o = u_e.reshape(batch, half, 3 * d), u_o.reshape(batch, half, 3 * d)
    y_e, y_o = _long_conv_gate(consts, planes, 0, (u_e, 0), (u_o, 0), (u_e, nj), (u_o, nj), filt_bias[0])
    y_e, y_o = _long_conv_gate(consts, planes, 1, (y_e, 0), (y_o, 0), (u_e, 2 * nj), (u_o, 2 * nj), filt_bias[1])
    return _proj_ln_pair(y_e.reshape(n // 2, d), y_o.reshape(n // 2, d), w_out.astype(BF16), b_out, x,
                         ln_g, ln_b)


def _na_in_kernel(x_ref, w_ref, b_ref, q_ref, k_ref, v_ref):
    d = D_MODEL
    p = jnp.dot(x_ref[...].astype(BF16), w_ref[...], preferred_element_type=F32) + b_ref[...]
    q_ref[...] = (p[:, :d] * (NA_HEAD_DIM ** -0.5 * LOG2E)).astype(BF16)
    k_ref[...] = p[:, d:2 * d].astype(BF16)
    v_ref[...] = p[:, 2 * d:].astype(BF16)


def _na_attn_kernel(q_ref, k_ref, vt_ref, bias_ref, o_ref):
    for i in range(NA_PAIRS):
        rp = pl.program_id(2) * NA_PAIRS + i
        o_ref[i * LANES:(i + 1) * LANES] = _na_pair(rp, q_ref[i * LANES:(i + 1) * LANES], k_ref, vt_ref, bias_ref)


def _na_pair(rp, q, k_ref, vt_ref, bias_ref):
    hpg = LANES // NA_HEAD_DIM
    n_rows = GRID_W
    r0 = 2 * rp
    rs0 = jnp.clip(r0 - NA_WIN_ROWS // 2, 0, n_rows - NA_WIN_ROWS)
    rs1 = jnp.clip(r0 + 1 - NA_WIN_ROWS // 2, 0, n_rows - NA_WIN_ROWS)
    start = jnp.minimum(rs0, n_rows - NA_KEY_ROWS)
    tok0 = pl.multiple_of(start * GRID_W, LANES)
    nkeys = NA_KEY_ROWS * GRID_W
    kw = k_ref[pl.ds(tok0, nkeys), :]
    vtw = vt_ref[0, :, pl.ds(tok0, nkeys)]
    lane = lax.broadcasted_iota(I32, (1, LANES), 1)
    sub = lax.broadcasted_iota(I32, (LANES, 1), 0)
    right = lane >= GRID_W
    acc = jnp.zeros((LANES, LANES), F32)
    for h in range(hpg):
        in_head = (lane >= h * NA_HEAD_DIM) & (lane < (h + 1) * NA_HEAD_DIM)
        qh = jnp.where(in_head, q, jnp.zeros_like(q))
        s = lax.dot_general(kw, qh, NT_DIMS, preferred_element_type=F32)
        tiles = []
        for a in range(NA_KEY_ROWS):
            kr = start + a
            dr0 = kr - r0 + (NA_WIN_ROWS - 1)
            bt = bias_ref[h, jnp.clip(dr0, 0, 2 * NA_WIN_ROWS - 1)]
            if a < 2 or a >= NA_WIN_ROWS:
                pen0 = jnp.where((kr >= rs0) & (kr < rs0 + NA_WIN_ROWS), 0.0, NEG)
                pen1 = jnp.where((kr >= rs1) & (kr < rs1 + NA_WIN_ROWS), 0.0, NEG)
                bt = bt + jnp.where(right, pen1, pen0)
            tiles.append(bt)
        s = s + jnp.concatenate(tiles, axis=0)
        m = jnp.max(s, axis=0, keepdims=True)
        p = jnp.exp2(s - m)
        inv = 1.0 / jnp.sum(p, axis=0, keepdims=True)
        in_head_rows = (sub >= h * NA_HEAD_DIM) & (sub < (h + 1) * NA_HEAD_DIM)
        pv = jnp.dot(vtw, p.astype(BF16), preferred_element_type=F32)
        acc = acc + jnp.where(in_head_rows, pv * inv, 0.0)
    return acc.T.astype(BF16)


def _na_bias_table(rpb):
    nh = rpb.shape[0]
    kc = jnp.arange(GRID_W)[:, None]
    qc = jnp.arange(GRID_W)[None, :]
    cs = jnp.clip(qc - NA_WIN_COLS // 2, 0, GRID_W - NA_WIN_COLS)
    inwin = (kc >= cs) & (kc < cs + NA_WIN_COLS)
    dc = jnp.clip(kc - qc + (NA_WIN_COLS - 1), 0, 2 * NA_WIN_COLS - 2)
    colb = jnp.where(inwin[None, None], rpb.astype(F32)[:, :, dc] * LOG2E, NEG)
    pad = jnp.full((nh, 1, GRID_W, GRID_W), NEG, F32)
    left = jnp.concatenate([colb, pad], axis=1)
    rgt = jnp.concatenate([pad, colb], axis=1)
    return jnp.concatenate([left, rgt], axis=-1)


def _na_mixer_ln(x, batch, w_in, b_in, rpb, w_out, b_out, ln_g, ln_b):
    n, d = x.shape
    seq = n // batch
    assert seq == GRID_W * GRID_W
    row = lambda i: (i, 0)
    fix = lambda i: (0, 0)
    qkv_shape = jax.ShapeDtypeStruct((n, d), BF16)
    q, k, v = pl.pallas_call(
        _na_in_kernel,
        grid=(n // ROW_TILE,),
        in_specs=[pl.BlockSpec((ROW_TILE, d), row), pl.BlockSpec((d, 3 * d), fix), pl.BlockSpec((1, 3 * d), fix)],
        out_specs=[pl.BlockSpec((ROW_TILE, d), row)] * 3,
        out_shape=[qkv_shape] * 3,
        compiler_params=_cparams(("parallel",)),
        name="na_in",
    )(x, w_in.astype(BF16), _row2(b_in))
    vt = jnp.swapaxes(v.reshape(batch, seq, d), 1, 2)
    bias = _na_bias_table(rpb)
    hpg = LANES // NA_HEAD_DIM
    n_pairs = GRID_W // 2
    n_steps = n_pairs // NA_PAIRS
    o = pl.pallas_call(
        _na_attn_kernel,
        grid=(batch, d // LANES, n_steps),
        in_specs=[pl.BlockSpec((NA_PAIRS * LANES, LANES), lambda b, g, r: (b * n_steps + r, g)),
                  pl.BlockSpec((seq, LANES), lambda b, g, r: (b, g)),
                  pl.BlockSpec((1, LANES, seq), lambda b, g, r: (b, g, 0)),
                  pl.BlockSpec((hpg, 2 * NA_WIN_ROWS, GRID_W, LANES), lambda b, g, r: (g, 0, 0, 0))],
        out_specs=pl.BlockSpec((NA_PAIRS * LANES, LANES), lambda b, g, r: (b * n_steps + r, g)),
        out_shape=qkv_shape,
        compiler_params=_cparams(("parallel", "parallel", "arbitrary")),
        name="na_attn",
    )(q, k, vt, bias)
    return _proj_ln(o, w_out.astype(BF16), b_out, x, ln_g, ln_b)


def _moe_route_kernel(x_ref, wr_ref, rb_ref, xs_ref, pos_ref, gate_ref, cnt_ref):
    ne, tb = N_EXPERTS, MOE_TB
    xb, x_lo = _split_bf16(x_ref[...])
    nt = lambda p, q: lax.dot_general(p, q, NT_DIMS, preferred_element_type=F32)
    logits = nt(wr_ref[0], xb) + nt(wr_ref[1], xb) + nt(wr_ref[0], x_lo) + rb_ref[...]
    eio = lax.broadcasted_iota(I32, (ne, tb), 0)
    work = logits
    vals, hots = [], []
    for _ in range(TOP_K):
        m = jnp.max(work, axis=0, keepdims=True)
        idx = jnp.min(jnp.where(work == m, eio, ne), axis=0, keepdims=True)
        hot = eio == idx
        vals.append(m)
        hots.append(hot)
        work = jnp.where(hot, -jnp.inf, work)
    exps = [jnp.exp(v - vals[0]) for v in vals]
    denom = exps[0] + exps[1] + exps[2] + exps[3]
    sel = jnp.zeros((ne, tb), F32)
    for hot in hots:
        sel = sel + jnp.where(hot, 1.0, 0.0)
    earlier = (lax.broadcasted_iota(I32, (tb, tb), 0) < lax.broadcasted_iota(I32, (tb, tb), 1))
    rank = jnp.dot(sel.astype(BF16), jnp.where(earlier, 1.0, 0.0).astype(BF16),
                   preferred_element_type=F32)
    cnt = jnp.sum(sel, axis=1, keepdims=True)
    cnt_pad = jnp.ceil(cnt * (1.0 / MOE_G)) * MOE_G
    below = (lax.broadcasted_iota(I32, (ne, ne), 1) < lax.broadcasted_iota(I32, (ne, ne), 0))
    seg_off = jnp.dot(jnp.where(below, 1.0, 0.0), jnp.broadcast_to(cnt_pad, (ne, tb)),
                      precision=HIGHEST, preferred_element_type=F32)
    slot = seg_off + rank
    pos = [jnp.sum(jnp.where(hot, slot, 0.0), axis=0, keepdims=True).astype(I32) for hot in hots]
    zero_row = jnp.zeros((1, tb), I32)
    pos_ref[0] = jnp.concatenate(pos + [zero_row] * (8 - TOP_K), axis=0)
    gate_ref[0] = jnp.concatenate([e / denom for e in exps] + [zero_row.astype(F32)] * (8 - TOP_K), axis=0)
    cnt_ref[0] = jnp.broadcast_to(cnt_pad, (ne, LANES)).astype(I32)
    local = lax.broadcasted_iota(I32, (MOE_PCH, tb), 0).astype(F32).astype(BF16)
    one, zero = jnp.ones((), BF16), jnp.zeros((), BF16)
    for r in range(MOE_RB // MOE_PCH):
        onehot = None
        for p in pos:
            hit = jnp.where(local == (p - r * MOE_PCH).astype(F32).astype(BF16), one, zero)
            onehot = hit if onehot is None else onehot + hit
        xs_ref[r * MOE_PCH:(r + 1) * MOE_PCH] = jnp.dot(
            onehot, xb, preferred_element_type=F32).astype(BF16)


def _moe_expert_kernel(te_ref, src_ref, nt_ref, xs_hbm, wgu_ref, bgu_ref, wd_ref, bd_ref,
                       ys_hbm, xbuf, ybuf, wgu_bf, wd_bf, sem_in, sem_out, *, n_blocks):
    t = pl.program_id(0)
    nt = nt_ref[0]
    s = t % 2

    def in_copy(chunk, slot, c):
        return pltpu.make_async_copy(
            xs_hbm.at[pl.ds(pl.multiple_of(chunk * MOE_G, MOE_G), MOE_G)],
            xbuf.at[slot, pl.ds(c * MOE_G, MOE_G)], sem_in.at[slot])

    def out_copy(chunk, slot, c):
        return pltpu.make_async_copy(
            ybuf.at[slot, pl.ds(c * MOE_G, MOE_G)],
            ys_hbm.at[pl.ds(pl.multiple_of(chunk * MOE_G, MOE_G), MOE_G)], sem_out.at[slot])

    def tile_chunks(tile):
        return [src_ref[tile * MOE_CPT + c] for c in range(MOE_CPT)]

    def start_in(tile, slot):
        for c, chunk in enumerate(tile_chunks(tile)):
            in_copy(chunk, slot, c).start()

    def start_out(tile, slot):
        chunks = tile_chunks(tile)
        for c, chunk in enumerate(chunks):
            spare = ((slot * MOE_CPT + c) % n_blocks) * MOE_BLOCK_CHUNKS + MOE_SPARE_CHUNK
            out_copy(jnp.where(chunk == MOE_ZERO_CHUNK, spare, chunk), slot, c).start()

    def wait_in(slot):
        for c in range(MOE_CPT):
            in_copy(0, slot, c).wait()

    def wait_out(slot):
        for c in range(MOE_CPT):
            out_copy(0, slot, c).wait()

    @pl.when(t == 0)
    def _():
        start_in(0, 0)

    @pl.when(t + 1 < nt)
    def _():
        start_in(t + 1, 1 - s)

    @pl.when(t < nt)
    def _():
        @pl.when(jnp.logical_or(t == 0, te_ref[t] != te_ref[jnp.maximum(t - 1, 0)]))
        def _():
            wgu_bf[...] = wgu_ref[0, 0].astype(BF16)
            wd_bf[...] = wd_ref[0, 0].astype(BF16)

        wait_in(s)
        f = wd_bf.shape[0]
        h = jnp.dot(xbuf[s], wgu_bf[...], preferred_element_type=F32) + bgu_ref[0, 0]
        glu = jnp.minimum(h[:, :f], SWIGLU_LIMIT)
        lin = jnp.clip(h[:, f:], -SWIGLU_LIMIT, SWIGLU_LIMIT)
        act = (glu * jax.nn.sigmoid(SWIGLU_ALPHA * glu) * (lin + 1.0)).astype(BF16)
        y = jnp.dot(act, wd_bf[...], preferred_element_type=F32) + bd_ref[0, 0]

        @pl.when(t >= 2)
        def _():
            wait_out(s)

        ybuf[s] = y.astype(BF16)
        start_out(t, s)

        @pl.when(t == nt - 1)
        def _():
            wait_out(s)

            @pl.when(t >= 1)
            def _():
                wait_out(1 - s)


def _moe_combine_kernel(ys_ref, pos_ref, gate_ref, x_ref, g_ref, b_ref, o_ref):
    tb = MOE_TB
    pos = pos_ref[...]
    gate = gate_ref[...].astype(BF16)
    local = lax.broadcasted_iota(I32, (tb, MOE_PCH), 1).astype(F32).astype(BF16)
    zero = jnp.zeros((), BF16)
    f = jnp.zeros((tb, D_MODEL), F32)
    for r in range(MOE_RB // MOE_PCH):
        w = None
        for k in range(TOP_K):
            pk = (pos[:, k:k + 1] - r * MOE_PCH).astype(F32).astype(BF16)
            wk = jnp.where(local == pk, gate[:, k:k + 1], zero)
            w = wk if w is None else w + wk
        f = f + jnp.dot(w, ys_ref[r * MOE_PCH:(r + 1) * MOE_PCH], preferred_element_type=F32)
    o_ref[...] = _ln_rows(ALPHA * x_ref[...] + f, g_ref[...], b_ref[...])


def _moe_tile_tables(cnt_pad, n_tiles_max):
    nb, ne = cnt_pad.shape
    nch_eb = (cnt_pad // MOE_G).T
    seg_off = jnp.cumsum(cnt_pad, axis=1) - cnt_pad
    base_eb = ((jnp.arange(nb, dtype=I32)[:, None] * MOE_RB + seg_off) // MOE_G).T
    first_eb = jnp.cumsum(nch_eb, axis=1) - nch_eb
    tiles_e = (jnp.sum(nch_eb, axis=1) + MOE_CPT - 1) // MOE_CPT
    tile_end = jnp.cumsum(tiles_e)
    n_tiles = tile_end[-1]
    tau = jnp.arange(n_tiles_max, dtype=I32)
    tile_e = jnp.minimum(jnp.sum(tau[:, None] >= tile_end[None, :], axis=1), ne - 1).astype(I32)
    of_tile = tile_e[:, None] == jnp.arange(ne, dtype=I32)[None, :]
    pick = lambda tab: jnp.sum(jnp.where(of_tile[:, :, None], tab[None], 0), axis=1)
    tile_first = tau - jnp.sum(jnp.where(of_tile, (tile_end - tiles_e)[None, :], 0), axis=1)
    k = (tile_first * MOE_CPT)[:, None] + jnp.arange(MOE_CPT, dtype=I32)[None, :]
    first, count, base = pick(first_eb), pick(nch_eb), pick(base_eb)
    inside = (k[:, :, None] >= first[:, None, :]) & (k[:, :, None] < (first + count)[:, None, :])
    src = jnp.sum(jnp.where(inside, (base - first)[:, None, :] + k[:, :, None], 0), axis=2)
    src = jnp.where(jnp.any(inside, axis=2) & (tau < n_tiles)[:, None], src, MOE_ZERO_CHUNK)
    return tile_e, src.reshape(-1).astype(I32), n_tiles.astype(I32).reshape(1)


def _moe_ln(x, layer, router_w, router_b, w_gu, b_gu, w_down, b_down, ln_g, ln_b):
    n, d = x.shape
    ne, f = N_EXPERTS, w_down.shape[2]
    depth = w_gu.shape[0]
    expert_block = lambda t, te, sr, nt: (layer, te[t], 0, 0)
    nb = n // MOE_TB
    assert MOE_RB >= MOE_TB * TOP_K + ne * (MOE_G - 1) + 2 * MOE_G
    blk = lambda i: (i, 0)
    blk3 = lambda i: (i, 0, 0)
    fix = lambda i: (0, 0)
    slot_rows = nb * MOE_RB
    xs, pos, gates, cnt = pl.pallas_call(
        _moe_route_kernel,
        grid=(nb,),
        in_specs=[pl.BlockSpec((MOE_TB, d), blk), pl.BlockSpec((2, ne, d), lambda i: (0, 0, 0)),
                  pl.BlockSpec((ne, 1), fix)],
        out_specs=[pl.BlockSpec((MOE_RB, d), blk), pl.BlockSpec((1, 8, MOE_TB), blk3),
                   pl.BlockSpec((1, 8, MOE_TB), blk3), pl.BlockSpec((1, ne, LANES), blk3)],
        out_shape=[jax.ShapeDtypeStruct((slot_rows, d), BF16), jax.ShapeDtypeStruct((nb, 8, MOE_TB), I32),
                   jax.ShapeDtypeStruct((nb, 8, MOE_TB), F32), jax.ShapeDtypeStruct((nb, ne, LANES), I32)],
        compiler_params=_cparams(("parallel",)),
        name="moe_route",
    )(x, jnp.stack(_split_bf16(router_w.T)), router_b.reshape(ne, 1))

    n_tiles_max = nb * MOE_RB // MOE_TM + ne
    tile_e, src, n_tiles = _moe_tile_tables(cnt[:, :, 0], n_tiles_max)
    ys = pl.pallas_call(
        functools.partial(_moe_expert_kernel, n_blocks=nb),
        grid_spec=pltpu.PrefetchScalarGridSpec(
            num_scalar_prefetch=3,
            grid=(n_tiles_max,),
            in_specs=[pl.BlockSpec(memory_space=pl.ANY),
                      pl.BlockSpec((1, 1, d, 2 * f), expert_block),
                      pl.BlockSpec((1, 1, 1, 2 * f), expert_block),
                      pl.BlockSpec((1, 1, f, d), expert_block),
                      pl.BlockSpec((1, 1, 1, d), expert_block)],
            out_specs=pl.BlockSpec(memory_space=pl.ANY),
            scratch_shapes=[pltpu.VMEM((2, MOE_TM, d), BF16), pltpu.VMEM((2, MOE_TM, d), BF16),
                            pltpu.VMEM((d, 2 * f), BF16), pltpu.VMEM((f, d), BF16),
                            pltpu.SemaphoreType.DMA((2,)), pltpu.SemaphoreType.DMA((2,))]),
        out_shape=jax.ShapeDtypeStruct((slot_rows, d), BF16),
        input_output_aliases={3: 0},
        compiler_params=pltpu.CompilerParams(dimension_semantics=("arbitrary",),
                                             vmem_limit_bytes=MOE_EXPERT_VMEM),
        name="moe_expert",
    )(tile_e, src, n_tiles, xs, w_gu, b_gu.reshape(depth, ne, 1, 2 * f), w_down,
      b_down.reshape(depth, ne, 1, d))

    pos_t = jnp.swapaxes(pos, 1, 2).reshape(n, 8)
    gate_t = jnp.swapaxes(gates, 1, 2).reshape(n, 8)
    return pl.pallas_call(
        _moe_combine_kernel,
        grid=(nb,),
        in_specs=[pl.BlockSpec((MOE_RB, d), blk), pl.BlockSpec((MOE_TB, 8), blk), pl.BlockSpec((MOE_TB, 8), blk),
                  pl.BlockSpec((MOE_TB, d), blk), pl.BlockSpec((1, d), fix), pl.BlockSpec((1, d), fix)],
        out_specs=pl.BlockSpec((MOE_TB, d), blk),
        out_shape=jax.ShapeDtypeStruct((n, d), F32),
        compiler_params=_cparams(("parallel",)),
        name="moe_combine",
    )(ys, pos_t, gate_t, x, _row2(ln_g), _row2(ln_b))


def kernel(x, ln1_g, ln1_b, ln2_g, ln2_b, gla_w_in, gla_gate_w2, gla_gate_b, gla_norm_g, gla_w_out,
           hy_w_in, hy_b_in, hy_conv_w, hy_conv_b, hy_ffn_w1, hy_ffn_b1, hy_sin_freq, hy_ffn_w2,
           hy_ffn_b2, hy_ffn_w3, hy_filt_bias, hy_w_out, hy_b_out, na_w_in, na_b_in, na_rpb, na_w_out,
           na_b_out, moe_router_w, moe_router_b, moe_w_gu, moe_b_gu, moe_w_down, moe_b_down):
    batch, seq, d = x.shape
    xt = x.reshape(batch * seq, d)
    for i in range(DEPTH):
        m, j = i % N_MIXERS, i // N_MIXERS
        if m == 0:
            xt = _gla_mixer_ln(xt, batch, gla_w_in[j], gla_gate_w2[j], gla_gate_b[j], gla_norm_g[j],
                               gla_w_out[j], ln1_g[i], ln1_b[i])
        elif m == 1:
            xt = _hyena_mixer_ln(xt, batch, hy_w_in[j], hy_b_in[j], hy_conv_w[j], hy_conv_b[j],
                                 hy_ffn_w1[j], hy_ffn_b1[j], hy_sin_freq[j], hy_ffn_w2[j], hy_ffn_b2[j],
                                 hy_ffn_w3[j], hy_filt_bias[j], hy_w_out[j], hy_b_out[j],
                                 ln1_g[i], ln1_b[i])
        else:
            xt = _na_mixer_ln(xt, batch, na_w_in[j], na_b_in[j], na_rpb[j], na_w_out[j], na_b_out[j],
                              ln1_g[i], ln1_b[i])
        xt = _moe_ln(xt, i, moe_router_w[i], moe_router_b[i], moe_w_gu, moe_b_gu, moe_w_down,
                     moe_b_down, ln2_g[i], ln2_b[i])
    return xt.reshape(batch, seq, d)
```

```python
import functools
import math

import jax
import jax.numpy as jnp
from jax import lax
from jax.experimental import pallas as pl
from jax.experimental.pallas import tpu as pltpu

F32, BF16, I32 = jnp.float32, jnp.bfloat16, jnp.int32
HIGHEST = lax.Precision.HIGHEST
NT_DIMS = (((1,), (1,)), ((), ()))
TN_DIMS = (((0,), (0,)), ((), ()))

D_MODEL = 1024
DEPTH = 4
N_MIXERS = 3
GRID_W = 64
ALPHA = (2 * DEPTH) ** 0.25
LN_EPS = 1e-5

GLA_HEADS = 4
GLA_KEY_DIM = 512
GLA_VAL_DIM = 1024
GLA_HEAD_K = 128
GLA_HEAD_V = 256
GLA_GATE_RANK = 16
GLA_TAU = 16.0
GLA_NORM_EPS = 1e-6

HYENA_ORDER = 2
HYENA_POS_EMB = 33
HYENA_FILTER_HIDDEN = 64
HYENA_DECAY_TARGET = 1e-2
HYENA_FAST_DECAY = 0.3
HYENA_SLOW_DECAY = 1.5
HYENA_WINDOW_SHIFT = 0.05

NA_HEAD_DIM = 32
NA_HEADS = 32
NA_WIN_ROWS = 8
NA_WIN_COLS = 16

N_EXPERTS = 32
TOP_K = 4
SWIGLU_ALPHA = 1.702
SWIGLU_LIMIT = 7.0

LANES = 128
BF16_SUBLANES = 16
VMEM_LIMIT = 48 * 1024 * 1024
MOE_EXPERT_VMEM = 56 * 1024 * 1024

ROW_TILE = 512
GLA_CHUNK = 64
GLA_SUB = 16
GLA_ROWS = 256
GLA_HEADS_PER_STEP = 4
GLA_EXP2_CLAMP = 115.0
DFT_TILE = 512
DFT_COLS = 512
DFT_SUBCOLS = 256
NA_KEY_ROWS = 10
NA_PAIRS = 16
NEG = -1e30
LOG2E = math.log2(math.e)

MOE_TB = 512
MOE_G = BF16_SUBLANES
MOE_RB = MOE_TB * TOP_K + 512
MOE_PCH = 256
MOE_BLOCK_CHUNKS = MOE_RB // MOE_G
MOE_ZERO_CHUNK = MOE_BLOCK_CHUNKS - 1
MOE_SPARE_CHUNK = MOE_BLOCK_CHUNKS - 2
MOE_TM = 512
MOE_CPT = MOE_TM // MOE_G


def _cparams(sem):
    return pltpu.CompilerParams(dimension_semantics=sem, vmem_limit_bytes=VMEM_LIMIT)


def _ln_rows(y, g, b):
    mu = jnp.mean(y, axis=-1, keepdims=True)
    yc = y - mu
    var = jnp.mean(yc * yc, axis=-1, keepdims=True)
    return yc * lax.rsqrt(var + LN_EPS) * g + b


def _row2(v):
    return v.reshape(1, -1)


def _split_bf16(a):
    hi = a.astype(BF16)
    return hi, (a - hi.astype(F32)).astype(BF16)


def _dot_split(a, b_hi, b_lo, dims):
    a_hi, a_lo = _split_bf16(a)
    if dims is None:
        mm = lambda p, q: jnp.dot(p, q, preferred_element_type=F32)
    else:
        mm = lambda p, q: lax.dot_general(p, q, dims, preferred_element_type=F32)
    return mm(a_hi, b_hi) + mm(a_lo, b_hi) + mm(a_hi, b_lo)


def _proj_ln_kernel(a_ref, w_ref, bias_ref, x_ref, g_ref, b_ref, o_ref):
    h = jnp.dot(a_ref[...], w_ref[...], preferred_element_type=F32) + bias_ref[...]
    o_ref[...] = _ln_rows(ALPHA * x_ref[...] + h, g_ref[...], b_ref[...])


def _proj_ln(a, w, bias, x, g, b):
    n, d = x.shape
    row = lambda i: (i, 0)
    fix = lambda i: (0, 0)
    return pl.pallas_call(
        _proj_ln_kernel,
        grid=(n // ROW_TILE,),
        in_specs=[pl.BlockSpec((ROW_TILE, a.shape[1]), row), pl.BlockSpec(w.shape, fix),
                  pl.BlockSpec((1, d), fix), pl.BlockSpec((ROW_TILE, d), row),
                  pl.BlockSpec((1, d), fix), pl.BlockSpec((1, d), fix)],
        out_specs=pl.BlockSpec((ROW_TILE, d), row),
        out_shape=jax.ShapeDtypeStruct((n, d), F32),
        compiler_params=_cparams(("parallel",)),
        name="proj_ln",
    )(a, w, _row2(bias), x, _row2(g), _row2(b))


def _gla_in_kernel(x_ref, w_ref, w2_ref, gb_ref, q_ref, k_ref, v_ref, g_ref, la_ref):
    kd, vd = GLA_KEY_DIM, GLA_VAL_DIM
    xb = x_ref[...].astype(BF16)
    proj = lambda lo, hi: jnp.dot(xb, w_ref[:, lo:hi], preferred_element_type=F32)
    glow = proj(2 * kd + 2 * vd, 2 * kd + 2 * vd + LANES)
    z = _dot_split(glow, w2_ref[0], w2_ref[1], None) + gb_ref[...]
    la_ref[...] = (jnp.minimum(z, 0.0) - jnp.log(1.0 + jnp.exp(-jnp.abs(z)))) * (LOG2E / GLA_TAU)
    og = proj(2 * kd + vd, 2 * kd + 2 * vd)
    g_ref[...] = (og * jax.nn.sigmoid(og)).astype(BF16)
    q_ref[...] = (proj(0, kd) * GLA_HEAD_K ** -0.5).astype(BF16)
    k_ref[...] = proj(kd, 2 * kd).astype(BF16)
    v_ref[...] = proj(2 * kd, 2 * kd + vd).astype(BF16)


def _gla_in(x, w_in, gate_w2, gate_b):
    n, d = x.shape
    kd, vd, r = GLA_KEY_DIM, GLA_VAL_DIM, GLA_GATE_RANK
    width = 2 * kd + 2 * vd
    w = jnp.pad(w_in, ((0, 0), (0, LANES - 2 * r))).astype(BF16)
    w2 = jnp.zeros((LANES, 2 * kd), F32)
    w2 = w2.at[:r, :kd].set(gate_w2[0]).at[r:2 * r, kd:].set(gate_w2[1])
    w2 = jnp.stack(_split_bf16(w2))
    gb = gate_b.reshape(1, 2 * kd)
    row = lambda i: (i, 0)
    fix = lambda i: (0, 0)
    return pl.pallas_call(
        _gla_in_kernel,
        grid=(n // ROW_TILE,),
        in_specs=[pl.BlockSpec((ROW_TILE, d), row), pl.BlockSpec((d, width + LANES), fix),
                  pl.BlockSpec((2, LANES, 2 * kd), lambda i: (0, 0, 0)), pl.BlockSpec((1, 2 * kd), fix)],
        out_specs=[pl.BlockSpec((ROW_TILE, kd), row), pl.BlockSpec((ROW_TILE, kd), row),
                   pl.BlockSpec((ROW_TILE, vd), row), pl.BlockSpec((ROW_TILE, vd), row),
                   pl.BlockSpec((ROW_TILE, 2 * kd), row)],
        out_shape=[jax.ShapeDtypeStruct((n, kd), BF16), jax.ShapeDtypeStruct((n, kd), BF16),
                   jax.ShapeDtypeStruct((n, vd), BF16), jax.ShapeDtypeStruct((n, vd), BF16),
                   jax.ShapeDtypeStruct((n, 2 * kd), F32)],
        compiler_params=_cparams(("parallel",)),
        name="gla_in",
    )(x, w, w2, gb)


def _gla_direction(q_ref, k_ref, v_ref, la_ref, st_ref, o_ref, reverse):
    rows, cr = q_ref.shape[0], GLA_CHUNK
    nch = rows // cr
    shift = cr.bit_length() - 1
    ri = lax.broadcasted_iota(I32, (rows, rows), 0)
    ci = lax.broadcasted_iota(I32, (rows, rows), 1)
    same_chunk = jnp.right_shift(ri, shift) == jnp.right_shift(ci, shift)
    within = (ci >= ri) if reverse else (ci <= ri)
    cum_mat = jnp.where(same_chunk & within, 1.0, 0.0).astype(BF16)
    la_all = la_ref[...]
    la_hi = la_all.astype(BF16)
    rest = la_all - la_hi.astype(F32)
    la_mid = rest.astype(BF16)
    la_lo = (rest - la_mid.astype(F32)).astype(BF16)
    c_heads = (jnp.dot(cum_mat, la_hi, preferred_element_type=F32)
               + jnp.dot(cum_mat, la_mid, preferred_element_type=F32)
               + jnp.dot(cum_mat, la_lo, preferred_element_type=F32))
    rj = lax.broadcasted_iota(I32, (cr, cr), 0)
    cj = lax.broadcasted_iota(I32, (cr, cr), 1)
    mask = (cj > rj) if reverse else (cj <= rj)

    for head in range(st_ref.shape[0]):
        kcols = slice(head * GLA_HEAD_K, (head + 1) * GLA_HEAD_K)
        vcols = slice(head * GLA_HEAD_V, (head + 1) * GLA_HEAD_V)
        la, c_all = la_all[:, kcols], c_heads[:, kcols]
        qes, decs, intra, upd = [], [], [], []
        for ch in range(nch):
            lo = ch * cr
            c = c_all[lo:lo + cr]
            qf = q_ref[lo:lo + cr, kcols].astype(F32)
            kf = k_ref[lo:lo + cr, kcols].astype(F32)
            v = v_ref[lo:lo + cr, vcols]
            ctot = c[0:1] if reverse else c[cr - 1:cr]
            qes.append((qf * jnp.exp2(c)).astype(BF16))
            decs.append(jnp.exp2(ctot))
            blocks = []
            for i in range(cr // GLA_SUB):
                blo, bhi = i * GLA_SUB, (i + 1) * GLA_SUB
                edge = bhi - 1 if reverse else blo
                ref = c[edge:edge + 1] - la[lo + edge:lo + edge + 1]
                qi = (qf[blo:bhi] * jnp.exp2(c[blo:bhi] - ref)).astype(BF16)
                ki = (kf * jnp.exp2(jnp.minimum(ref - c, GLA_EXP2_CLAMP))).astype(BF16)
                blocks.append(lax.dot_general(qi, ki, NT_DIMS, preferred_element_type=F32))
            scores = jnp.where(mask, jnp.concatenate(blocks, axis=0), 0.0).astype(BF16)
            intra.append(jnp.dot(scores, v, preferred_element_type=F32))
            kd = (kf * jnp.exp2(ctot - c)).astype(BF16)
            upd.append(lax.dot_general(v, kd, TN_DIMS, preferred_element_type=F32))

        st = st_ref[head]
        states = [None] * nch
        for ch in (reversed(range(nch)) if reverse else range(nch)):
            states[ch] = st
            st = st * decs[ch] + upd[ch]
        st_ref[head] = st
        for ch in range(nch):
            inter = lax.dot_general(qes[ch], states[ch].astype(BF16), NT_DIMS, preferred_element_type=F32)
            o_ref[ch * cr:(ch + 1) * cr, vcols] = (intra[ch] + inter).astype(BF16)


def _gla_scan_kernel(qf_ref, kf_ref, vf_ref, laf_ref, qb_ref, kb_ref, vb_ref, lab_ref,
                     of_ref, ob_ref, sf_ref, sb_ref):
    @pl.when(pl.program_id(2) == 0)
    def _():
        sf_ref[...] = jnp.zeros_like(sf_ref)
        sb_ref[...] = jnp.zeros_like(sb_ref)

    _gla_direction(qf_ref, kf_ref, vf_ref, laf_ref, sf_ref, of_ref, False)
    _gla_direction(qb_ref, kb_ref, vb_ref, lab_ref, sb_ref, ob_ref, True)


def _gla_scan(q, k, v, la, batch):
    n = q.shape[0]
    nblk = n // batch // GLA_ROWS
    hps = GLA_HEADS_PER_STEP
    hk, hv, ngrp = hps * GLA_HEAD_K, hps * GLA_HEAD_V, GLA_HEADS // hps
    fwd = lambda b, h, i: (b * nblk + i, h)
    bwd = lambda b, h, i: (b * nblk + nblk - 1 - i, h)
    bwd_la = lambda b, h, i: (b * nblk + nblk - 1 - i, ngrp + h)
    state = pltpu.VMEM((hps, GLA_HEAD_V, GLA_HEAD_K), F32)
    return pl.pallas_call(
        _gla_scan_kernel,
        grid=(batch, ngrp, nblk),
        in_specs=[pl.BlockSpec((GLA_ROWS, hk), fwd), pl.BlockSpec((GLA_ROWS, hk), fwd),
                  pl.BlockSpec((GLA_ROWS, hv), fwd), pl.BlockSpec((GLA_ROWS, hk), fwd),
                  pl.BlockSpec((GLA_ROWS, hk), bwd), pl.BlockSpec((GLA_ROWS, hk), bwd),
                  pl.BlockSpec((GLA_ROWS, hv), bwd), pl.BlockSpec((GLA_ROWS, hk), bwd_la)],
        out_specs=[pl.BlockSpec((GLA_ROWS, hv), fwd), pl.BlockSpec((GLA_ROWS, hv), bwd)],
        out_shape=[jax.ShapeDtypeStruct((n, GLA_VAL_DIM), BF16)] * 2,
        scratch_shapes=[state, state],
        compiler_params=_cparams(("parallel", "parallel", "arbitrary")),
        name="gla_scan",
    )(q, k, v, la, q, k, v, la)


def _gla_out_kernel(of_ref, ob_ref, gate_ref, ng_ref, w_ref, x_ref, g_ref, b_ref, o_ref):
    o = of_ref[...].astype(F32) + ob_ref[...].astype(F32)
    parts = []
    for h in range(GLA_HEADS):
        oh = o[:, h * GLA_HEAD_V:(h + 1) * GLA_HEAD_V]
        ms = jnp.mean(oh * oh, axis=-1, keepdims=True)
        parts.append(oh * lax.rsqrt(ms + GLA_NORM_EPS))
    on = jnp.concatenate(parts, axis=-1) * ng_ref[...]
    a = (on * gate_ref[...].astype(F32)).astype(BF16)
    h = jnp.dot(a, w_ref[...], preferred_element_type=F32)
    o_ref[...] = _ln_rows(ALPHA * x_ref[...] + h, g_ref[...], b_ref[...])


def _gla_mixer_ln(x, batch, w_in, gate_w2, gate_b, norm_g, w_out, ln_g, ln_b):
    n, d = x.shape
    q, k, v, gate, la = _gla_in(x, w_in, gate_w2, gate_b)
    o_f, o_b = _gla_scan(q, k, v, la, batch)
    row = lambda i: (i, 0)
    fix = lambda i: (0, 0)
    vd = GLA_VAL_DIM
    return pl.pallas_call(
        _gla_out_kernel,
        grid=(n // ROW_TILE,),
        in_specs=[pl.BlockSpec((ROW_TILE, vd), row), pl.BlockSpec((ROW_TILE, vd), row),
                  pl.BlockSpec((ROW_TILE, vd), row), pl.BlockSpec((1, vd), fix),
                  pl.BlockSpec((vd, d), fix), pl.BlockSpec((ROW_TILE, d), row),
                  pl.BlockSpec((1, d), fix), pl.BlockSpec((1, d), fix)],
        out_specs=pl.BlockSpec((ROW_TILE, d), row),
        out_shape=jax.ShapeDtypeStruct((n, d), F32),
        compiler_params=_cparams(("parallel",)),
        name="gla_out",
    )(o_f, o_b, gate, _row2(jnp.tile(norm_g, GLA_HEADS)), w_out.astype(BF16), x,
      _row2(ln_g), _row2(ln_b))


def _split_even_odd_kernel(x_ref, xe_ref, xo_ref, mix_ref):
    x = x_ref[...]
    half = x.shape[0] // 2
    nchunk = mix_ref.shape[0]
    for c in range(nchunk):
        mix_ref[c] = x[:, c * LANES:(c + 1) * LANES]
    for parity, o_ref in ((0, xe_ref), (1, xo_ref)):
        rows = pl.ds(parity, half, stride=2)
        o_ref[...] = jnp.concatenate([mix_ref[c, rows, :] for c in range(nchunk)], axis=1).astype(BF16)


def _split_even_odd(x):
    n, d = x.shape
    row = lambda i: (i, 0)
    half_shape = jax.ShapeDtypeStruct((n // 2, d), BF16)
    return pl.pallas_call(
        _split_even_odd_kernel,
        grid=(n // ROW_TILE,),
        in_specs=[pl.BlockSpec((ROW_TILE, d), row)],
        out_specs=[pl.BlockSpec((ROW_TILE // 2, d), row)] * 2,
        out_shape=[half_shape, half_shape],
        scratch_shapes=[pltpu.VMEM((d // LANES, ROW_TILE, LANES), F32)],
        compiler_params=_cparams(("parallel",)),
        name="split_even_odd",
    )(x)


def _hy_in_kernel(xe_ref, xo_ref, w_ref, b_ref, cw_ref, cb_ref, ue_ref, uo_ref):
    w, b = w_ref[...], b_ref[...]
    qe = jnp.dot(xe_ref[...], w, preferred_element_type=F32)
    qo = jnp.dot(xo_ref[...], w, preferred_element_type=F32)
    rows = qe.shape[0]
    cw = cw_ref[...]
    w0, w1, w2 = cw[0:1], cw[1:2], cw[2:3]
    const = cb_ref[...] + b * (w0 + w1 + w2)
    ue_ref[...] = (pltpu.roll(qo, 1, axis=0) * w0 + qe * w1 + qo * w2 + const).astype(BF16)
    uo_ref[...] = (qe * w0 + qo * w1 + pltpu.roll(qe, rows - 1, axis=0) * w2 + const).astype(BF16)
    g = BF16_SUBLANES
    ri = lax.broadcasted_iota(I32, (g, 1), 0)
    qe_s, qo_s = qe[0:g], qo[0:g]
    inner = qe_s * w1 + qo_s * w2 + const
    ue_ref[0:g] = jnp.where(ri == 0, inner - b * w0, inner + pltpu.roll(qo_s, 1, axis=0) * w0).astype(BF16)
    qe_s, qo_s = qe[rows - g:], qo[rows - g:]
    inner = qe_s * w0 + qo_s * w1 + const
    uo_ref[rows - g:] = jnp.where(ri == g - 1, inner - b * w2,
                                  inner + pltpu.roll(qe_s, g - 1, axis=0) * w2).astype(BF16)


def _hy_in(xe, xo, batch, w_in, b_in, conv_w, conv_b):
    n2, d = xe.shape
    half = n2 // batch
    width = w_in.shape[1]
    cols = 256
    seq_spec = pl.BlockSpec((half, d), lambda b, j: (b, 0))
    col_spec = lambda r: pl.BlockSpec((r, cols), lambda b, j: (0, j))
    out_spec = pl.BlockSpec((half, cols), lambda b, j: (b, j))
    return pl.pallas_call(
        _hy_in_kernel,
        grid=(batch, width // cols),
        in_specs=[seq_spec, seq_spec, pl.BlockSpec((d, cols), lambda b, j: (0, j)), col_spec(1), col_spec(3),
                  col_spec(1)],
        out_specs=[out_spec, out_spec],
        out_shape=[jax.ShapeDtypeStruct((n2, width), BF16)] * 2,
        compiler_params=_cparams(("parallel", "arbitrary")),
        name="hy_in",
    )(xe, xo, w_in.astype(BF16), _row2(b_in), conv_w, _row2(conv_b))


def _hy_filter_kernel(pe_ref, w1_ref, b1_ref, f_ref, w2_ref, b2_ref, w3_ref, win_ref, hs_ref, hd_ref):
    f = f_ref[...]
    h = jnp.sin(f[0:1] * (jnp.dot(pe_ref[...], w1_ref[...], precision=HIGHEST,
                                  preferred_element_type=F32) + b1_ref[...]))
    h = jnp.sin(f[1:2] * (jnp.dot(h, w2_ref[...], precision=HIGHEST,
                                  preferred_element_type=F32) + b2_ref[...]))
    h = jnp.dot(h, w3_ref[...], precision=HIGHEST, preferred_element_type=F32)
    win = win_ref[...]
    d = win.shape[1]
    rows = win.shape[0]
    t = pl.program_id(0) * rows + lax.broadcasted_iota(I32, (rows, 1), 0)
    for o in range(HYENA_ORDER):
        hf = h[:, 2 * o * d:(2 * o + 1) * d] * win
        hb = jnp.where(t == 0, 0.0, h[:, (2 * o + 1) * d:(2 * o + 2) * d] * win)
        hs_ref[:, o * d:(o + 1) * d] = (hf + hb).astype(BF16)
        hd_ref[:, o * d:(o + 1) * d] = (hf - hb).astype(BF16)


def _hy_spec_kernel(c_ref, s_ref, cmid_ref, smid_ref, hs_ref, hd_ref, har_ref, hai_ref, hbr_ref, hbi_ref):
    c, s = c_ref[...], s_ref[...]
    hs, hd = hs_ref[...], hd_ref[...]
    n = lax.broadcasted_iota(I32, (hs.shape[0], 1), 0)
    sgn = jnp.where(jnp.bitwise_and(n, 1) == 0, 1.0, -1.0).astype(BF16)
    ri = lax.broadcasted_iota(I32, (c.shape[0], 1), 0) + pl.program_id(0) * c.shape[0]
    mid_r = jnp.dot(cmid_ref[...], hs, preferred_element_type=F32)[0:1]
    mid_i = jnp.dot(smid_ref[...], hd, preferred_element_type=F32)[0:1]
    har_ref[...] = jnp.dot(c, hs, preferred_element_type=F32)
    hai_ref[...] = jnp.where(ri == 0, mid_r, jnp.dot(s, hd, preferred_element_type=F32))
    hbr_ref[...] = jnp.dot(c, hs * sgn, preferred_element_type=F32)
    hbi_ref[...] = jnp.where(ri == 0, mid_i, -jnp.dot(s, hd * sgn, preferred_element_type=F32))


def _dft_consts(seq):
    k = jnp.arange(seq, dtype=I32)
    step = LANES
    assert seq % step == 0
    phase = lambda t: ((k[:, None] * t[None, :]) % (2 * seq)).astype(F32) * (math.pi / seq)
    ang_hi = phase(jnp.arange(seq // step, dtype=I32) * step)[:, :, None]
    ang_lo = phase(jnp.arange(step, dtype=I32))[:, None, :]
    cos = (jnp.cos(ang_hi) * jnp.cos(ang_lo) - jnp.sin(ang_hi) * jnp.sin(ang_lo)).reshape(seq, seq)
    sin = (jnp.sin(ang_hi) * jnp.cos(ang_lo) + jnp.cos(ang_hi) * jnp.sin(ang_lo)).reshape(seq, seq)
    sgn = jnp.where(k % 2 == 0, 1.0, -1.0).astype(F32)
    fwd_i = jnp.where(k[:, None] == 0, sgn[None, :], -sin)
    return cos.astype(BF16), fwd_i.astype(BF16)


def _hy_filters(seq, consts, w1, b1, sin_freq, w2, b2, w3):
    cos, fwd_i = consts[0], consts[1]
    d = w3.shape[-1] // (2 * HYENA_ORDER)
    hid = HYENA_FILTER_HIDDEN
    t = jnp.arange(seq, dtype=F32)
    t_norm = t / max(seq - 1, 1)
    bands = (HYENA_POS_EMB - 1) // 2
    freqs = jnp.linspace(1e-4, bands - 1, bands, dtype=F32)
    ang = (2.0 * math.pi / seq) * t[:, None] * freqs[None, :]
    pe = jnp.concatenate([t_norm[:, None], jnp.cos(ang), -jnp.sin(ang)], axis=-1)
    pe = jnp.pad(pe, ((0, 0), (0, LANES - HYENA_POS_EMB)))
    min_decay = math.log(HYENA_DECAY_TARGET) / HYENA_SLOW_DECAY
    max_decay = math.log(HYENA_DECAY_TARGET) / HYENA_FAST_DECAY
    deltas = jnp.abs(jnp.linspace(min_decay, max_decay, d, dtype=F32))
    window = jnp.exp(-t_norm[:, None] * deltas[None, :]) + HYENA_WINDOW_SHIFT
    padc = LANES - hid
    w1p = jnp.pad(w1, ((0, LANES - HYENA_POS_EMB), (0, padc)))
    w2p = jnp.pad(w2, ((0, padc), (0, padc)))
    w3p = jnp.pad(w3, ((0, padc), (0, 0)))
    rows = 256
    fix = lambda i: (0, 0)
    row = lambda i: (i, 0)
    hs, hd = pl.pallas_call(
        _hy_filter_kernel,
        grid=(seq // rows,),
        in_specs=[pl.BlockSpec((rows, LANES), row), pl.BlockSpec((LANES, LANES), fix),
                  pl.BlockSpec((1, LANES), fix), pl.BlockSpec((2, LANES), fix),
                  pl.BlockSpec((LANES, LANES), fix), pl.BlockSpec((1, LANES), fix),
                  pl.BlockSpec((LANES, 2 * HYENA_ORDER * d), fix), pl.BlockSpec((rows, d), row)],
        out_specs=[pl.BlockSpec((rows, HYENA_ORDER * d), row)] * 2,
        out_shape=[jax.ShapeDtypeStruct((seq, HYENA_ORDER * d), BF16)] * 2,
        compiler_params=_cparams(("parallel",)),
        name="hy_filter",
    )(pe, w1p, _row2(jnp.pad(b1, (0, padc))), jnp.pad(sin_freq, ((0, 0), (0, padc))), w2p,
      _row2(jnp.pad(b2, (0, padc))), w3p, window)
    tf, tn = DFT_TILE, DFT_COLS
    half = seq // 2
    tn = DFT_SUBCOLS
    plane = jax.ShapeDtypeStruct((half, HYENA_ORDER * d), F32)
    mat = pl.BlockSpec((tf, seq), lambda i, j: (i, 0))
    mid = pl.BlockSpec((BF16_SUBLANES, seq), lambda i, j: (half // BF16_SUBLANES, 0))
    col = pl.BlockSpec((seq, tn), lambda i, j: (0, j))
    return pl.pallas_call(
        _hy_spec_kernel,
        grid=(half // tf, HYENA_ORDER * d // tn),
        in_specs=[mat, mat, mid, mid, col, col],
        out_specs=[pl.BlockSpec((tf, tn), lambda i, j: (i, j))] * 4,
        out_shape=[plane] * 4,
        compiler_params=_cparams(("parallel", "arbitrary")),
        name="hy_spec",
    )(cos, fwd_i, cos, fwd_i, hs, hd)


def _dft_fwd_kernel(c_ref, s_ref, e_ref, o_ref, tw_ref, har_ref, hai_ref, hbr_ref, hbi_ref,
                    v0r_ref, v0i_ref, v1r_ref, v1i_ref):
    cm, sm = c_ref[...], s_ref[...]
    tw = tw_ref[...]
    c, s = tw[:, 0:1], tw[:, 1:2]
    out_refs = (v0r_ref, v0i_ref, v1r_ref, v1i_ref)

    def combine(er, ei, orr, oi, planes, c, s, first_row):
        har, hai, hbr, hbi = planes
        tr = c * orr + s * oi
        ti = c * oi - s * orr
        ar, br = er + tr, er - tr
        if first_row:
            ai, bi = ei, -oi
            yar, ybr = ar * har, br * hbr
            yai = ai * hai - bi * hbi
            ybi = ai * hbi + bi * hai
            return 0.5 * (yar + ybr), yai, 0.5 * (yar - ybr), -ybi
        ai, bi = ei + ti, ti - ei
        yar, yai = ar * har - ai * hai, ar * hai + ai * har
        ybr, ybi = br * hbr - bi * hbi, br * hbi + bi * hbr
        dr, di = yar - ybr, yai + ybi
        return yar + ybr, yai - ybi, c * dr - s * di, c * di + s * dr

    subs = [slice(lo, lo + DFT_SUBCOLS) for lo in range(0, e_ref.shape[2], DFT_SUBCOLS)]
    prods = []
    for cs in subs:
        e, o = e_ref[0, :, cs], o_ref[0, :, cs]
        prods.append((jnp.dot(cm, e, preferred_element_type=F32), jnp.dot(sm, e, preferred_element_type=F32),
                      jnp.dot(cm, o, preferred_element_type=F32), jnp.dot(sm, o, preferred_element_type=F32)))
    for cs, prod in zip(subs, prods):
        planes = (har_ref[:, cs], hai_ref[:, cs], hbr_ref[:, cs], hbi_ref[:, cs])
        for ref, val in zip(out_refs, combine(*prod, planes, c, s, False)):
            ref[0, :, cs] = val.astype(BF16)

    @pl.when(pl.program_id(0) == 0)
    def _():
        top = slice(0, BF16_SUBLANES)
        is_row0 = lax.broadcasted_iota(I32, (BF16_SUBLANES, 1), 0) == 0
        for cs, prod in zip(subs, prods):
            planes = (har_ref[top, cs], hai_ref[top, cs], hbr_ref[top, cs], hbi_ref[top, cs])
            head = [p[top] for p in prod]
            general = combine(*head, planes, c[top], s[top], False)
            special = combine(*head, planes, c[top], s[top], True)
            for ref, g, sp in zip(out_refs, general, special):
                ref[0, top, cs] = jnp.where(is_row0, sp, g).astype(BF16)


def _dft_inv_kernel(gr_ref, gi_ref, v0r_ref, v0i_ref, v1r_ref, v1i_ref, ze_ref, zo_ref, ge_ref, go_ref,
                    fb_ref, ye_ref, yo_ref):
    gr, gi = gr_ref[...], gi_ref[...]
    fb = fb_ref[...]
    ye = jnp.dot(gr, v0r_ref[0], preferred_element_type=F32) + jnp.dot(gi, v0i_ref[0], preferred_element_type=F32)
    ye_ref[0] = (ge_ref[0].astype(F32) * (ye + ze_ref[0].astype(F32) * fb)).astype(BF16)
    yo = jnp.dot(gr, v1r_ref[0], preferred_element_type=F32) + jnp.dot(gi, v1i_ref[0], preferred_element_type=F32)
    yo_ref[0] = (go_ref[0].astype(F32) * (yo + zo_ref[0].astype(F32) * fb)).astype(BF16)


def _long_conv_gate(consts, planes, order, z_e, z_o, gate_e, gate_o, filt_bias):
    cos, fwd_i, inv_r, inv_i, twiddle = consts
    (ze_arr, ze_col), (zo_arr, zo_col) = z_e, z_o
    (ge_arr, ge_col), (go_arr, go_col) = gate_e, gate_o
    batch, half, _ = ze_arr.shape
    d = D_MODEL
    tf, tn = DFT_TILE, DFT_COLS
    nj = d // tn
    plane_spec = pl.BlockSpec((tf, tn), lambda i, j, b: (i, order * nj + j))
    mat_f = pl.BlockSpec((tf, half), lambda i, j, b: (i, 0))
    cols_f = lambda off: pl.BlockSpec((1, half, tn), lambda i, j, b: (b, 0, off + j))
    tile_f = pl.BlockSpec((1, tf, tn), lambda i, j, b: (b, i, j))
    v_shape = jax.ShapeDtypeStruct((batch, half, d), BF16)
    v = pl.pallas_call(
        _dft_fwd_kernel,
        grid=(half // tf, nj, batch),
        in_specs=[mat_f, mat_f, cols_f(ze_col), cols_f(zo_col), pl.BlockSpec((tf, LANES), lambda i, j, b: (i, 0)),
                  plane_spec, plane_spec, plane_spec, plane_spec],
        out_specs=[tile_f] * 4,
        out_shape=[v_shape] * 4,
        compiler_params=_cparams(("parallel", "parallel", "arbitrary")),
        name="dft_fwd",
    )(cos, fwd_i, ze_arr, zo_arr, twiddle, *planes)
    mat_i = pl.BlockSpec((tf, half), lambda b, j, i: (i, 0))
    cols_i = pl.BlockSpec((1, half, tn), lambda b, j, i: (b, 0, j))
    tile_i = lambda off: pl.BlockSpec((1, tf, tn), lambda b, j, i: (b, i, off + j))
    return pl.pallas_call(
        _dft_inv_kernel,
        grid=(batch, nj, half // tf),
        in_specs=[mat_i, mat_i, cols_i, cols_i, cols_i, cols_i, tile_i(ze_col), tile_i(zo_col),
                  tile_i(ge_col), tile_i(go_col), pl.BlockSpec((1, tn), lambda b, j, i: (0, j))],
        out_specs=[tile_i(0), tile_i(0)],
        out_shape=[v_shape, v_shape],
        compiler_params=_cparams(("parallel", "parallel", "arbitrary")),
        name="dft_inv",
    )(inv_r, inv_i, *v, ze_arr, zo_arr, ge_arr, go_arr, _row2(filt_bias))


def _proj_ln_pair_kernel(ae_ref, ao_ref, w_ref, bias_ref, x_ref, g_ref, b_ref, o_ref, mix_ref):
    w, bias = w_ref[...], bias_ref[...]
    half = ae_ref.shape[0]
    for parity, a_ref in ((0, ae_ref), (1, ao_ref)):
        h = jnp.dot(a_ref[...], w, preferred_element_type=F32) + bias
        for c in range(h.shape[1] // LANES):
            mix_ref[c, pl.ds(parity, half, stride=2), :] = h[:, c * LANES:(c + 1) * LANES]
    h = jnp.concatenate([mix_ref[c] for c in range(mix_ref.shape[0])], axis=1)
    o_ref[...] = _ln_rows(ALPHA * x_ref[...] + h, g_ref[...], b_ref[...])


def _proj_ln_pair(a_e, a_o, w, bias, x, g, b):
    n, d = x.shape
    rows = ROW_TILE // 2
    row = lambda i: (i, 0)
    fix = lambda i: (0, 0)
    return pl.pallas_call(
        _proj_ln_pair_kernel,
        grid=(n // ROW_TILE,),
        in_specs=[pl.BlockSpec((rows, d), row), pl.BlockSpec((rows, d), row), pl.BlockSpec(w.shape, fix),
                  pl.BlockSpec((1, d), fix), pl.BlockSpec((ROW_TILE, d), row),
                  pl.BlockSpec((1, d), fix), pl.BlockSpec((1, d), fix)],
        out_specs=pl.BlockSpec((ROW_TILE, d), row),
        out_shape=jax.ShapeDtypeStruct((n, d), F32),
        scratch_shapes=[pltpu.VMEM((d // LANES, ROW_TILE, LANES), F32)],
        compiler_params=_cparams(("parallel",)),
        name="proj_ln_pair",
    )(a_e, a_o, w, _row2(bias), x, _row2(g), _row2(b))


def _hyena_mixer_ln(x, batch, w_in, b_in, conv_w, conv_b, ffn_w1, ffn_b1, sin_freq, ffn_w2, ffn_b2,
                    ffn_w3, filt_bias, w_out, b_out, ln_g, ln_b):
    n, d = x.shape
    seq = n // batch
    half = seq // 2
    nj = d // DFT_COLS
    planes = _hy_filters(seq, _dft_consts(seq), ffn_w1, ffn_b1, sin_freq, ffn_w2, ffn_b2, ffn_w3)
    cos, fwd_i = _dft_consts(half)
    scale = 1.0 / seq
    k = jnp.arange(half, dtype=F32)[:, None] * (math.pi / seq)
    twiddle = jnp.pad(jnp.concatenate([jnp.cos(k), jnp.sin(k)], axis=1), ((0, 0), (0, LANES - 2)))
    consts = (cos, fwd_i, cos * scale, fwd_i.T * scale, twiddle)
    x_e, x_o = _split_even_odd(x)
    u_e, u_o = _hy_in(x_e, x_o, batch, w_in, b_in, conv_w, conv_b)
    u_e, u_o = u_e.reshape(batch, half, 3 * d), u_o.reshape(batch, half, 3 * d)
    y_e, y_o = _long_conv_gate(consts, planes, 0, (u_e, 0), (u_o, 0), (u_e, nj), (u_o, nj), filt_bias[0])
    y_e, y_o = _long_conv_gate(consts, planes, 1, (y_e, 0), (y_o, 0), (u_e, 2 * nj), (u_o, 2 * nj), filt_bias[1])
    return _proj_ln_pair(y_e.reshape(n // 2, d), y_o.reshape(n // 2, d), w_out.astype(BF16), b_out, x,
                         ln_g, ln_b)


def _na_in_kernel(x_ref, w_ref, b_ref, q_ref, k_ref, v_ref):
    d = D_MODEL
    p = jnp.dot(x_ref[...].astype(BF16), w_ref[...], preferred_element_type=F32) + b_ref[...]
    q_ref[...] = (p[:, :d] * (NA_HEAD_DIM ** -0.5 * LOG2E)).astype(BF16)
    k_ref[...] = p[:, d:2 * d].astype(BF16)
    v_ref[...] = p[:, 2 * d:].astype(BF16)


def _na_attn_kernel(q_ref, k_ref, vt_ref, bias_ref, o_ref):
    for i in range(NA_PAIRS):
        rp = pl.program_id(2) * NA_PAIRS + i
        o_ref[i * LANES:(i + 1) * LANES] = _na_pair(rp, q_ref[i * LANES:(i + 1) * LANES], k_ref, vt_ref, bias_ref)


def _na_pair(rp, q, k_ref, vt_ref, bias_ref):
    hpg = LANES // NA_HEAD_DIM
    n_rows = GRID_W
    r0 = 2 * rp
    rs0 = jnp.clip(r0 - NA_WIN_ROWS // 2, 0, n_rows - NA_WIN_ROWS)
    rs1 = jnp.clip(r0 + 1 - NA_WIN_ROWS // 2, 0, n_rows - NA_WIN_ROWS)
    start = jnp.minimum(rs0, n_rows - NA_KEY_ROWS)
    tok0 = pl.multiple_of(start * GRID_W, LANES)
    nkeys = NA_KEY_ROWS * GRID_W
    kw = k_ref[pl.ds(tok0, nkeys), :]
    vtw = vt_ref[0, :, pl.ds(tok0, nkeys)]
    lane = lax.broadcasted_iota(I32, (1, LANES), 1)
    sub = lax.broadcasted_iota(I32, (LANES, 1), 0)
    right = lane >= GRID_W
    acc = jnp.zeros((LANES, LANES), F32)
    for h in range(hpg):
        in_head = (lane >= h * NA_HEAD_DIM) & (lane < (h + 1) * NA_HEAD_DIM)
        qh = jnp.where(in_head, q, jnp.zeros_like(q))
        s = lax.dot_general(kw, qh, NT_DIMS, preferred_element_type=F32)
        tiles = []
        for a in range(NA_KEY_ROWS):
            kr = start + a
            dr0 = kr - r0 + (NA_WIN_ROWS - 1)
            bt = bias_ref[h, jnp.clip(dr0, 0, 2 * NA_WIN_ROWS - 1)]
            if a < 2 or a >= NA_WIN_ROWS:
                pen0 = jnp.where((kr >= rs0) & (kr < rs0 + NA_WIN_ROWS), 0.0, NEG)
                pen1 = jnp.where((kr >= rs1) & (kr < rs1 + NA_WIN_ROWS), 0.0, NEG)
                bt = bt + jnp.where(right, pen1, pen0)
            tiles.append(bt)
        s = s + jnp.concatenate(tiles, axis=0)
        m = jnp.max(s, axis=0, keepdims=True)
        p = jnp.exp2(s - m)
        inv = 1.0 / jnp.sum(p, axis=0, keepdims=True)
        in_head_rows = (sub >= h * NA_HEAD_DIM) & (sub < (h + 1) * NA_HEAD_DIM)
        pv = jnp.dot(vtw, p.astype(BF16), preferred_element_type=F32)
        acc = acc + jnp.where(in_head_rows, pv * inv, 0.0)
    return acc.T.astype(BF16)


def _na_bias_table(rpb):
    nh = rpb.shape[0]
    kc = jnp.arange(GRID_W)[:, None]
    qc = jnp.arange(GRID_W)[None, :]
    cs = jnp.clip(qc - NA_WIN_COLS // 2, 0, GRID_W - NA_WIN_COLS)
    inwin = (kc >= cs) & (kc < cs + NA_WIN_COLS)
    dc = jnp.clip(kc - qc + (NA_WIN_COLS - 1), 0, 2 * NA_WIN_COLS - 2)
    colb = jnp.where(inwin[None, None], rpb.astype(F32)[:, :, dc] * LOG2E, NEG)
    pad = jnp.full((nh, 1, GRID_W, GRID_W), NEG, F32)
    left = jnp.concatenate([colb, pad], axis=1)
    rgt = jnp.concatenate([pad, colb], axis=1)
    return jnp.concatenate([left, rgt], axis=-1)


def _na_mixer_ln(x, batch, w_in, b_in, rpb, w_out, b_out, ln_g, ln_b):
    n, d = x.shape
    seq = n // batch
    assert seq == GRID_W * GRID_W
    row = lambda i: (i, 0)
    fix = lambda i: (0, 0)
    qkv_shape = jax.ShapeDtypeStruct((n, d), BF16)
    q, k, v = pl.pallas_call(
        _na_in_kernel,
        grid=(n // ROW_TILE,),
        in_specs=[pl.BlockSpec((ROW_TILE, d), row), pl.BlockSpec((d, 3 * d), fix), pl.BlockSpec((1, 3 * d), fix)],
        out_specs=[pl.BlockSpec((ROW_TILE, d), row)] * 3,
        out_shape=[qkv_shape] * 3,
        compiler_params=_cparams(("parallel",)),
        name="na_in",
    )(x, w_in.astype(BF16), _row2(b_in))
    vt = jnp.swapaxes(v.reshape(batch, seq, d), 1, 2)
    bias = _na_bias_table(rpb)
    hpg = LANES // NA_HEAD_DIM
    n_pairs = GRID_W // 2
    n_steps = n_pairs // NA_PAIRS
    o = pl.pallas_call(
        _na_attn_kernel,
        grid=(batch, d // LANES, n_steps),
        in_specs=[pl.BlockSpec((NA_PAIRS * LANES, LANES), lambda b, g, r: (b * n_steps + r, g)),
                  pl.BlockSpec((seq, LANES), lambda b, g, r: (b, g)),
                  pl.BlockSpec((1, LANES, seq), lambda b, g, r: (b, g, 0)),
                  pl.BlockSpec((hpg, 2 * NA_WIN_ROWS, GRID_W, LANES), lambda b, g, r: (g, 0, 0, 0))],
        out_specs=pl.BlockSpec((NA_PAIRS * LANES, LANES), lambda b, g, r: (b * n_steps + r, g)),
        out_shape=qkv_shape,
        compiler_params=_cparams(("parallel", "parallel", "arbitrary")),
        name="na_attn",
    )(q, k, vt, bias)
    return _proj_ln(o, w_out.astype(BF16), b_out, x, ln_g, ln_b)


def _moe_route_kernel(x_ref, wr_ref, rb_ref, xs_ref, pos_ref, gate_ref, cnt_ref):
    ne, tb = N_EXPERTS, MOE_TB
    xb, x_lo = _split_bf16(x_ref[...])
    nt = lambda p, q: lax.dot_general(p, q, NT_DIMS, preferred_element_type=F32)
    logits = nt(wr_ref[0], xb) + nt(wr_ref[1], xb) + nt(wr_ref[0], x_lo) + rb_ref[...]
    eio = lax.broadcasted_iota(I32, (ne, tb), 0)
    work = logits
    vals, hots = [], []
    for _ in range(TOP_K):
        m = jnp.max(work, axis=0, keepdims=True)
        idx = jnp.min(jnp.where(work == m, eio, ne), axis=0, keepdims=True)
        hot = eio == idx
        vals.append(m)
        hots.append(hot)
        work = jnp.where(hot, -jnp.inf, work)
    exps = [jnp.exp(v - vals[0]) for v in vals]
    denom = exps[0] + exps[1] + exps[2] + exps[3]
    sel = jnp.zeros((ne, tb), F32)
    for hot in hots:
        sel = sel + jnp.where(hot, 1.0, 0.0)
    earlier = (lax.broadcasted_iota(I32, (tb, tb), 0) < lax.broadcasted_iota(I32, (tb, tb), 1))
    rank = jnp.dot(sel.astype(BF16), jnp.where(earlier, 1.0, 0.0).astype(BF16),
                   preferred_element_type=F32)
    cnt = jnp.sum(sel, axis=1, keepdims=True)
    cnt_pad = jnp.ceil(cnt * (1.0 / MOE_G)) * MOE_G
    below = (lax.broadcasted_iota(I32, (ne, ne), 1) < lax.broadcasted_iota(I32, (ne, ne), 0))
    seg_off = jnp.dot(jnp.where(below, 1.0, 0.0), jnp.broadcast_to(cnt_pad, (ne, tb)),
                      precision=HIGHEST, preferred_element_type=F32)
    slot = seg_off + rank
    pos = [jnp.sum(jnp.where(hot, slot, 0.0), axis=0, keepdims=True).astype(I32) for hot in hots]
    zero_row = jnp.zeros((1, tb), I32)
    pos_ref[0] = jnp.concatenate(pos + [zero_row] * (8 - TOP_K), axis=0)
    gate_ref[0] = jnp.concatenate([e / denom for e in exps] + [zero_row.astype(F32)] * (8 - TOP_K), axis=0)
    cnt_ref[0] = jnp.broadcast_to(cnt_pad, (ne, LANES)).astype(I32)
    local = lax.broadcasted_iota(I32, (MOE_PCH, tb), 0).astype(F32).astype(BF16)
    one, zero = jnp.ones((), BF16), jnp.zeros((), BF16)
    for r in range(MOE_RB // MOE_PCH):
        onehot = None
        for p in pos:
            hit = jnp.where(local == (p - r * MOE_PCH).astype(F32).astype(BF16), one, zero)
            onehot = hit if onehot is None else onehot + hit
        xs_ref[r * MOE_PCH:(r + 1) * MOE_PCH] = jnp.dot(
            onehot, xb, preferred_element_type=F32).astype(BF16)


def _moe_expert_kernel(te_ref, src_ref, dst_ref, nt_ref, xs_hbm, wgu_ref, bgu_ref, wd_ref, bd_ref,
                       ys_hbm, xbuf, ybuf, wgu_bf, wd_bf, sem_in, sem_out):
    t = pl.program_id(0)
    nt = nt_ref[0]
    s = t % 2

    def in_copy(chunk, slot, c):
        return pltpu.make_async_copy(
            xs_hbm.at[pl.ds(pl.multiple_of(chunk * MOE_G, MOE_G), MOE_G)],
            xbuf.at[slot, pl.ds(c * MOE_G, MOE_G)], sem_in.at[slot])

    def out_copy(chunk, slot, c):
        return pltpu.make_async_copy(
            ybuf.at[slot, pl.ds(c * MOE_G, MOE_G)],
            ys_hbm.at[pl.ds(pl.multiple_of(chunk * MOE_G, MOE_G), MOE_G)], sem_out.at[slot])

    def tile_chunks(table_ref, tile):
        base = tile * MOE_CPT
        return [table_ref[base + c] for c in range(MOE_CPT)]

    def start_in(tile, slot):
        for c, chunk in enumerate(tile_chunks(src_ref, tile)):
            in_copy(chunk, slot, c).start()

    def start_out(tile, slot):
        for c, chunk in enumerate(tile_chunks(dst_ref, tile)):
            out_copy(chunk, slot, c).start()

    def wait_in(slot):
        for c in range(MOE_CPT):
            in_copy(0, slot, c).wait()

    def wait_out(slot):
        for c in range(MOE_CPT):
            out_copy(0, slot, c).wait()

    @pl.when(t == 0)
    def _():
        start_in(0, 0)

    @pl.when(t + 1 < nt)
    def _():
        start_in(t + 1, 1 - s)

    @pl.when(t < nt)
    def _():
        @pl.when(jnp.logical_or(t == 0, te_ref[t] != te_ref[jnp.maximum(t - 1, 0)]))
        def _():
            wgu_bf[...] = wgu_ref[0, 0].astype(BF16)
            wd_bf[...] = wd_ref[0, 0].astype(BF16)

        wait_in(s)
        f = wd_bf.shape[0]
        h = jnp.dot(xbuf[s], wgu_bf[...], preferred_element_type=F32) + bgu_ref[0, 0]
        glu = jnp.minimum(h[:, :f], SWIGLU_LIMIT)
        lin = jnp.clip(h[:, f:], -SWIGLU_LIMIT, SWIGLU_LIMIT)
        act = (glu * jax.nn.sigmoid(SWIGLU_ALPHA * glu) * (lin + 1.0)).astype(BF16)
        y = jnp.dot(act, wd_bf[...], preferred_element_type=F32) + bd_ref[0, 0]

        @pl.when(t >= 2)
        def _():
            wait_out(s)

        ybuf[s] = y.astype(BF16)
        start_out(t, s)

        @pl.when(t == nt - 1)
        def _():
            wait_out(s)

            @pl.when(t >= 1)
            def _():
                wait_out(1 - s)


def _moe_combine_kernel(ys_ref, pos_ref, gate_ref, x_ref, g_ref, b_ref, o_ref):
    tb = MOE_TB
    pos = pos_ref[...]
    gate = gate_ref[...].astype(BF16)
    local = lax.broadcasted_iota(I32, (tb, MOE_PCH), 1).astype(F32).astype(BF16)
    zero = jnp.zeros((), BF16)
    f = jnp.zeros((tb, D_MODEL), F32)
    for r in range(MOE_RB // MOE_PCH):
        w = None
        for k in range(TOP_K):
            pk = (pos[:, k:k + 1] - r * MOE_PCH).astype(F32).astype(BF16)
            wk = jnp.where(local == pk, gate[:, k:k + 1], zero)
            w = wk if w is None else w + wk
        f = f + jnp.dot(w, ys_ref[r * MOE_PCH:(r + 1) * MOE_PCH], preferred_element_type=F32)
    o_ref[...] = _ln_rows(ALPHA * x_ref[...] + f, g_ref[...], b_ref[...])


def _moe_tile_tables(cnt_pad, n_tiles_max):
    nb, ne = cnt_pad.shape
    nch_eb = (cnt_pad // MOE_G).T
    seg_off = jnp.cumsum(cnt_pad, axis=1) - cnt_pad
    base_eb = ((jnp.arange(nb, dtype=I32)[:, None] * MOE_RB + seg_off) // MOE_G).T
    first_eb = jnp.cumsum(nch_eb, axis=1) - nch_eb
    tiles_e = (jnp.sum(nch_eb, axis=1) + MOE_CPT - 1) // MOE_CPT
    tile_end = jnp.cumsum(tiles_e)
    n_tiles = tile_end[-1]
    tau = jnp.arange(n_tiles_max, dtype=I32)
    tile_e = jnp.minimum(jnp.sum(tau[:, None] >= tile_end[None, :], axis=1), ne - 1).astype(I32)
    of_tile = tile_e[:, None] == jnp.arange(ne, dtype=I32)[None, :]
    pick = lambda tab: jnp.sum(jnp.where(of_tile[:, :, None], tab[None], 0), axis=1)
    tile_first = tau - jnp.sum(jnp.where(of_tile, (tile_end - tiles_e)[None, :], 0), axis=1)
    k = (tile_first * MOE_CPT)[:, None] + jnp.arange(MOE_CPT, dtype=I32)[None, :]
    first, count, base = pick(first_eb), pick(nch_eb), pick(base_eb)
    inside = (k[:, :, None] >= first[:, None, :]) & (k[:, :, None] < (first + count)[:, None, :])
    src = jnp.sum(jnp.where(inside, (base - first)[:, None, :] + k[:, :, None], 0), axis=2)
    occupied = jnp.any(inside, axis=2) & (tau < n_tiles)[:, None]
    spare_block = ((tau % 2)[:, None] * MOE_CPT + jnp.arange(MOE_CPT, dtype=I32)[None, :]) % nb
    dst = jnp.where(occupied, src, spare_block * MOE_BLOCK_CHUNKS + MOE_SPARE_CHUNK)
    src = jnp.where(occupied, src, MOE_ZERO_CHUNK)
    return (tile_e, src.reshape(-1).astype(I32), dst.reshape(-1).astype(I32),
            n_tiles.astype(I32).reshape(1))


def _moe_ln(x, layer, router_w, router_b, w_gu, b_gu, w_down, b_down, ln_g, ln_b):
    n, d = x.shape
    ne, f = N_EXPERTS, w_down.shape[2]
    depth = w_gu.shape[0]
    expert_block = lambda t, te, sr, ds, nt: (layer, te[t], 0, 0)
    nb = n // MOE_TB
    assert MOE_RB >= MOE_TB * TOP_K + ne * (MOE_G - 1) + 2 * MOE_G
    blk = lambda i: (i, 0)
    blk3 = lambda i: (i, 0, 0)
    fix = lambda i: (0, 0)
    slot_rows = nb * MOE_RB
    xs, pos, gates, cnt = pl.pallas_call(
        _moe_route_kernel,
        grid=(nb,),
        in_specs=[pl.BlockSpec((MOE_TB, d), blk), pl.BlockSpec((2, ne, d), lambda i: (0, 0, 0)),
                  pl.BlockSpec((ne, 1), fix)],
        out_specs=[pl.BlockSpec((MOE_RB, d), blk), pl.BlockSpec((1, 8, MOE_TB), blk3),
                   pl.BlockSpec((1, 8, MOE_TB), blk3), pl.BlockSpec((1, ne, LANES), blk3)],
        out_shape=[jax.ShapeDtypeStruct((slot_rows, d), BF16), jax.ShapeDtypeStruct((nb, 8, MOE_TB), I32),
                   jax.ShapeDtypeStruct((nb, 8, MOE_TB), F32), jax.ShapeDtypeStruct((nb, ne, LANES), I32)],
        compiler_params=_cparams(("parallel",)),
        name="moe_route",
    )(x, jnp.stack(_split_bf16(router_w.T)), router_b.reshape(ne, 1))

    n_tiles_max = nb * MOE_RB // MOE_TM + ne
    tile_e, src, dst, n_tiles = _moe_tile_tables(cnt[:, :, 0], n_tiles_max)
    ys = pl.pallas_call(
        _moe_expert_kernel,
        grid_spec=pltpu.PrefetchScalarGridSpec(
            num_scalar_prefetch=4,
            grid=(n_tiles_max,),
            in_specs=[pl.BlockSpec(memory_space=pl.ANY),
                      pl.BlockSpec((1, 1, d, 2 * f), expert_block),
                      pl.BlockSpec((1, 1, 1, 2 * f), expert_block),
                      pl.BlockSpec((1, 1, f, d), expert_block),
                      pl.BlockSpec((1, 1, 1, d), expert_block)],
            out_specs=pl.BlockSpec(memory_space=pl.ANY),
            scratch_shapes=[pltpu.VMEM((2, MOE_TM, d), BF16), pltpu.VMEM((2, MOE_TM, d), BF16),
                            pltpu.VMEM((d, 2 * f), BF16), pltpu.VMEM((f, d), BF16),
                            pltpu.SemaphoreType.DMA((2,)), pltpu.SemaphoreType.DMA((2,))]),
        out_shape=jax.ShapeDtypeStruct((slot_rows, d), BF16),
        input_output_aliases={4: 0},
        compiler_params=pltpu.CompilerParams(dimension_semantics=("arbitrary",),
                                             vmem_limit_bytes=MOE_EXPERT_VMEM),
        name="moe_expert",
    )(tile_e, src, dst, n_tiles, xs, w_gu, b_gu.reshape(depth, ne, 1, 2 * f), w_down,
      b_down.reshape(depth, ne, 1, d))

    pos_t = jnp.swapaxes(pos, 1, 2).reshape(n, 8)
    gate_t = jnp.swapaxes(gates, 1, 2).reshape(n, 8)
    return pl.pallas_call(
        _moe_combine_kernel,
        grid=(nb,),
        in_specs=[pl.BlockSpec((MOE_RB, d), blk), pl.BlockSpec((MOE_TB, 8), blk), pl.BlockSpec((MOE_TB, 8), blk),
                  pl.BlockSpec((MOE_TB, d), blk), pl.BlockSpec((1, d), fix), pl.BlockSpec((1, d), fix)],
        out_specs=pl.BlockSpec((MOE_TB, d), blk),
        out_shape=jax.ShapeDtypeStruct((n, d), F32),
        compiler_params=_cparams(("parallel",)),
        name="moe_combine",
    )(ys, pos_t, gate_t, x, _row2(ln_g), _row2(ln_b))


def kernel(x, ln1_g, ln1_b, ln2_g, ln2_b, gla_w_in, gla_gate_w2, gla_gate_b, gla_norm_g, gla_w_out,
           hy_w_in, hy_b_in, hy_conv_w, hy_conv_b, hy_ffn_w1, hy_ffn_b1, hy_sin_freq, hy_ffn_w2,
           hy_ffn_b2, hy_ffn_w3, hy_filt_bias, hy_w_out, hy_b_out, na_w_in, na_b_in, na_rpb, na_w_out,
           na_b_out, moe_router_w, moe_router_b, moe_w_gu, moe_b_gu, moe_w_down, moe_b_down):
    batch, seq, d = x.shape
    xt = x.reshape(batch * seq, d)
    for i in range(DEPTH):
        m, j = i % N_MIXERS, i // N_MIXERS
        if m == 0:
            xt = _gla_mixer_ln(xt, batch, gla_w_in[j], gla_gate_w2[j], gla_gate_b[j], gla_norm_g[j],
                               gla_w_out[j], ln1_g[i], ln1_b[i])
        elif m == 1:
            xt = _hyena_mixer_ln(xt, batch, hy_w_in[j], hy_b_in[j], hy_conv_w[j], hy_conv_b[j],
                                 hy_ffn_w1[j], hy_ffn_b1[j], hy_sin_freq[j], hy_ffn_w2[j], hy_ffn_b2[j],
                                 hy_ffn_w3[j], hy_filt_bias[j], hy_w_out[j], hy_b_out[j],
                                 ln1_g[i], ln1_b[i])
        else:
            xt = _na_mixer_ln(xt, batch, na_w_in[j], na_b_in[j], na_rpb[j], na_w_out[j], na_b_out[j],
                              ln1_g[i], ln1_b[i])
        xt = _moe_ln(xt, i, moe_router_w[i], moe_router_b[i], moe_w_gu, moe_b_gu, moe_w_down,
                     moe_b_down, ln2_g[i], ln2_b[i])
    return xt.reshape(batch, seq, d)
```

```python
import functools
import math

import jax
import jax.numpy as jnp
from jax import lax
from jax.experimental import pallas as pl
from jax.experimental.pallas import tpu as pltpu

F32, BF16, I32 = jnp.float32, jnp.bfloat16, jnp.int32
HIGHEST = lax.Precision.HIGHEST
NT_DIMS = (((1,), (1,)), ((), ()))
TN_DIMS = (((0,), (0,)), ((), ()))

D_MODEL = 1024
DEPTH = 4
N_MIXERS = 3
GRID_W = 64
ALPHA = (2 * DEPTH) ** 0.25
LN_EPS = 1e-5

GLA_HEADS = 4
GLA_KEY_DIM = 512
GLA_VAL_DIM = 1024
GLA_HEAD_K = 128
GLA_HEAD_V = 256
GLA_GATE_RANK = 16
GLA_TAU = 16.0
GLA_NORM_EPS = 1e-6

HYENA_ORDER = 2
HYENA_POS_EMB = 33
HYENA_FILTER_HIDDEN = 64
HYENA_DECAY_TARGET = 1e-2
HYENA_FAST_DECAY = 0.3
HYENA_SLOW_DECAY = 1.5
HYENA_WINDOW_SHIFT = 0.05

NA_HEAD_DIM = 32
NA_HEADS = 32
NA_WIN_ROWS = 8
NA_WIN_COLS = 16

N_EXPERTS = 32
TOP_K = 4
SWIGLU_ALPHA = 1.702
SWIGLU_LIMIT = 7.0

LANES = 128
BF16_SUBLANES = 16
VMEM_LIMIT = 48 * 1024 * 1024
MOE_EXPERT_VMEM = 56 * 1024 * 1024

ROW_TILE = 512
LN_ROW_TILE = 1024
GLA_CHUNK = 128
GLA_SUB = 32
GLA_ROWS = 256
GLA_HEADS_PER_STEP = 4
GLA_EXP2_CLAMP = 115.0
DFT_TILE = 512
DFT_COLS = 512
DFT_SUBCOLS = 256
NA_KEY_ROWS = 10
NA_PAIRS = 16
NEG = -1e30
LOG2E = math.log2(math.e)

MOE_TB = 512
MOE_G = BF16_SUBLANES
MOE_RB = MOE_TB * TOP_K + 512
MOE_PCH = 256
MOE_BLOCK_CHUNKS = MOE_RB // MOE_G
MOE_ZERO_CHUNK = MOE_BLOCK_CHUNKS - 1
MOE_SPARE_CHUNK = MOE_BLOCK_CHUNKS - 2
MOE_TM = 512
MOE_CPT = MOE_TM // MOE_G


def _cparams(sem):
    return pltpu.CompilerParams(dimension_semantics=sem, vmem_limit_bytes=VMEM_LIMIT)


def _ln_rows(y, g, b):
    mu = jnp.mean(y, axis=-1, keepdims=True)
    yc = y - mu
    var = jnp.mean(yc * yc, axis=-1, keepdims=True)
    return yc * lax.rsqrt(var + LN_EPS) * g + b


def _row2(v):
    return v.reshape(1, -1)


def _split_bf16(a):
    hi = a.astype(BF16)
    return hi, (a - hi.astype(F32)).astype(BF16)


def _dot_split(a, b_hi, b_lo, dims):
    a_hi, a_lo = _split_bf16(a)
    if dims is None:
        mm = lambda p, q: jnp.dot(p, q, preferred_element_type=F32)
    else:
        mm = lambda p, q: lax.dot_general(p, q, dims, preferred_element_type=F32)
    return mm(a_hi, b_hi) + mm(a_lo, b_hi) + mm(a_hi, b_lo)


def _proj_ln_kernel(a_ref, w_ref, bias_ref, x_ref, g_ref, b_ref, o_ref):
    h = jnp.dot(a_ref[...], w_ref[...], preferred_element_type=F32) + bias_ref[...]
    o_ref[...] = _ln_rows(ALPHA * x_ref[...] + h, g_ref[...], b_ref[...])


def _proj_ln(a, w, bias, x, g, b):
    n, d = x.shape
    row = lambda i: (i, 0)
    fix = lambda i: (0, 0)
    return pl.pallas_call(
        _proj_ln_kernel,
        grid=(n // LN_ROW_TILE,),
        in_specs=[pl.BlockSpec((LN_ROW_TILE, a.shape[1]), row), pl.BlockSpec(w.shape, fix),
                  pl.BlockSpec((1, d), fix), pl.BlockSpec((LN_ROW_TILE, d), row),
                  pl.BlockSpec((1, d), fix), pl.BlockSpec((1, d), fix)],
        out_specs=pl.BlockSpec((LN_ROW_TILE, d), row),
        out_shape=jax.ShapeDtypeStruct((n, d), F32),
        compiler_params=_cparams(("parallel",)),
        name="proj_ln",
    )(a, w, _row2(bias), x, _row2(g), _row2(b))


def _gla_in_kernel(x_ref, w_ref, w2_ref, gb_ref, q_ref, k_ref, v_ref, g_ref, la_ref):
    kd, vd = GLA_KEY_DIM, GLA_VAL_DIM
    xb = x_ref[...].astype(BF16)
    proj = lambda lo, hi: jnp.dot(xb, w_ref[:, lo:hi], preferred_element_type=F32)
    glow = proj(2 * kd + 2 * vd, 2 * kd + 2 * vd + LANES)
    z = _dot_split(glow, w2_ref[0], w2_ref[1], None) + gb_ref[...]
    la_ref[...] = (jnp.minimum(z, 0.0) - jnp.log(1.0 + jnp.exp(-jnp.abs(z)))) * (LOG2E / GLA_TAU)
    og = proj(2 * kd + vd, 2 * kd + 2 * vd)
    g_ref[...] = (og * jax.nn.sigmoid(og)).astype(BF16)
    q_ref[...] = (proj(0, kd) * GLA_HEAD_K ** -0.5).astype(BF16)
    k_ref[...] = proj(kd, 2 * kd).astype(BF16)
    v_ref[...] = proj(2 * kd, 2 * kd + vd).astype(BF16)


def _gla_in(x, w_in, gate_w2, gate_b):
    n, d = x.shape
    kd, vd, r = GLA_KEY_DIM, GLA_VAL_DIM, GLA_GATE_RANK
    width = 2 * kd + 2 * vd
    w = jnp.pad(w_in, ((0, 0), (0, LANES - 2 * r))).astype(BF16)
    w2 = jnp.zeros((LANES, 2 * kd), F32)
    w2 = w2.at[:r, :kd].set(gate_w2[0]).at[r:2 * r, kd:].set(gate_w2[1])
    w2 = jnp.stack(_split_bf16(w2))
    gb = gate_b.reshape(1, 2 * kd)
    row = lambda i: (i, 0)
    fix = lambda i: (0, 0)
    return pl.pallas_call(
        _gla_in_kernel,
        grid=(n // ROW_TILE,),
        in_specs=[pl.BlockSpec((ROW_TILE, d), row), pl.BlockSpec((d, width + LANES), fix),
                  pl.BlockSpec((2, LANES, 2 * kd), lambda i: (0, 0, 0)), pl.BlockSpec((1, 2 * kd), fix)],
        out_specs=[pl.BlockSpec((ROW_TILE, kd), row), pl.BlockSpec((ROW_TILE, kd), row),
                   pl.BlockSpec((ROW_TILE, vd), row), pl.BlockSpec((ROW_TILE, vd), row),
                   pl.BlockSpec((ROW_TILE, 2 * kd), row)],
        out_shape=[jax.ShapeDtypeStruct((n, kd), BF16), jax.ShapeDtypeStruct((n, kd), BF16),
                   jax.ShapeDtypeStruct((n, vd), BF16), jax.ShapeDtypeStruct((n, vd), BF16),
                   jax.ShapeDtypeStruct((n, 2 * kd), F32)],
        compiler_params=_cparams(("parallel",)),
        name="gla_in",
    )(x, w, w2, gb)


def _gla_direction(q_ref, k_ref, v_ref, la_ref, st_ref, o_ref, reverse):
    rows, cr = q_ref.shape[0], GLA_CHUNK
    nch = rows // cr
    shift = cr.bit_length() - 1
    ri = lax.broadcasted_iota(I32, (rows, rows), 0)
    ci = lax.broadcasted_iota(I32, (rows, rows), 1)
    same_chunk = jnp.right_shift(ri, shift) == jnp.right_shift(ci, shift)
    within = (ci >= ri) if reverse else (ci <= ri)
    cum_mat = jnp.where(same_chunk & within, 1.0, 0.0).astype(BF16)
    la_all = la_ref[...]
    la_hi = la_all.astype(BF16)
    rest = la_all - la_hi.astype(F32)
    la_mid = rest.astype(BF16)
    la_lo = (rest - la_mid.astype(F32)).astype(BF16)
    c_heads = (jnp.dot(cum_mat, la_hi, preferred_element_type=F32)
               + jnp.dot(cum_mat, la_mid, preferred_element_type=F32)
               + jnp.dot(cum_mat, la_lo, preferred_element_type=F32))
    rj = lax.broadcasted_iota(I32, (cr, cr), 0)
    cj = lax.broadcasted_iota(I32, (cr, cr), 1)
    mask = (cj > rj) if reverse else (cj <= rj)

    for head in range(st_ref.shape[0]):
        kcols = slice(head * GLA_HEAD_K, (head + 1) * GLA_HEAD_K)
        vcols = slice(head * GLA_HEAD_V, (head + 1) * GLA_HEAD_V)
        la, c_all = la_all[:, kcols], c_heads[:, kcols]
        qes, decs, intra, upd = [], [], [], []
        for ch in range(nch):
            lo = ch * cr
            c = c_all[lo:lo + cr]
            qf = q_ref[lo:lo + cr, kcols].astype(F32)
            kf = k_ref[lo:lo + cr, kcols].astype(F32)
            v = v_ref[lo:lo + cr, vcols]
            ctot = c[0:1] if reverse else c[cr - 1:cr]
            qes.append((qf * jnp.exp2(c)).astype(BF16))
            decs.append(jnp.exp2(ctot))
            blocks = []
            for i in range(cr // GLA_SUB):
                blo, bhi = i * GLA_SUB, (i + 1) * GLA_SUB
                edge = bhi - 1 if reverse else blo
                ref = c[edge:edge + 1] - la[lo + edge:lo + edge + 1]
                qi = (qf[blo:bhi] * jnp.exp2(c[blo:bhi] - ref)).astype(BF16)
                ki = (kf * jnp.exp2(jnp.minimum(ref - c, GLA_EXP2_CLAMP))).astype(BF16)
                blocks.append(lax.dot_general(qi, ki, NT_DIMS, preferred_element_type=F32))
            scores = jnp.where(mask, jnp.concatenate(blocks, axis=0), 0.0).astype(BF16)
            intra.append(jnp.dot(scores, v, preferred_element_type=F32))
            kd = (kf * jnp.exp2(ctot - c)).astype(BF16)
            upd.append(lax.dot_general(v, kd, TN_DIMS, preferred_element_type=F32))

        st = st_ref[head]
        states = [None] * nch
        for ch in (reversed(range(nch)) if reverse else range(nch)):
            states[ch] = st
            st = st * decs[ch] + upd[ch]
        st_ref[head] = st
        for ch in range(nch):
            inter = lax.dot_general(qes[ch], states[ch].astype(BF16), NT_DIMS, preferred_element_type=F32)
            o_ref[ch * cr:(ch + 1) * cr, vcols] = (intra[ch] + inter).astype(BF16)


def _gla_scan_kernel(qf_ref, kf_ref, vf_ref, laf_ref, qb_ref, kb_ref, vb_ref, lab_ref,
                     of_ref, ob_ref, sf_ref, sb_ref):
    @pl.when(pl.program_id(2) == 0)
    def _():
        sf_ref[...] = jnp.zeros_like(sf_ref)
        sb_ref[...] = jnp.zeros_like(sb_ref)

    _gla_direction(qf_ref, kf_ref, vf_ref, laf_ref, sf_ref, of_ref, False)
    _gla_direction(qb_ref, kb_ref, vb_ref, lab_ref, sb_ref, ob_ref, True)


def _gla_scan(q, k, v, la, batch):
    n = q.shape[0]
    nblk = n // batch // GLA_ROWS
    hps = GLA_HEADS_PER_STEP
    hk, hv, ngrp = hps * GLA_HEAD_K, hps * GLA_HEAD_V, GLA_HEADS // hps
    fwd = lambda b, h, i: (b * nblk + i, h)
    bwd = lambda b, h, i: (b * nblk + nblk - 1 - i, h)
    bwd_la = lambda b, h, i: (b * nblk + nblk - 1 - i, ngrp + h)
    state = pltpu.VMEM((hps, GLA_HEAD_V, GLA_HEAD_K), F32)
    return pl.pallas_call(
        _gla_scan_kernel,
        grid=(batch, ngrp, nblk),
        in_specs=[pl.BlockSpec((GLA_ROWS, hk), fwd), pl.BlockSpec((GLA_ROWS, hk), fwd),
                  pl.BlockSpec((GLA_ROWS, hv), fwd), pl.BlockSpec((GLA_ROWS, hk), fwd),
                  pl.BlockSpec((GLA_ROWS, hk), bwd), pl.BlockSpec((GLA_ROWS, hk), bwd),
                  pl.BlockSpec((GLA_ROWS, hv), bwd), pl.BlockSpec((GLA_ROWS, hk), bwd_la)],
        out_specs=[pl.BlockSpec((GLA_ROWS, hv), fwd), pl.BlockSpec((GLA_ROWS, hv), bwd)],
        out_shape=[jax.ShapeDtypeStruct((n, GLA_VAL_DIM), BF16)] * 2,
        scratch_shapes=[state, state],
        compiler_params=_cparams(("parallel", "parallel", "arbitrary")),
        name="gla_scan",
    )(q, k, v, la, q, k, v, la)


def _gla_out_kernel(of_ref, ob_ref, gate_ref, ng_ref, w_ref, x_ref, g_ref, b_ref, o_ref):
    o = of_ref[...].astype(F32) + ob_ref[...].astype(F32)
    parts = []
    for h in range(GLA_HEADS):
        oh = o[:, h * GLA_HEAD_V:(h + 1) * GLA_HEAD_V]
        ms = jnp.mean(oh * oh, axis=-1, keepdims=True)
        parts.append(oh * lax.rsqrt(ms + GLA_NORM_EPS))
    on = jnp.concatenate(parts, axis=-1) * ng_ref[...]
    a = (on * gate_ref[...].astype(F32)).astype(BF16)
    h = jnp.dot(a, w_ref[...], preferred_element_type=F32)
    o_ref[...] = _ln_rows(ALPHA * x_ref[...] + h, g_ref[...], b_ref[...])


def _gla_mixer_ln(x, batch, w_in, gate_w2, gate_b, norm_g, w_out, ln_g, ln_b):
    n, d = x.shape
    q, k, v, gate, la = _gla_in(x, w_in, gate_w2, gate_b)
    o_f, o_b = _gla_scan(q, k, v, la, batch)
    row = lambda i: (i, 0)
    fix = lambda i: (0, 0)
    vd = GLA_VAL_DIM
    return pl.pallas_call(
        _gla_out_kernel,
        grid=(n // LN_ROW_TILE,),
        in_specs=[pl.BlockSpec((LN_ROW_TILE, vd), row), pl.BlockSpec((LN_ROW_TILE, vd), row),
                  pl.BlockSpec((LN_ROW_TILE, vd), row), pl.BlockSpec((1, vd), fix),
                  pl.BlockSpec((vd, d), fix), pl.BlockSpec((LN_ROW_TILE, d), row),
                  pl.BlockSpec((1, d), fix), pl.BlockSpec((1, d), fix)],
        out_specs=pl.BlockSpec((LN_ROW_TILE, d), row),
        out_shape=jax.ShapeDtypeStruct((n, d), F32),
        compiler_params=_cparams(("parallel",)),
        name="gla_out",
    )(o_f, o_b, gate, _row2(jnp.tile(norm_g, GLA_HEADS)), w_out.astype(BF16), x,
      _row2(ln_g), _row2(ln_b))


def _split_even_odd_kernel(x_ref, xe_ref, xo_ref, mix_ref):
    x = x_ref[...]
    half = x.shape[0] // 2
    nchunk = mix_ref.shape[0]
    for c in range(nchunk):
        mix_ref[c] = x[:, c * LANES:(c + 1) * LANES]
    for parity, o_ref in ((0, xe_ref), (1, xo_ref)):
        rows = pl.ds(parity, half, stride=2)
        o_ref[...] = jnp.concatenate([mix_ref[c, rows, :] for c in range(nchunk)], axis=1).astype(BF16)


def _split_even_odd(x):
    n, d = x.shape
    row = lambda i: (i, 0)
    half_shape = jax.ShapeDtypeStruct((n // 2, d), BF16)
    return pl.pallas_call(
        _split_even_odd_kernel,
        grid=(n // ROW_TILE,),
        in_specs=[pl.BlockSpec((ROW_TILE, d), row)],
        out_specs=[pl.BlockSpec((ROW_TILE // 2, d), row)] * 2,
        out_shape=[half_shape, half_shape],
        scratch_shapes=[pltpu.VMEM((d // LANES, ROW_TILE, LANES), F32)],
        compiler_params=_cparams(("parallel",)),
        name="split_even_odd",
    )(x)


def _hy_in_kernel(xe_ref, xo_ref, w_ref, b_ref, cw_ref, cb_ref, ue_ref, uo_ref):
    w, b = w_ref[...], b_ref[...]
    qe = jnp.dot(xe_ref[...], w, preferred_element_type=F32)
    qo = jnp.dot(xo_ref[...], w, preferred_element_type=F32)
    rows = qe.shape[0]
    cw = cw_ref[...]
    w0, w1, w2 = cw[0:1], cw[1:2], cw[2:3]
    const = cb_ref[...] + b * (w0 + w1 + w2)
    ue_ref[...] = (pltpu.roll(qo, 1, axis=0) * w0 + qe * w1 + qo * w2 + const).astype(BF16)
    uo_ref[...] = (qe * w0 + qo * w1 + pltpu.roll(qe, rows - 1, axis=0) * w2 + const).astype(BF16)
    g = BF16_SUBLANES
    ri = lax.broadcasted_iota(I32, (g, 1), 0)
    qe_s, qo_s = qe[0:g], qo[0:g]
    inner = qe_s * w1 + qo_s * w2 + const
    ue_ref[0:g] = jnp.where(ri == 0, inner - b * w0, inner + pltpu.roll(qo_s, 1, axis=0) * w0).astype(BF16)
    qe_s, qo_s = qe[rows - g:], qo[rows - g:]
    inner = qe_s * w0 + qo_s * w1 + const
    uo_ref[rows - g:] = jnp.where(ri == g - 1, inner - b * w2,
                                  inner + pltpu.roll(qe_s, g - 1, axis=0) * w2).astype(BF16)


def _hy_in(xe, xo, batch, w_in, b_in, conv_w, conv_b):
    n2, d = xe.shape
    half = n2 // batch
    width = w_in.shape[1]
    cols = 256
    seq_spec = pl.BlockSpec((half, d), lambda b, j: (b, 0))
    col_spec = lambda r: pl.BlockSpec((r, cols), lambda b, j: (0, j))
    out_spec = pl.BlockSpec((half, cols), lambda b, j: (b, j))
    return pl.pallas_call(
        _hy_in_kernel,
        grid=(batch, width // cols),
        in_specs=[seq_spec, seq_spec, pl.BlockSpec((d, cols), lambda b, j: (0, j)), col_spec(1), col_spec(3),
                  col_spec(1)],
        out_specs=[out_spec, out_spec],
        out_shape=[jax.ShapeDtypeStruct((n2, width), BF16)] * 2,
        compiler_params=_cparams(("parallel", "arbitrary")),
        name="hy_in",
    )(xe, xo, w_in.astype(BF16), _row2(b_in), conv_w, _row2(conv_b))


def _hy_filter_kernel(pe_ref, w1_ref, b1_ref, f_ref, w2_ref, b2_ref, w3_ref, win_ref, hs_ref, hd_ref):
    f = f_ref[...]
    h = jnp.sin(f[0:1] * (jnp.dot(pe_ref[...], w1_ref[...], precision=HIGHEST,
                                  preferred_element_type=F32) + b1_ref[...]))
    h = jnp.sin(f[1:2] * (jnp.dot(h, w2_ref[...], precision=HIGHEST,
                                  preferred_element_type=F32) + b2_ref[...]))
    h = jnp.dot(h, w3_ref[...], precision=HIGHEST, preferred_element_type=F32)
    win = win_ref[...]
    d = win.shape[1]
    rows = win.shape[0]
    t = pl.program_id(0) * rows + lax.broadcasted_iota(I32, (rows, 1), 0)
    for o in range(HYENA_ORDER):
        hf = h[:, 2 * o * d:(2 * o + 1) * d] * win
        hb = jnp.where(t == 0, 0.0, h[:, (2 * o + 1) * d:(2 * o + 2) * d] * win)
        hs_ref[:, o * d:(o + 1) * d] = (hf + hb).astype(BF16)
        hd_ref[:, o * d:(o + 1) * d] = (hf - hb).astype(BF16)


def _hy_spec_kernel(c_ref, s_ref, cmid_ref, smid_ref, hs_ref, hd_ref, har_ref, hai_ref, hbr_ref, hbi_ref):
    c, s = c_ref[...], s_ref[...]
    hs, hd = hs_ref[...], hd_ref[...]
    n = lax.broadcasted_iota(I32, (hs.shape[0], 1), 0)
    sgn = jnp.where(jnp.bitwise_and(n, 1) == 0, 1.0, -1.0).astype(BF16)
    ri = lax.broadcasted_iota(I32, (c.shape[0], 1), 0) + pl.program_id(0) * c.shape[0]
    mid_r = jnp.dot(cmid_ref[...], hs, preferred_element_type=F32)[0:1]
    mid_i = jnp.dot(smid_ref[...], hd, preferred_element_type=F32)[0:1]
    har_ref[...] = jnp.dot(c, hs, preferred_element_type=F32)
    hai_ref[...] = jnp.where(ri == 0, mid_r, jnp.dot(s, hd, preferred_element_type=F32))
    hbr_ref[...] = jnp.dot(c, hs * sgn, preferred_element_type=F32)
    hbi_ref[...] = jnp.where(ri == 0, mid_i, -jnp.dot(s, hd * sgn, preferred_element_type=F32))


def _dft_consts(seq):
    k = jnp.arange(seq, dtype=I32)
    step = LANES
    assert seq % step == 0
    phase = lambda t: ((k[:, None] * t[None, :]) % (2 * seq)).astype(F32) * (math.pi / seq)
    ang_hi = phase(jnp.arange(seq // step, dtype=I32) * step)[:, :, None]
    ang_lo = phase(jnp.arange(step, dtype=I32))[:, None, :]
    cos = (jnp.cos(ang_hi) * jnp.cos(ang_lo) - jnp.sin(ang_hi) * jnp.sin(ang_lo)).reshape(seq, seq)
    sin = (jnp.sin(ang_hi) * jnp.cos(ang_lo) + jnp.cos(ang_hi) * jnp.sin(ang_lo)).reshape(seq, seq)
    sgn = jnp.where(k % 2 == 0, 1.0, -1.0).astype(F32)
    fwd_i = jnp.where(k[:, None] == 0, sgn[None, :], -sin)
    return cos.astype(BF16), fwd_i.astype(BF16)


def _hy_filters(seq, consts, w1, b1, sin_freq, w2, b2, w3):
    cos, fwd_i = consts[0], consts[1]
    d = w3.shape[-1] // (2 * HYENA_ORDER)
    hid = HYENA_FILTER_HIDDEN
    t = jnp.arange(seq, dtype=F32)
    t_norm = t / max(seq - 1, 1)
    bands = (HYENA_POS_EMB - 1) // 2
    freqs = jnp.linspace(1e-4, bands - 1, bands, dtype=F32)
    ang = (2.0 * math.pi / seq) * t[:, None] * freqs[None, :]
    pe = jnp.concatenate([t_norm[:, None], jnp.cos(ang), -jnp.sin(ang)], axis=-1)
    pe = jnp.pad(pe, ((0, 0), (0, LANES - HYENA_POS_EMB)))
    min_decay = math.log(HYENA_DECAY_TARGET) / HYENA_SLOW_DECAY
    max_decay = math.log(HYENA_DECAY_TARGET) / HYENA_FAST_DECAY
    deltas = jnp.abs(jnp.linspace(min_decay, max_decay, d, dtype=F32))
    window = jnp.exp(-t_norm[:, None] * deltas[None, :]) + HYENA_WINDOW_SHIFT
    padc = LANES - hid
    w1p = jnp.pad(w1, ((0, LANES - HYENA_POS_EMB), (0, padc)))
    w2p = jnp.pad(w2, ((0, padc), (0, padc)))
    w3p = jnp.pad(w3, ((0, padc), (0, 0)))
    rows = 256
    fix = lambda i: (0, 0)
    row = lambda i: (i, 0)
    hs, hd = pl.pallas_call(
        _hy_filter_kernel,
        grid=(seq // rows,),
        in_specs=[pl.BlockSpec((rows, LANES), row), pl.BlockSpec((LANES, LANES), fix),
                  pl.BlockSpec((1, LANES), fix), pl.BlockSpec((2, LANES), fix),
                  pl.BlockSpec((LANES, LANES), fix), pl.BlockSpec((1, LANES), fix),
                  pl.BlockSpec((LANES, 2 * HYENA_ORDER * d), fix), pl.BlockSpec((rows, d), row)],
        out_specs=[pl.BlockSpec((rows, HYENA_ORDER * d), row)] * 2,
        out_shape=[jax.ShapeDtypeStruct((seq, HYENA_ORDER * d), BF16)] * 2,
        compiler_params=_cparams(("parallel",)),
        name="hy_filter",
    )(pe, w1p, _row2(jnp.pad(b1, (0, padc))), jnp.pad(sin_freq, ((0, 0), (0, padc))), w2p,
      _row2(jnp.pad(b2, (0, padc))), w3p, window)
    tf, tn = DFT_TILE, DFT_COLS
    half = seq // 2
    tn = DFT_SUBCOLS
    plane = jax.ShapeDtypeStruct((half, HYENA_ORDER * d), F32)
    mat = pl.BlockSpec((tf, seq), lambda i, j: (i, 0))
    mid = pl.BlockSpec((BF16_SUBLANES, seq), lambda i, j: (half // BF16_SUBLANES, 0))
    col = pl.BlockSpec((seq, tn), lambda i, j: (0, j))
    return pl.pallas_call(
        _hy_spec_kernel,
        grid=(half // tf, HYENA_ORDER * d // tn),
        in_specs=[mat, mat, mid, mid, col, col],
        out_specs=[pl.BlockSpec((tf, tn), lambda i, j: (i, j))] * 4,
        out_shape=[plane] * 4,
        compiler_params=_cparams(("parallel", "arbitrary")),
        name="hy_spec",
    )(cos, fwd_i, cos, fwd_i, hs, hd)


def _dft_fwd_kernel(c_ref, s_ref, e_ref, o_ref, tw_ref, har_ref, hai_ref, hbr_ref, hbi_ref,
                    v0r_ref, v0i_ref, v1r_ref, v1i_ref):
    cm, sm = c_ref[...], s_ref[...]
    tw = tw_ref[...]
    c, s = tw[:, 0:1], tw[:, 1:2]
    out_refs = (v0r_ref, v0i_ref, v1r_ref, v1i_ref)

    def combine(er, ei, orr, oi, planes, c, s, first_row):
        har, hai, hbr, hbi = planes
        tr = c * orr + s * oi
        ti = c * oi - s * orr
        ar, br = er + tr, er - tr
        if first_row:
            ai, bi = ei, -oi
            yar, ybr = ar * har, br * hbr
            yai = ai * hai - bi * hbi
            ybi = ai * hbi + bi * hai
            return 0.5 * (yar + ybr), yai, 0.5 * (yar - ybr), -ybi
        ai, bi = ei + ti, ti - ei
        yar, yai = ar * har - ai * hai, ar * hai + ai * har
        ybr, ybi = br * hbr - bi * hbi, br * hbi + bi * hbr
        dr, di = yar - ybr, yai + ybi
        return yar + ybr, yai - ybi, c * dr - s * di, c * di + s * dr

    subs = [slice(lo, lo + DFT_SUBCOLS) for lo in range(0, e_ref.shape[2], DFT_SUBCOLS)]
    prods = []
    for cs in subs:
        e, o = e_ref[0, :, cs], o_ref[0, :, cs]
        prods.append((jnp.dot(cm, e, preferred_element_type=F32), jnp.dot(sm, e, preferred_element_type=F32),
                      jnp.dot(cm, o, preferred_element_type=F32), jnp.dot(sm, o, preferred_element_type=F32)))
    for cs, prod in zip(subs, prods):
        planes = (har_ref[:, cs], hai_ref[:, cs], hbr_ref[:, cs], hbi_ref[:, cs])
        for ref, val in zip(out_refs, combine(*prod, planes, c, s, False)):
            ref[0, :, cs] = val.astype(BF16)

    @pl.when(pl.program_id(0) == 0)
    def _():
        top = slice(0, BF16_SUBLANES)
        is_row0 = lax.broadcasted_iota(I32, (BF16_SUBLANES, 1), 0) == 0
        for cs, prod in zip(subs, prods):
            planes = (har_ref[top, cs], hai_ref[top, cs], hbr_ref[top, cs], hbi_ref[top, cs])
            head = [p[top] for p in prod]
            general = combine(*head, planes, c[top], s[top], False)
            special = combine(*head, planes, c[top], s[top], True)
            for ref, g, sp in zip(out_refs, general, special):
                ref[0, top, cs] = jnp.where(is_row0, sp, g).astype(BF16)


def _dft_inv_kernel(gr_ref, gi_ref, v0r_ref, v0i_ref, v1r_ref, v1i_ref, ze_ref, zo_ref, ge_ref, go_ref,
                    fb_ref, ye_ref, yo_ref):
    gr, gi = gr_ref[...], gi_ref[...]
    fb = fb_ref[...]
    ye = jnp.dot(gr, v0r_ref[0], preferred_element_type=F32) + jnp.dot(gi, v0i_ref[0], preferred_element_type=F32)
    ye_ref[0] = (ge_ref[0].astype(F32) * (ye + ze_ref[0].astype(F32) * fb)).astype(BF16)
    yo = jnp.dot(gr, v1r_ref[0], preferred_element_type=F32) + jnp.dot(gi, v1i_ref[0], preferred_element_type=F32)
    yo_ref[0] = (go_ref[0].astype(F32) * (yo + zo_ref[0].astype(F32) * fb)).astype(BF16)


def _long_conv_gate(consts, planes, order, z_e, z_o, gate_e, gate_o, filt_bias):
    cos, fwd_i, inv_r, inv_i, twiddle = consts
    (ze_arr, ze_col), (zo_arr, zo_col) = z_e, z_o
    (ge_arr, ge_col), (go_arr, go_col) = gate_e, gate_o
    batch, half, _ = ze_arr.shape
    d = D_MODEL
    tf, tn = DFT_TILE, DFT_COLS
    nj = d // tn
    plane_spec = pl.BlockSpec((tf, tn), lambda i, j, b: (i, order * nj + j))
    mat_f = pl.BlockSpec((tf, half), lambda i, j, b: (i, 0))
    cols_f = lambda off: pl.BlockSpec((1, half, tn), lambda i, j, b: (b, 0, off + j))
    tile_f = pl.BlockSpec((1, tf, tn), lambda i, j, b: (b, i, j))
    v_shape = jax.ShapeDtypeStruct((batch, half, d), BF16)
    v = pl.pallas_call(
        _dft_fwd_kernel,
        grid=(half // tf, nj, batch),
        in_specs=[mat_f, mat_f, cols_f(ze_col), cols_f(zo_col), pl.BlockSpec((tf, LANES), lambda i, j, b: (i, 0)),
                  plane_spec, plane_spec, plane_spec, plane_spec],
        out_specs=[tile_f] * 4,
        out_shape=[v_shape] * 4,
        compiler_params=_cparams(("parallel", "parallel", "arbitrary")),
        name="dft_fwd",
    )(cos, fwd_i, ze_arr, zo_arr, twiddle, *planes)
    mat_i = pl.BlockSpec((tf, half), lambda b, j, i: (i, 0))
    cols_i = pl.BlockSpec((1, half, tn), lambda b, j, i: (b, 0, j))
    tile_i = lambda off: pl.BlockSpec((1, tf, tn), lambda b, j, i: (b, i, off + j))
    return pl.pallas_call(
        _dft_inv_kernel,
        grid=(batch, nj, half // tf),
        in_specs=[mat_i, mat_i, cols_i, cols_i, cols_i, cols_i, tile_i(ze_col), tile_i(zo_col),
                  tile_i(ge_col), tile_i(go_col), pl.BlockSpec((1, tn), lambda b, j, i: (0, j))],
        out_specs=[tile_i(0), tile_i(0)],
        out_shape=[v_shape, v_shape],
        compiler_params=_cparams(("parallel", "parallel", "arbitrary")),
        name="dft_inv",
    )(inv_r, inv_i, *v, ze_arr, zo_arr, ge_arr, go_arr, _row2(filt_bias))


def _proj_ln_pair_kernel(ae_ref, ao_ref, w_ref, bias_ref, x_ref, g_ref, b_ref, o_ref, mix_ref):
    w, bias = w_ref[...], bias_ref[...]
    half = ae_ref.shape[0]
    for parity, a_ref in ((0, ae_ref), (1, ao_ref)):
        h = jnp.dot(a_ref[...], w, preferred_element_type=F32) + bias
        for c in range(h.shape[1] // LANES):
            mix_ref[c, pl.ds(parity, half, stride=2), :] = h[:, c * LANES:(c + 1) * LANES]
    h = jnp.concatenate([mix_ref[c] for c in range(mix_ref.shape[0])], axis=1)
    o_ref[...] = _ln_rows(ALPHA * x_ref[...] + h, g_ref[...], b_ref[...])


def _proj_ln_pair(a_e, a_o, w, bias, x, g, b):
    n, d = x.shape
    rows = ROW_TILE // 2
    row = lambda i: (i, 0)
    fix = lambda i: (0, 0)
    return pl.pallas_call(
        _proj_ln_pair_kernel,
        grid=(n // ROW_TILE,),
        in_specs=[pl.BlockSpec((rows, d), row), pl.BlockSpec((rows, d), row), pl.BlockSpec(w.shape, fix),
                  pl.BlockSpec((1, d), fix), pl.BlockSpec((ROW_TILE, d), row),
                  pl.BlockSpec((1, d), fix), pl.BlockSpec((1, d), fix)],
        out_specs=pl.BlockSpec((ROW_TILE, d), row),
        out_shape=jax.ShapeDtypeStruct((n, d), F32),
        scratch_shapes=[pltpu.VMEM((d // LANES, ROW_TILE, LANES), F32)],
        compiler_params=_cparams(("parallel",)),
        name="proj_ln_pair",
    )(a_e, a_o, w, _row2(bias), x, _row2(g), _row2(b))


def _hyena_mixer_ln(x, batch, w_in, b_in, conv_w, conv_b, ffn_w1, ffn_b1, sin_freq, ffn_w2, ffn_b2,
                    ffn_w3, filt_bias, w_out, b_out, ln_g, ln_b):
    n, d = x.shape
    seq = n // batch
    half = seq // 2
    nj = d // DFT_COLS
    planes = _hy_filters(seq, _dft_consts(seq), ffn_w1, ffn_b1, sin_freq, ffn_w2, ffn_b2, ffn_w3)
    cos, fwd_i = _dft_consts(half)
    scale = 1.0 / seq
    k = jnp.arange(half, dtype=F32)[:, None] * (math.pi / seq)
    twiddle = jnp.pad(jnp.concatenate([jnp.cos(k), jnp.sin(k)], axis=1), ((0, 0), (0, LANES - 2)))
    consts = (cos, fwd_i, cos * scale, fwd_i.T * scale, twiddle)
    x_e, x_o = _split_even_odd(x)
    u_e, u_o = _hy_in(x_e, x_o, batch, w_in, b_in, conv_w, conv_b)
    u_e, u_o = u_e.reshape(batch, half, 3 * d), u_o.reshape(batch, half, 3 * d)
    y_e, y_o = _long_conv_gate(consts, planes, 0, (u_e, 0), (u_o, 0), (u_e, nj), (u_o, nj), filt_bias[0])
    y_e, y_o = _long_conv_gate(consts, planes, 1, (y_e, 0), (y_o, 0), (u_e, 2 * nj), (u_o, 2 * nj), filt_bias[1])
    return _proj_ln_pair(y_e.reshape(n // 2, d), y_o.reshape(n // 2, d), w_out.astype(BF16), b_out, x,
                         ln_g, ln_b)


def _na_in_kernel(x_ref, w_ref, b_ref, q_ref, k_ref, v_ref):
    d = D_MODEL
    p = jnp.dot(x_ref[...].astype(BF16), w_ref[...], preferred_element_type=F32) + b_ref[...]
    q_ref[...] = (p[:, :d] * (NA_HEAD_DIM ** -0.5 * LOG2E)).astype(BF16)
    k_ref[...] = p[:, d:2 * d].astype(BF16)
    v_ref[...] = p[:, 2 * d:].astype(BF16)


def _na_attn_kernel(q_ref, k_ref, vt_ref, bias_ref, o_ref):
    for i in range(NA_PAIRS):
        rp = pl.program_id(2) * NA_PAIRS + i
        o_ref[i * LANES:(i + 1) * LANES] = _na_pair(rp, q_ref[i * LANES:(i + 1) * LANES], k_ref, vt_ref, bias_ref)


def _na_pair(rp, q, k_ref, vt_ref, bias_ref):
    hpg = LANES // NA_HEAD_DIM
    n_rows = GRID_W
    r0 = 2 * rp
    rs0 = jnp.clip(r0 - NA_WIN_ROWS // 2, 0, n_rows - NA_WIN_ROWS)
    rs1 = jnp.clip(r0 + 1 - NA_WIN_ROWS // 2, 0, n_rows - NA_WIN_ROWS)
    start = jnp.minimum(rs0, n_rows - NA_KEY_ROWS)
    tok0 = pl.multiple_of(start * GRID_W, LANES)
    nkeys = NA_KEY_ROWS * GRID_W
    kw = k_ref[pl.ds(tok0, nkeys), :]
    vtw = vt_ref[0, :, pl.ds(tok0, nkeys)]
    lane = lax.broadcasted_iota(I32, (1, LANES), 1)
    sub = lax.broadcasted_iota(I32, (LANES, 1), 0)
    right = lane >= GRID_W
    acc = jnp.zeros((LANES, LANES), F32)
    for h in range(hpg):
        in_head = (lane >= h * NA_HEAD_DIM) & (lane < (h + 1) * NA_HEAD_DIM)
        qh = jnp.where(in_head, q, jnp.zeros_like(q))
        s = lax.dot_general(kw, qh, NT_DIMS, preferred_element_type=F32)
        tiles = []
        for a in range(NA_KEY_ROWS):
            kr = start + a
            dr0 = kr - r0 + (NA_WIN_ROWS - 1)
            bt = bias_ref[h, jnp.clip(dr0, 0, 2 * NA_WIN_ROWS - 1)]
            if a < 2 or a >= NA_WIN_ROWS:
                pen0 = jnp.where((kr >= rs0) & (kr < rs0 + NA_WIN_ROWS), 0.0, NEG)
                pen1 = jnp.where((kr >= rs1) & (kr < rs1 + NA_WIN_ROWS), 0.0, NEG)
                bt = bt + jnp.where(right, pen1, pen0)
            tiles.append(bt)
        s = s + jnp.concatenate(tiles, axis=0)
        m = jnp.max(s, axis=0, keepdims=True)
        p = jnp.exp2(s - m)
        inv = 1.0 / jnp.sum(p, axis=0, keepdims=True)
        in_head_rows = (sub >= h * NA_HEAD_DIM) & (sub < (h + 1) * NA_HEAD_DIM)
        pv = jnp.dot(vtw, p.astype(BF16), preferred_element_type=F32)
        acc = acc + jnp.where(in_head_rows, pv * inv, 0.0)
    return acc.T.astype(BF16)


def _na_bias_table(rpb):
    nh = rpb.shape[0]
    kc = jnp.arange(GRID_W)[:, None]
    qc = jnp.arange(GRID_W)[None, :]
    cs = jnp.clip(qc - NA_WIN_COLS // 2, 0, GRID_W - NA_WIN_COLS)
    inwin = (kc >= cs) & (kc < cs + NA_WIN_COLS)
    dc = jnp.clip(kc - qc + (NA_WIN_COLS - 1), 0, 2 * NA_WIN_COLS - 2)
    colb = jnp.where(inwin[None, None], rpb.astype(F32)[:, :, dc] * LOG2E, NEG)
    pad = jnp.full((nh, 1, GRID_W, GRID_W), NEG, F32)
    left = jnp.concatenate([colb, pad], axis=1)
    rgt = jnp.concatenate([pad, colb], axis=1)
    return jnp.concatenate([left, rgt], axis=-1)


def _na_mixer_ln(x, batch, w_in, b_in, rpb, w_out, b_out, ln_g, ln_b):
    n, d = x.shape
    seq = n // batch
    assert seq == GRID_W * GRID_W
    row = lambda i: (i, 0)
    fix = lambda i: (0, 0)
    qkv_shape = jax.ShapeDtypeStruct((n, d), BF16)
    q, k, v = pl.pallas_call(
        _na_in_kernel,
        grid=(n // ROW_TILE,),
        in_specs=[pl.BlockSpec((ROW_TILE, d), row), pl.BlockSpec((d, 3 * d), fix), pl.BlockSpec((1, 3 * d), fix)],
        out_specs=[pl.BlockSpec((ROW_TILE, d), row)] * 3,
        out_shape=[qkv_shape] * 3,
        compiler_params=_cparams(("parallel",)),
        name="na_in",
    )(x, w_in.astype(BF16), _row2(b_in))
    vt = jnp.swapaxes(v.reshape(batch, seq, d), 1, 2)
    bias = _na_bias_table(rpb)
    hpg = LANES // NA_HEAD_DIM
    n_pairs = GRID_W // 2
    n_steps = n_pairs // NA_PAIRS
    o = pl.pallas_call(
        _na_attn_kernel,
        grid=(batch, d // LANES, n_steps),
        in_specs=[pl.BlockSpec((NA_PAIRS * LANES, LANES), lambda b, g, r: (b * n_steps + r, g)),
                  pl.BlockSpec((seq, LANES), lambda b, g, r: (b, g)),
                  pl.BlockSpec((1, LANES, seq), lambda b, g, r: (b, g, 0)),
                  pl.BlockSpec((hpg, 2 * NA_WIN_ROWS, GRID_W, LANES), lambda b, g, r: (g, 0, 0, 0))],
        out_specs=pl.BlockSpec((NA_PAIRS * LANES, LANES), lambda b, g, r: (b * n_steps + r, g)),
        out_shape=qkv_shape,
        compiler_params=_cparams(("parallel", "parallel", "arbitrary")),
        name="na_attn",
    )(q, k, vt, bias)
    return _proj_ln(o, w_out.astype(BF16), b_out, x, ln_g, ln_b)


def _moe_route_kernel(x_ref, wr_ref, rb_ref, xs_ref, pos_ref, gate_ref, cnt_ref):
    ne, tb = N_EXPERTS, MOE_TB
    xb, x_lo = _split_bf16(x_ref[...])
    nt = lambda p, q: lax.dot_general(p, q, NT_DIMS, preferred_element_type=F32)
    logits = nt(wr_ref[0], xb) + nt(wr_ref[1], xb) + nt(wr_ref[0], x_lo) + rb_ref[...]
    eio = lax.broadcasted_iota(I32, (ne, tb), 0)
    work = logits
    vals, hots = [], []
    for _ in range(TOP_K):
        m = jnp.max(work, axis=0, keepdims=True)
        idx = jnp.min(jnp.where(work == m, eio, ne), axis=0, keepdims=True)
        hot = eio == idx
        vals.append(m)
        hots.append(hot)
        work = jnp.where(hot, -jnp.inf, work)
    exps = [jnp.exp(v - vals[0]) for v in vals]
    denom = exps[0] + exps[1] + exps[2] + exps[3]
    sel = jnp.zeros((ne, tb), F32)
    for hot in hots:
        sel = sel + jnp.where(hot, 1.0, 0.0)
    earlier = (lax.broadcasted_iota(I32, (tb, tb), 0) < lax.broadcasted_iota(I32, (tb, tb), 1))
    rank = jnp.dot(sel.astype(BF16), jnp.where(earlier, 1.0, 0.0).astype(BF16),
                   preferred_element_type=F32)
    cnt = jnp.sum(sel, axis=1, keepdims=True)
    cnt_pad = jnp.ceil(cnt * (1.0 / MOE_G)) * MOE_G
    below = (lax.broadcasted_iota(I32, (ne, ne), 1) < lax.broadcasted_iota(I32, (ne, ne), 0))
    seg_off = jnp.dot(jnp.where(below, 1.0, 0.0), jnp.broadcast_to(cnt_pad, (ne, tb)),
                      precision=HIGHEST, preferred_element_type=F32)
    slot = seg_off + rank
    pos = [jnp.sum(jnp.where(hot, slot, 0.0), axis=0, keepdims=True).astype(I32) for hot in hots]
    zero_row = jnp.zeros((1, tb), I32)
    pos_ref[0] = jnp.concatenate(pos + [zero_row] * (8 - TOP_K), axis=0)
    gate_ref[0] = jnp.concatenate([e / denom for e in exps] + [zero_row.astype(F32)] * (8 - TOP_K), axis=0)
    cnt_ref[0] = jnp.broadcast_to(cnt_pad, (ne, LANES)).astype(I32)
    local = lax.broadcasted_iota(I32, (MOE_PCH, tb), 0).astype(F32).astype(BF16)
    one, zero = jnp.ones((), BF16), jnp.zeros((), BF16)
    for r in range(MOE_RB // MOE_PCH):
        onehot = None
        for p in pos:
            hit = jnp.where(local == (p - r * MOE_PCH).astype(F32).astype(BF16), one, zero)
            onehot = hit if onehot is None else onehot + hit
        xs_ref[r * MOE_PCH:(r + 1) * MOE_PCH] = jnp.dot(
            onehot, xb, preferred_element_type=F32).astype(BF16)


def _moe_expert_kernel(te_ref, src_ref, dst_ref, nt_ref, xs_hbm, wgu_ref, bgu_ref, wd_ref, bd_ref,
                       ys_hbm, xbuf, ybuf, wgu_bf, wd_bf, sem_in, sem_out):
    t = pl.program_id(0)
    nt = nt_ref[0]
    s = t % 2

    def in_copy(chunk, slot, c):
        return pltpu.make_async_copy(
            xs_hbm.at[pl.ds(pl.multiple_of(chunk * MOE_G, MOE_G), MOE_G)],
            xbuf.at[slot, pl.ds(c * MOE_G, MOE_G)], sem_in.at[slot])

    def out_copy(chunk, slot, c):
        return pltpu.make_async_copy(
            ybuf.at[slot, pl.ds(c * MOE_G, MOE_G)],
            ys_hbm.at[pl.ds(pl.multiple_of(chunk * MOE_G, MOE_G), MOE_G)], sem_out.at[slot])

    def tile_chunks(table_ref, tile):
        base = tile * MOE_CPT
        return [table_ref[base + c] for c in range(MOE_CPT)]

    def start_in(tile, slot):
        for c, chunk in enumerate(tile_chunks(src_ref, tile)):
            in_copy(chunk, slot, c).start()

    def start_out(tile, slot):
        for c, chunk in enumerate(tile_chunks(dst_ref, tile)):
            out_copy(chunk, slot, c).start()

    def wait_in(slot):
        for c in range(MOE_CPT):
            in_copy(0, slot, c).wait()

    def wait_out(slot):
        for c in range(MOE_CPT):
            out_copy(0, slot, c).wait()

    @pl.when(t == 0)
    def _():
        start_in(0, 0)

    @pl.when(t + 1 < nt)
    def _():
        start_in(t + 1, 1 - s)

    @pl.when(t < nt)
    def _():
        @pl.when(jnp.logical_or(t == 0, te_ref[t] != te_ref[jnp.maximum(t - 1, 0)]))
        def _():
            wgu_bf[...] = wgu_ref[0, 0].astype(BF16)
            wd_bf[...] = wd_ref[0, 0].astype(BF16)

        wait_in(s)
        f = wd_bf.shape[0]
        h = jnp.dot(xbuf[s], wgu_bf[...], preferred_element_type=F32) + bgu_ref[0, 0]
        glu = jnp.minimum(h[:, :f], SWIGLU_LIMIT)
        lin = jnp.clip(h[:, f:], -SWIGLU_LIMIT, SWIGLU_LIMIT)
        act = (glu * jax.nn.sigmoid(SWIGLU_ALPHA * glu) * (lin + 1.0)).astype(BF16)
        y = jnp.dot(act, wd_bf[...], preferred_element_type=F32) + bd_ref[0, 0]

        @pl.when(t >= 2)
        def _():
            wait_out(s)

        ybuf[s] = y.astype(BF16)
        start_out(t, s)

        @pl.when(t == nt - 1)
        def _():
            wait_out(s)

            @pl.when(t >= 1)
            def _():
                wait_out(1 - s)


def _moe_combine_kernel(ys_ref, pos_ref, gate_ref, x_ref, g_ref, b_ref, o_ref):
    tb = MOE_TB
    pos = pos_ref[...]
    gate = gate_ref[...].astype(BF16)
    local = lax.broadcasted_iota(I32, (tb, MOE_PCH), 1).astype(F32).astype(BF16)
    zero = jnp.zeros((), BF16)
    f = jnp.zeros((tb, D_MODEL), F32)
    for r in range(MOE_RB // MOE_PCH):
        w = None
        for k in range(TOP_K):
            pk = (pos[:, k:k + 1] - r * MOE_PCH).astype(F32).astype(BF16)
            wk = jnp.where(local == pk, gate[:, k:k + 1], zero)
            w = wk if w is None else w + wk
        f = f + jnp.dot(w, ys_ref[r * MOE_PCH:(r + 1) * MOE_PCH], preferred_element_type=F32)
    o_ref[...] = _ln_rows(ALPHA * x_ref[...] + f, g_ref[...], b_ref[...])


def _moe_tile_tables(cnt_pad, n_tiles_max):
    nb, ne = cnt_pad.shape
    nch_eb = (cnt_pad // MOE_G).T
    seg_off = jnp.cumsum(cnt_pad, axis=1) - cnt_pad
    base_eb = ((jnp.arange(nb, dtype=I32)[:, None] * MOE_RB + seg_off) // MOE_G).T
    first_eb = jnp.cumsum(nch_eb, axis=1) - nch_eb
    tiles_e = (jnp.sum(nch_eb, axis=1) + MOE_CPT - 1) // MOE_CPT
    tile_end = jnp.cumsum(tiles_e)
    n_tiles = tile_end[-1]
    tau = jnp.arange(n_tiles_max, dtype=I32)
    tile_e = jnp.minimum(jnp.sum(tau[:, None] >= tile_end[None, :], axis=1), ne - 1).astype(I32)
    of_tile = tile_e[:, None] == jnp.arange(ne, dtype=I32)[None, :]
    pick = lambda tab: jnp.sum(jnp.where(of_tile[:, :, None], tab[None], 0), axis=1)
    tile_first = tau - jnp.sum(jnp.where(of_tile, (tile_end - tiles_e)[None, :], 0), axis=1)
    k = (tile_first * MOE_CPT)[:, None] + jnp.arange(MOE_CPT, dtype=I32)[None, :]
    first, count, base = pick(first_eb), pick(nch_eb), pick(base_eb)
    inside = (k[:, :, None] >= first[:, None, :]) & (k[:, :, None] < (first + count)[:, None, :])
    src = jnp.sum(jnp.where(inside, (base - first)[:, None, :] + k[:, :, None], 0), axis=2)
    occupied = jnp.any(inside, axis=2) & (tau < n_tiles)[:, None]
    spare_block = ((tau % 2)[:, None] * MOE_CPT + jnp.arange(MOE_CPT, dtype=I32)[None, :]) % nb
    dst = jnp.where(occupied, src, spare_block * MOE_BLOCK_CHUNKS + MOE_SPARE_CHUNK)
    src = jnp.where(occupied, src, MOE_ZERO_CHUNK)
    return (tile_e, src.reshape(-1).astype(I32), dst.reshape(-1).astype(I32),
            n_tiles.astype(I32).reshape(1))


def _moe_ln(x, layer, router_w, router_b, w_gu, b_gu, w_down, b_down, ln_g, ln_b):
    n, d = x.shape
    ne, f = N_EXPERTS, w_down.shape[2]
    depth = w_gu.shape[0]
    expert_block = lambda t, te, sr, ds, nt: (layer, te[t], 0, 0)
    nb = n // MOE_TB
    assert MOE_RB >= MOE_TB * TOP_K + ne * (MOE_G - 1) + 2 * MOE_G
    blk = lambda i: (i, 0)
    blk3 = lambda i: (i, 0, 0)
    fix = lambda i: (0, 0)
    slot_rows = nb * MOE_RB
    xs, pos, gates, cnt = pl.pallas_call(
        _moe_route_kernel,
        grid=(nb,),
        in_specs=[pl.BlockSpec((MOE_TB, d), blk), pl.BlockSpec((2, ne, d), lambda i: (0, 0, 0)),
                  pl.BlockSpec((ne, 1), fix)],
        out_specs=[pl.BlockSpec((MOE_RB, d), blk), pl.BlockSpec((1, 8, MOE_TB), blk3),
                   pl.BlockSpec((1, 8, MOE_TB), blk3), pl.BlockSpec((1, ne, LANES), blk3)],
        out_shape=[jax.ShapeDtypeStruct((slot_rows, d), BF16), jax.ShapeDtypeStruct((nb, 8, MOE_TB), I32),
                   jax.ShapeDtypeStruct((nb, 8, MOE_TB), F32), jax.ShapeDtypeStruct((nb, ne, LANES), I32)],
        compiler_params=_cparams(("parallel",)),
        name="moe_route",
    )(x, jnp.stack(_split_bf16(router_w.T)), router_b.reshape(ne, 1))

    n_tiles_max = nb * MOE_RB // MOE_TM + ne
    tile_e, src, dst, n_tiles = _moe_tile_tables(cnt[:, :, 0], n_tiles_max)
    ys = pl.pallas_call(
        _moe_expert_kernel,
        grid_spec=pltpu.PrefetchScalarGridSpec(
            num_scalar_prefetch=4,
            grid=(n_tiles_max,),
            in_specs=[pl.BlockSpec(memory_space=pl.ANY),
                      pl.BlockSpec((1, 1, d, 2 * f), expert_block),
                      pl.BlockSpec((1, 1, 1, 2 * f), expert_block),
                      pl.BlockSpec((1, 1, f, d), expert_block),
                      pl.BlockSpec((1, 1, 1, d), expert_block)],
            out_specs=pl.BlockSpec(memory_space=pl.ANY),
            scratch_shapes=[pltpu.VMEM((2, MOE_TM, d), BF16), pltpu.VMEM((2, MOE_TM, d), BF16),
                            pltpu.VMEM((d, 2 * f), BF16), pltpu.VMEM((f, d), BF16),
                            pltpu.SemaphoreType.DMA((2,)), pltpu.SemaphoreType.DMA((2,))]),
        out_shape=jax.ShapeDtypeStruct((slot_rows, d), BF16),
        input_output_aliases={4: 0},
        compiler_params=pltpu.CompilerParams(dimension_semantics=("arbitrary",),
                                             vmem_limit_bytes=MOE_EXPERT_VMEM),
        name="moe_expert",
    )(tile_e, src, dst, n_tiles, xs, w_gu, b_gu.reshape(depth, ne, 1, 2 * f), w_down,
      b_down.reshape(depth, ne, 1, d))

    pos_t = jnp.swapaxes(pos, 1, 2).reshape(n, 8)
    gate_t = jnp.swapaxes(gates, 1, 2).reshape(n, 8)
    return pl.pallas_call(
        _moe_combine_kernel,
        grid=(nb,),
        in_specs=[pl.BlockSpec((MOE_RB, d), blk), pl.BlockSpec((MOE_TB, 8), blk), pl.BlockSpec((MOE_TB, 8), blk),
                  pl.BlockSpec((MOE_TB, d), blk), pl.BlockSpec((1, d), fix), pl.BlockSpec((1, d), fix)],
        out_specs=pl.BlockSpec((MOE_TB, d), blk),
        out_shape=jax.ShapeDtypeStruct((n, d), F32),
        compiler_params=_cparams(("parallel",)),
        name="moe_combine",
    )(ys, pos_t, gate_t, x, _row2(ln_g), _row2(ln_b))


def kernel(x, ln1_g, ln1_b, ln2_g, ln2_b, gla_w_in, gla_gate_w2, gla_gate_b, gla_norm_g, gla_w_out,
           hy_w_in, hy_b_in, hy_conv_w, hy_conv_b, hy_ffn_w1, hy_ffn_b1, hy_sin_freq, hy_ffn_w2,
           hy_ffn_b2, hy_ffn_w3, hy_filt_bias, hy_w_out, hy_b_out, na_w_in, na_b_in, na_rpb, na_w_out,
           na_b_out, moe_router_w, moe_router_b, moe_w_gu, moe_b_gu, moe_w_down, moe_b_down):
    batch, seq, d = x.shape
    xt = x.reshape(batch * seq, d)
    for i in range(DEPTH):
        m, j = i % N_MIXERS, i // N_MIXERS
        if m == 0:
            xt = _gla_mixer_ln(xt, batch, gla_w_in[j], gla_gate_w2[j], gla_gate_b[j], gla_norm_g[j],
                               gla_w_out[j], ln1_g[i], ln1_b[i])
        elif m == 1:
            xt = _hyena_mixer_ln(xt, batch, hy_w_in[j], hy_b_in[j], hy_conv_w[j], hy_conv_b[j],
                                 hy_ffn_w1[j], hy_ffn_b1[j], hy_sin_freq[j], hy_ffn_w2[j], hy_ffn_b2[j],
                                 hy_ffn_w3[j], hy_filt_bias[j], hy_w_out[j], hy_b_out[j],
                                 ln1_g[i], ln1_b[i])
        else:
            xt = _na_mixer_ln(xt, batch, na_w_in[j], na_b_in[j], na_rpb[j], na_w_out[j], na_b_out[j],
                              ln1_g[i], ln1_b[i])
        xt = _moe_ln(xt, i, moe_router_w[i], moe_router_b[i], moe_w_gu, moe_b_gu, moe_w_down,
                     moe_b_down, ln2_g[i], ln2_b[i])
    return xt.reshape(batch, seq, d)
```

```python
import functools
import math

import jax
import jax.numpy as jnp
from jax import lax
from jax.experimental import pallas as pl
from jax.experimental.pallas import tpu as pltpu

F32, BF16, I32 = jnp.float32, jnp.bfloat16, jnp.int32
HIGHEST = lax.Precision.HIGHEST
NT_DIMS = (((1,), (1,)), ((), ()))
TN_DIMS = (((0,), (0,)), ((), ()))

D_MODEL = 1024
DEPTH = 4
N_MIXERS = 3
GRID_W = 64
ALPHA = (2 * DEPTH) ** 0.25
LN_EPS = 1e-5

GLA_HEADS = 4
GLA_KEY_DIM = 512
GLA_VAL_DIM = 1024
GLA_HEAD_K = 128
GLA_HEAD_V = 256
GLA_GATE_RANK = 16
GLA_TAU = 16.0
GLA_NORM_EPS = 1e-6

HYENA_ORDER = 2
HYENA_POS_EMB = 33
HYENA_FILTER_HIDDEN = 64
HYENA_DECAY_TARGET = 1e-2
HYENA_FAST_DECAY = 0.3
HYENA_SLOW_DECAY = 1.5
HYENA_WINDOW_SHIFT = 0.05

NA_HEAD_DIM = 32
NA_HEADS = 32
NA_WIN_ROWS = 8
NA_WIN_COLS = 16

N_EXPERTS = 32
TOP_K = 4
SWIGLU_ALPHA = 1.702
SWIGLU_LIMIT = 7.0

LANES = 128
BF16_SUBLANES = 16
VMEM_LIMIT = 48 * 1024 * 1024
MOE_EXPERT_VMEM = 56 * 1024 * 1024

ROW_TILE = 512
LN_ROW_TILE = 1024
GLA_CHUNK = 128
GLA_SUB = 32
GLA_ROWS = 256
GLA_HEADS_PER_STEP = 4
GLA_EXP2_CLAMP = 115.0
DFT_TILE = 512
DFT_COLS = 512
DFT_SUBCOLS = 256
NA_KEY_ROWS = 10
NA_PAIRS = 16
NEG = -1e30
LOG2E = math.log2(math.e)

MOE_TB = 512
MOE_G = BF16_SUBLANES
MOE_RB = MOE_TB * TOP_K + 512
MOE_PCH = 256
MOE_BLOCK_CHUNKS = MOE_RB // MOE_G
MOE_ZERO_CHUNK = MOE_BLOCK_CHUNKS - 1
MOE_SPARE_CHUNK = MOE_BLOCK_CHUNKS - 2
MOE_TM = 512
MOE_CPT = MOE_TM // MOE_G


def _cparams(sem):
    return pltpu.CompilerParams(dimension_semantics=sem, vmem_limit_bytes=VMEM_LIMIT)


def _ln_rows(y, g, b):
    mu = jnp.mean(y, axis=-1, keepdims=True)
    yc = y - mu
    var = jnp.mean(yc * yc, axis=-1, keepdims=True)
    return yc * lax.rsqrt(var + LN_EPS) * g + b


def _row2(v):
    return v.reshape(1, -1)


def _split_bf16(a):
    hi = a.astype(BF16)
    return hi, (a - hi.astype(F32)).astype(BF16)


def _dot_split(a, b_hi, b_lo, dims):
    a_hi, a_lo = _split_bf16(a)
    if dims is None:
        mm = lambda p, q: jnp.dot(p, q, preferred_element_type=F32)
    else:
        mm = lambda p, q: lax.dot_general(p, q, dims, preferred_element_type=F32)
    return mm(a_hi, b_hi) + mm(a_lo, b_hi) + mm(a_hi, b_lo)


def _proj_ln_kernel(a_ref, w_ref, bias_ref, x_ref, g_ref, b_ref, o_ref):
    h = jnp.dot(a_ref[...], w_ref[...], preferred_element_type=F32) + bias_ref[...]
    o_ref[...] = _ln_rows(ALPHA * x_ref[...] + h, g_ref[...], b_ref[...])


def _proj_ln(a, w, bias, x, g, b):
    n, d = x.shape
    row = lambda i: (i, 0)
    fix = lambda i: (0, 0)
    return pl.pallas_call(
        _proj_ln_kernel,
        grid=(n // LN_ROW_TILE,),
        in_specs=[pl.BlockSpec((LN_ROW_TILE, a.shape[1]), row), pl.BlockSpec(w.shape, fix),
                  pl.BlockSpec((1, d), fix), pl.BlockSpec((LN_ROW_TILE, d), row),
                  pl.BlockSpec((1, d), fix), pl.BlockSpec((1, d), fix)],
        out_specs=pl.BlockSpec((LN_ROW_TILE, d), row),
        out_shape=jax.ShapeDtypeStruct((n, d), F32),
        compiler_params=_cparams(("parallel",)),
        name="proj_ln",
    )(a, w, _row2(bias), x, _row2(g), _row2(b))


def _gla_in_kernel(x_ref, w_ref, w2_ref, gb_ref, q_ref, k_ref, v_ref, g_ref, la_ref):
    kd, vd = GLA_KEY_DIM, GLA_VAL_DIM
    xb = x_ref[...].astype(BF16)
    proj = lambda lo, hi: jnp.dot(xb, w_ref[:, lo:hi], preferred_element_type=F32)
    glow = proj(2 * kd + 2 * vd, 2 * kd + 2 * vd + LANES)
    z = _dot_split(glow, w2_ref[0], w2_ref[1], None) + gb_ref[...]
    la_ref[...] = (jnp.minimum(z, 0.0) - jnp.log(1.0 + jnp.exp(-jnp.abs(z)))) * (LOG2E / GLA_TAU)
    og = proj(2 * kd + vd, 2 * kd + 2 * vd)
    g_ref[...] = (og * jax.nn.sigmoid(og)).astype(BF16)
    q_ref[...] = (proj(0, kd) * GLA_HEAD_K ** -0.5).astype(BF16)
    k_ref[...] = proj(kd, 2 * kd).astype(BF16)
    v_ref[...] = proj(2 * kd, 2 * kd + vd).astype(BF16)


def _gla_in(x, w_in, gate_w2, gate_b):
    n, d = x.shape
    kd, vd, r = GLA_KEY_DIM, GLA_VAL_DIM, GLA_GATE_RANK
    width = 2 * kd + 2 * vd
    w = jnp.pad(w_in, ((0, 0), (0, LANES - 2 * r))).astype(BF16)
    w2 = jnp.zeros((LANES, 2 * kd), F32)
    w2 = w2.at[:r, :kd].set(gate_w2[0]).at[r:2 * r, kd:].set(gate_w2[1])
    w2 = jnp.stack(_split_bf16(w2))
    gb = gate_b.reshape(1, 2 * kd)
    row = lambda i: (i, 0)
    fix = lambda i: (0, 0)
    return pl.pallas_call(
        _gla_in_kernel,
        grid=(n // ROW_TILE,),
        in_specs=[pl.BlockSpec((ROW_TILE, d), row), pl.BlockSpec((d, width + LANES), fix),
                  pl.BlockSpec((2, LANES, 2 * kd), lambda i: (0, 0, 0)), pl.BlockSpec((1, 2 * kd), fix)],
        out_specs=[pl.BlockSpec((ROW_TILE, kd), row), pl.BlockSpec((ROW_TILE, kd), row),
                   pl.BlockSpec((ROW_TILE, vd), row), pl.BlockSpec((ROW_TILE, vd), row),
                   pl.BlockSpec((ROW_TILE, 2 * kd), row)],
        out_shape=[jax.ShapeDtypeStruct((n, kd), BF16), jax.ShapeDtypeStruct((n, kd), BF16),
                   jax.ShapeDtypeStruct((n, vd), BF16), jax.ShapeDtypeStruct((n, vd), BF16),
                   jax.ShapeDtypeStruct((n, 2 * kd), F32)],
        compiler_params=_cparams(("parallel",)),
        name="gla_in",
    )(x, w, w2, gb)


def _gla_direction(q_ref, k_ref, v_ref, la_ref, st_ref, o_ref, reverse):
    rows, cr = q_ref.shape[0], GLA_CHUNK
    nch = rows // cr
    shift = cr.bit_length() - 1
    ri = lax.broadcasted_iota(I32, (rows, rows), 0)
    ci = lax.broadcasted_iota(I32, (rows, rows), 1)
    same_chunk = jnp.right_shift(ri, shift) == jnp.right_shift(ci, shift)
    within = (ci >= ri) if reverse else (ci <= ri)
    cum_mat = jnp.where(same_chunk & within, 1.0, 0.0).astype(BF16)
    la_all = la_ref[...]
    la_hi = la_all.astype(BF16)
    rest = la_all - la_hi.astype(F32)
    la_mid = rest.astype(BF16)
    la_lo = (rest - la_mid.astype(F32)).astype(BF16)
    c_heads = (jnp.dot(cum_mat, la_hi, preferred_element_type=F32)
               + jnp.dot(cum_mat, la_mid, preferred_element_type=F32)
               + jnp.dot(cum_mat, la_lo, preferred_element_type=F32))
    rj = lax.broadcasted_iota(I32, (cr, cr), 0)
    cj = lax.broadcasted_iota(I32, (cr, cr), 1)
    mask = (cj > rj) if reverse else (cj <= rj)

    for head in range(st_ref.shape[0]):
        kcols = slice(head * GLA_HEAD_K, (head + 1) * GLA_HEAD_K)
        vcols = slice(head * GLA_HEAD_V, (head + 1) * GLA_HEAD_V)
        la, c_all = la_all[:, kcols], c_heads[:, kcols]
        qes, decs, intra, upd = [], [], [], []
        for ch in range(nch):
            lo = ch * cr
            c = c_all[lo:lo + cr]
            qf = q_ref[lo:lo + cr, kcols].astype(F32)
            kf = k_ref[lo:lo + cr, kcols].astype(F32)
            v = v_ref[lo:lo + cr, vcols]
            ctot = c[0:1] if reverse else c[cr - 1:cr]
            qes.append((qf * jnp.exp2(c)).astype(BF16))
            decs.append(jnp.exp2(ctot))
            blocks = []
            for i in range(cr // GLA_SUB):
                blo, bhi = i * GLA_SUB, (i + 1) * GLA_SUB
                edge = bhi - 1 if reverse else blo
                ref = c[edge:edge + 1] - la[lo + edge:lo + edge + 1]
                qi = (qf[blo:bhi] * jnp.exp2(c[blo:bhi] - ref)).astype(BF16)
                ki = (kf * jnp.exp2(jnp.minimum(ref - c, GLA_EXP2_CLAMP))).astype(BF16)
                blocks.append(lax.dot_general(qi, ki, NT_DIMS, preferred_element_type=F32))
            scores = jnp.where(mask, jnp.concatenate(blocks, axis=0), 0.0).astype(BF16)
            intra.append(jnp.dot(scores, v, preferred_element_type=F32))
            kd = (kf * jnp.exp2(ctot - c)).astype(BF16)
            upd.append(lax.dot_general(v, kd, TN_DIMS, preferred_element_type=F32))

        st = st_ref[head]
        states = [None] * nch
        for ch in (reversed(range(nch)) if reverse else range(nch)):
            states[ch] = st
            st = st * decs[ch] + upd[ch]
        st_ref[head] = st
        for ch in range(nch):
            inter = lax.dot_general(qes[ch], states[ch].astype(BF16), NT_DIMS, preferred_element_type=F32)
            o_ref[ch * cr:(ch + 1) * cr, vcols] = (intra[ch] + inter).astype(BF16)


def _gla_scan_kernel(qf_ref, kf_ref, vf_ref, laf_ref, qb_ref, kb_ref, vb_ref, lab_ref,
                     of_ref, ob_ref, sf_ref, sb_ref):
    @pl.when(pl.program_id(2) == 0)
    def _():
        sf_ref[...] = jnp.zeros_like(sf_ref)
        sb_ref[...] = jnp.zeros_like(sb_ref)

    _gla_direction(qf_ref, kf_ref, vf_ref, laf_ref, sf_ref, of_ref, False)
    _gla_direction(qb_ref, kb_ref, vb_ref, lab_ref, sb_ref, ob_ref, True)


def _gla_scan(q, k, v, la, batch):
    n = q.shape[0]
    nblk = n // batch // GLA_ROWS
    hps = GLA_HEADS_PER_STEP
    hk, hv, ngrp = hps * GLA_HEAD_K, hps * GLA_HEAD_V, GLA_HEADS // hps
    fwd = lambda b, h, i: (b * nblk + i, h)
    bwd = lambda b, h, i: (b * nblk + nblk - 1 - i, h)
    bwd_la = lambda b, h, i: (b * nblk + nblk - 1 - i, ngrp + h)
    state = pltpu.VMEM((hps, GLA_HEAD_V, GLA_HEAD_K), F32)
    return pl.pallas_call(
        _gla_scan_kernel,
        grid=(batch, ngrp, nblk),
        in_specs=[pl.BlockSpec((GLA_ROWS, hk), fwd), pl.BlockSpec((GLA_ROWS, hk), fwd),
                  pl.BlockSpec((GLA_ROWS, hv), fwd), pl.BlockSpec((GLA_ROWS, hk), fwd),
                  pl.BlockSpec((GLA_ROWS, hk), bwd), pl.BlockSpec((GLA_ROWS, hk), bwd),
                  pl.BlockSpec((GLA_ROWS, hv), bwd), pl.BlockSpec((GLA_ROWS, hk), bwd_la)],
        out_specs=[pl.BlockSpec((GLA_ROWS, hv), fwd), pl.BlockSpec((GLA_ROWS, hv), bwd)],
        out_shape=[jax.ShapeDtypeStruct((n, GLA_VAL_DIM), BF16)] * 2,
        scratch_shapes=[state, state],
        compiler_params=_cparams(("parallel", "parallel", "arbitrary")),
        name="gla_scan",
    )(q, k, v, la, q, k, v, la)


def _gla_out_kernel(of_ref, ob_ref, gate_ref, ng_ref, w_ref, x_ref, g_ref, b_ref, o_ref):
    o = of_ref[...].astype(F32) + ob_ref[...].astype(F32)
    parts = []
    for h in range(GLA_HEADS):
        oh = o[:, h * GLA_HEAD_V:(h + 1) * GLA_HEAD_V]
        ms = jnp.mean(oh * oh, axis=-1, keepdims=True)
        parts.append(oh * lax.rsqrt(ms + GLA_NORM_EPS))
    on = jnp.concatenate(parts, axis=-1) * ng_ref[...]
    a = (on * gate_ref[...].astype(F32)).astype(BF16)
    h = jnp.dot(a, w_ref[...], preferred_element_type=F32)
    o_ref[...] = _ln_rows(ALPHA * x_ref[...] + h, g_ref[...], b_ref[...])


def _gla_mixer_ln(x, batch, w_in, gate_w2, gate_b, norm_g, w_out, ln_g, ln_b):
    n, d = x.shape
    q, k, v, gate, la = _gla_in(x, w_in, gate_w2, gate_b)
    o_f, o_b = _gla_scan(q, k, v, la, batch)
    row = lambda i: (i, 0)
    fix = lambda i: (0, 0)
    vd = GLA_VAL_DIM
    return pl.pallas_call(
        _gla_out_kernel,
        grid=(n // LN_ROW_TILE,),
        in_specs=[pl.BlockSpec((LN_ROW_TILE, vd), row), pl.BlockSpec((LN_ROW_TILE, vd), row),
                  pl.BlockSpec((LN_ROW_TILE, vd), row), pl.BlockSpec((1, vd), fix),
                  pl.BlockSpec((vd, d), fix), pl.BlockSpec((LN_ROW_TILE, d), row),
                  pl.BlockSpec((1, d), fix), pl.BlockSpec((1, d), fix)],
        out_specs=pl.BlockSpec((LN_ROW_TILE, d), row),
        out_shape=jax.ShapeDtypeStruct((n, d), F32),
        compiler_params=_cparams(("parallel",)),
        name="gla_out",
    )(o_f, o_b, gate, _row2(jnp.tile(norm_g, GLA_HEADS)), w_out.astype(BF16), x,
      _row2(ln_g), _row2(ln_b))


def _split_even_odd_kernel(x_ref, xe_ref, xo_ref, mix_ref):
    x = x_ref[...]
    half = x.shape[0] // 2
    nchunk = mix_ref.shape[0]
    for c in range(nchunk):
        mix_ref[c] = x[:, c * LANES:(c + 1) * LANES]
    for parity, o_ref in ((0, xe_ref), (1, xo_ref)):
        rows = pl.ds(parity, half, stride=2)
        o_ref[...] = jnp.concatenate([mix_ref[c, rows, :] for c in range(nchunk)], axis=1).astype(BF16)


def _split_even_odd(x):
    n, d = x.shape
    row = lambda i: (i, 0)
    half_shape = jax.ShapeDtypeStruct((n // 2, d), BF16)
    return pl.pallas_call(
        _split_even_odd_kernel,
        grid=(n // ROW_TILE,),
        in_specs=[pl.BlockSpec((ROW_TILE, d), row)],
        out_specs=[pl.BlockSpec((ROW_TILE // 2, d), row)] * 2,
        out_shape=[half_shape, half_shape],
        scratch_shapes=[pltpu.VMEM((d // LANES, ROW_TILE, LANES), F32)],
        compiler_params=_cparams(("parallel",)),
        name="split_even_odd",
    )(x)


def _hy_in_kernel(xe_ref, xo_ref, w_ref, b_ref, cw_ref, cb_ref, ue_ref, uo_ref):
    w, b = w_ref[...], b_ref[...]
    qe = jnp.dot(xe_ref[...], w, preferred_element_type=F32)
    qo = jnp.dot(xo_ref[...], w, preferred_element_type=F32)
    rows = qe.shape[0]
    cw = cw_ref[...]
    w0, w1, w2 = cw[0:1], cw[1:2], cw[2:3]
    const = cb_ref[...] + b * (w0 + w1 + w2)
    ue_ref[...] = (pltpu.roll(qo, 1, axis=0) * w0 + qe * w1 + qo * w2 + const).astype(BF16)
    uo_ref[...] = (qe * w0 + qo * w1 + pltpu.roll(qe, rows - 1, axis=0) * w2 + const).astype(BF16)
    g = BF16_SUBLANES
    ri = lax.broadcasted_iota(I32, (g, 1), 0)
    qe_s, qo_s = qe[0:g], qo[0:g]
    inner = qe_s * w1 + qo_s * w2 + const
    ue_ref[0:g] = jnp.where(ri == 0, inner - b * w0, inner + pltpu.roll(qo_s, 1, axis=0) * w0).astype(BF16)
    qe_s, qo_s = qe[rows - g:], qo[rows - g:]
    inner = qe_s * w0 + qo_s * w1 + const
    uo_ref[rows - g:] = jnp.where(ri == g - 1, inner - b * w2,
                                  inner + pltpu.roll(qe_s, g - 1, axis=0) * w2).astype(BF16)


def _hy_in(xe, xo, batch, w_in, b_in, conv_w, conv_b):
    n2, d = xe.shape
    half = n2 // batch
    width = w_in.shape[1]
    cols = 256
    seq_spec = pl.BlockSpec((half, d), lambda b, j: (b, 0))
    col_spec = lambda r: pl.BlockSpec((r, cols), lambda b, j: (0, j))
    out_spec = pl.BlockSpec((half, cols), lambda b, j: (b, j))
    return pl.pallas_call(
        _hy_in_kernel,
        grid=(batch, width // cols),
        in_specs=[seq_spec, seq_spec, pl.BlockSpec((d, cols), lambda b, j: (0, j)), col_spec(1), col_spec(3),
                  col_spec(1)],
        out_specs=[out_spec, out_spec],
        out_shape=[jax.ShapeDtypeStruct((n2, width), BF16)] * 2,
        compiler_params=_cparams(("parallel", "arbitrary")),
        name="hy_in",
    )(xe, xo, w_in.astype(BF16), _row2(b_in), conv_w, _row2(conv_b))


def _hy_filter_kernel(pe_ref, w1_ref, b1_ref, f_ref, w2_ref, b2_ref, w3_ref, win_ref, hs_ref, hd_ref):
    f = f_ref[...]
    h = jnp.sin(f[0:1] * (jnp.dot(pe_ref[...], w1_ref[...], precision=HIGHEST,
                                  preferred_element_type=F32) + b1_ref[...]))
    h = jnp.sin(f[1:2] * (jnp.dot(h, w2_ref[...], precision=HIGHEST,
                                  preferred_element_type=F32) + b2_ref[...]))
    h = jnp.dot(h, w3_ref[...], precision=HIGHEST, preferred_element_type=F32)
    win = win_ref[...]
    d = win.shape[1]
    rows = win.shape[0]
    t = pl.program_id(0) * rows + lax.broadcasted_iota(I32, (rows, 1), 0)
    for o in range(HYENA_ORDER):
        hf = h[:, 2 * o * d:(2 * o + 1) * d] * win
        hb = jnp.where(t == 0, 0.0, h[:, (2 * o + 1) * d:(2 * o + 2) * d] * win)
        hs_ref[:, o * d:(o + 1) * d] = (hf + hb).astype(BF16)
        hd_ref[:, o * d:(o + 1) * d] = (hf - hb).astype(BF16)


def _hy_spec_kernel(c_ref, s_ref, cmid_ref, smid_ref, hs_ref, hd_ref, har_ref, hai_ref, hbr_ref, hbi_ref):
    c, s = c_ref[...], s_ref[...]
    hs, hd = hs_ref[...], hd_ref[...]
    n = lax.broadcasted_iota(I32, (hs.shape[0], 1), 0)
    sgn = jnp.where(jnp.bitwise_and(n, 1) == 0, 1.0, -1.0).astype(BF16)
    ri = lax.broadcasted_iota(I32, (c.shape[0], 1), 0) + pl.program_id(0) * c.shape[0]
    mid_r = jnp.dot(cmid_ref[...], hs, preferred_element_type=F32)[0:1]
    mid_i = jnp.dot(smid_ref[...], hd, preferred_element_type=F32)[0:1]
    har_ref[...] = jnp.dot(c, hs, preferred_element_type=F32)
    hai_ref[...] = jnp.where(ri == 0, mid_r, jnp.dot(s, hd, preferred_element_type=F32))
    hbr_ref[...] = jnp.dot(c, hs * sgn, preferred_element_type=F32)
    hbi_ref[...] = jnp.where(ri == 0, mid_i, -jnp.dot(s, hd * sgn, preferred_element_type=F32))


def _dft_consts(seq):
    k = jnp.arange(seq, dtype=I32)
    step = LANES
    assert seq % step == 0
    phase = lambda t: ((k[:, None] * t[None, :]) % (2 * seq)).astype(F32) * (math.pi / seq)
    ang_hi = phase(jnp.arange(seq // step, dtype=I32) * step)[:, :, None]
    ang_lo = phase(jnp.arange(step, dtype=I32))[:, None, :]
    cos = (jnp.cos(ang_hi) * jnp.cos(ang_lo) - jnp.sin(ang_hi) * jnp.sin(ang_lo)).reshape(seq, seq)
    sin = (jnp.sin(ang_hi) * jnp.cos(ang_lo) + jnp.cos(ang_hi) * jnp.sin(ang_lo)).reshape(seq, seq)
    sgn = jnp.where(k % 2 == 0, 1.0, -1.0).astype(F32)
    fwd_i = jnp.where(k[:, None] == 0, sgn[None, :], -sin)
    return cos.astype(BF16), fwd_i.astype(BF16)


def _hy_filters(seq, consts, w1, b1, sin_freq, w2, b2, w3):
    cos, fwd_i = consts[0], consts[1]
    d = w3.shape[-1] // (2 * HYENA_ORDER)
    hid = HYENA_FILTER_HIDDEN
    t = jnp.arange(seq, dtype=F32)
    t_norm = t / max(seq - 1, 1)
    bands = (HYENA_POS_EMB - 1) // 2
    freqs = jnp.linspace(1e-4, bands - 1, bands, dtype=F32)
    ang = (2.0 * math.pi / seq) * t[:, None] * freqs[None, :]
    pe = jnp.concatenate([t_norm[:, None], jnp.cos(ang), -jnp.sin(ang)], axis=-1)
    pe = jnp.pad(pe, ((0, 0), (0, LANES - HYENA_POS_EMB)))
    min_decay = math.log(HYENA_DECAY_TARGET) / HYENA_SLOW_DECAY
    max_decay = math.log(HYENA_DECAY_TARGET) / HYENA_FAST_DECAY
    deltas = jnp.abs(jnp.linspace(min_decay, max_decay, d, dtype=F32))
    window = jnp.exp(-t_norm[:, None] * deltas[None, :]) + HYENA_WINDOW_SHIFT
    padc = LANES - hid
    w1p = jnp.pad(w1, ((0, LANES - HYENA_POS_EMB), (0, padc)))
    w2p = jnp.pad(w2, ((0, padc), (0, padc)))
    w3p = jnp.pad(w3, ((0, padc), (0, 0)))
    rows = 256
    fix = lambda i: (0, 0)
    row = lambda i: (i, 0)
    hs, hd = pl.pallas_call(
        _hy_filter_kernel,
        grid=(seq // rows,),
        in_specs=[pl.BlockSpec((rows, LANES), row), pl.BlockSpec((LANES, LANES), fix),
                  pl.BlockSpec((1, LANES), fix), pl.BlockSpec((2, LANES), fix),
                  pl.BlockSpec((LANES, LANES), fix), pl.BlockSpec((1, LANES), fix),
                  pl.BlockSpec((LANES, 2 * HYENA_ORDER * d), fix), pl.BlockSpec((rows, d), row)],
        out_specs=[pl.BlockSpec((rows, HYENA_ORDER * d), row)] * 2,
        out_shape=[jax.ShapeDtypeStruct((seq, HYENA_ORDER * d), BF16)] * 2,
        compiler_params=_cparams(("parallel",)),
        name="hy_filter",
    )(pe, w1p, _row2(jnp.pad(b1, (0, padc))), jnp.pad(sin_freq, ((0, 0), (0, padc))), w2p,
      _row2(jnp.pad(b2, (0, padc))), w3p, window)
    tf, tn = DFT_TILE, DFT_COLS
    half = seq // 2
    tn = DFT_SUBCOLS
    plane = jax.ShapeDtypeStruct((half, HYENA_ORDER * d), F32)
    mat = pl.BlockSpec((tf, seq), lambda i, j: (i, 0))
    mid = pl.BlockSpec((BF16_SUBLANES, seq), lambda i, j: (half // BF16_SUBLANES, 0))
    col = pl.BlockSpec((seq, tn), lambda i, j: (0, j))
    return pl.pallas_call(
        _hy_spec_kernel,
        grid=(half // tf, HYENA_ORDER * d // tn),
        in_specs=[mat, mat, mid, mid, col, col],
        out_specs=[pl.BlockSpec((tf, tn), lambda i, j: (i, j))] * 4,
        out_shape=[plane] * 4,
        compiler_params=_cparams(("parallel", "arbitrary")),
        name="hy_spec",
    )(cos, fwd_i, cos, fwd_i, hs, hd)


def _dft_fwd_kernel(c_ref, s_ref, e_ref, o_ref, tw_ref, har_ref, hai_ref, hbr_ref, hbi_ref,
                    v0r_ref, v0i_ref, v1r_ref, v1i_ref):
    cm, sm = c_ref[...], s_ref[...]
    tw = tw_ref[...]
    c, s = tw[:, 0:1], tw[:, 1:2]
    out_refs = (v0r_ref, v0i_ref, v1r_ref, v1i_ref)

    def combine(er, ei, orr, oi, planes, c, s, first_row):
        har, hai, hbr, hbi = planes
        tr = c * orr + s * oi
        ti = c * oi - s * orr
        ar, br = er + tr, er - tr
        if first_row:
            ai, bi = ei, -oi
            yar, ybr = ar * har, br * hbr
            yai = ai * hai - bi * hbi
            ybi = ai * hbi + bi * hai
            return 0.5 * (yar + ybr), yai, 0.5 * (yar - ybr), -ybi
        ai, bi = ei + ti, ti - ei
        yar, yai = ar * har - ai * hai, ar * hai + ai * har
        ybr, ybi = br * hbr - bi * hbi, br * hbi + bi * hbr
        dr, di = yar - ybr, yai + ybi
        return yar + ybr, yai - ybi, c * dr - s * di, c * di + s * dr

    subs = [slice(lo, lo + DFT_SUBCOLS) for lo in range(0, e_ref.shape[2], DFT_SUBCOLS)]
    prods = []
    for cs in subs:
        e, o = e_ref[0, :, cs], o_ref[0, :, cs]
        prods.append((jnp.dot(cm, e, preferred_element_type=F32), jnp.dot(sm, e, preferred_element_type=F32),
                      jnp.dot(cm, o, preferred_element_type=F32), jnp.dot(sm, o, preferred_element_type=F32)))
    for cs, prod in zip(subs, prods):
        planes = (har_ref[:, cs], hai_ref[:, cs], hbr_ref[:, cs], hbi_ref[:, cs])
        for ref, val in zip(out_refs, combine(*prod, planes, c, s, False)):
            ref[0, :, cs] = val.astype(BF16)

    @pl.when(pl.program_id(0) == 0)
    def _():
        top = slice(0, BF16_SUBLANES)
        is_row0 = lax.broadcasted_iota(I32, (BF16_SUBLANES, 1), 0) == 0
        for cs, prod in zip(subs, prods):
            planes = (har_ref[top, cs], hai_ref[top, cs], hbr_ref[top, cs], hbi_ref[top, cs])
            head = [p[top] for p in prod]
            general = combine(*head, planes, c[top], s[top], False)
            special = combine(*head, planes, c[top], s[top], True)
            for ref, g, sp in zip(out_refs, general, special):
                ref[0, top, cs] = jnp.where(is_row0, sp, g).astype(BF16)


def _dft_inv_kernel(gr_ref, gi_ref, v0r_ref, v0i_ref, v1r_ref, v1i_ref, ze_ref, zo_ref, ge_ref, go_ref,
                    fb_ref, ye_ref, yo_ref):
    gr, gi = gr_ref[...], gi_ref[...]
    fb = fb_ref[...]
    ye = jnp.dot(gr, v0r_ref[0], preferred_element_type=F32) + jnp.dot(gi, v0i_ref[0], preferred_element_type=F32)
    ye_ref[0] = (ge_ref[0].astype(F32) * (ye + ze_ref[0].astype(F32) * fb)).astype(BF16)
    yo = jnp.dot(gr, v1r_ref[0], preferred_element_type=F32) + jnp.dot(gi, v1i_ref[0], preferred_element_type=F32)
    yo_ref[0] = (go_ref[0].astype(F32) * (yo + zo_ref[0].astype(F32) * fb)).astype(BF16)


def _long_conv_gate(consts, planes, order, z_e, z_o, gate_e, gate_o, filt_bias):
    cos, fwd_i, inv_r, inv_i, twiddle = consts
    (ze_arr, ze_col), (zo_arr, zo_col) = z_e, z_o
    (ge_arr, ge_col), (go_arr, go_col) = gate_e, gate_o
    batch, half, _ = ze_arr.shape
    d = D_MODEL
    tf, tn = DFT_TILE, DFT_COLS
    nj = d // tn
    plane_spec = pl.BlockSpec((tf, tn), lambda i, j, b: (i, order * nj + j))
    mat_f = pl.BlockSpec((tf, half), lambda i, j, b: (i, 0))
    cols_f = lambda off: pl.BlockSpec((1, half, tn), lambda i, j, b: (b, 0, off + j))
    tile_f = pl.BlockSpec((1, tf, tn), lambda i, j, b: (b, i, j))
    v_shape = jax.ShapeDtypeStruct((batch, half, d), BF16)
    v = pl.pallas_call(
        _dft_fwd_kernel,
        grid=(half // tf, nj, batch),
        in_specs=[mat_f, mat_f, cols_f(ze_col), cols_f(zo_col), pl.BlockSpec((tf, LANES), lambda i, j, b: (i, 0)),
                  plane_spec, plane_spec, plane_spec, plane_spec],
        out_specs=[tile_f] * 4,
        out_shape=[v_shape] * 4,
        compiler_params=_cparams(("parallel", "parallel", "arbitrary")),
        name="dft_fwd",
    )(cos, fwd_i, ze_arr, zo_arr, twiddle, *planes)
    mat_i = pl.BlockSpec((tf, half), lambda b, j, i: (i, 0))
    cols_i = pl.BlockSpec((1, half, tn), lambda b, j, i: (b, 0, j))
    tile_i = lambda off: pl.BlockSpec((1, tf, tn), lambda b, j, i: (b, i, off + j))
    return pl.pallas_call(
        _dft_inv_kernel,
        grid=(batch, nj, half // tf),
        in_specs=[mat_i, mat_i, cols_i, cols_i, cols_i, cols_i, tile_i(ze_col), tile_i(zo_col),
                  tile_i(ge_col), tile_i(go_col), pl.BlockSpec((1, tn), lambda b, j, i: (0, j))],
        out_specs=[tile_i(0), tile_i(0)],
        out_shape=[v_shape, v_shape],
        compiler_params=_cparams(("parallel", "parallel", "arbitrary")),
        name="dft_inv",
    )(inv_r, inv_i, *v, ze_arr, zo_arr, ge_arr, go_arr, _row2(filt_bias))


def _proj_ln_pair_kernel(ae_ref, ao_ref, w_ref, bias_ref, x_ref, g_ref, b_ref, o_ref, mix_ref):
    w, bias = w_ref[...], bias_ref[...]
    half = ae_ref.shape[0]
    for parity, a_ref in ((0, ae_ref), (1, ao_ref)):
        h = jnp.dot(a_ref[...], w, preferred_element_type=F32) + bias
        for c in range(h.shape[1] // LANES):
            mix_ref[c, pl.ds(parity, half, stride=2), :] = h[:, c * LANES:(c + 1) * LANES]
    h = jnp.concatenate([mix_ref[c] for c in range(mix_ref.shape[0])], axis=1)
    o_ref[...] = _ln_rows(ALPHA * x_ref[...] + h, g_ref[...], b_ref[...])


def _proj_ln_pair(a_e, a_o, w, bias, x, g, b):
    n, d = x.shape
    rows = ROW_TILE // 2
    row = lambda i: (i, 0)
    fix = lambda i: (0, 0)
    return pl.pallas_call(
        _proj_ln_pair_kernel,
        grid=(n // ROW_TILE,),
        in_specs=[pl.BlockSpec((rows, d), row), pl.BlockSpec((rows, d), row), pl.BlockSpec(w.shape, fix),
                  pl.BlockSpec((1, d), fix), pl.BlockSpec((ROW_TILE, d), row),
                  pl.BlockSpec((1, d), fix), pl.BlockSpec((1, d), fix)],
        out_specs=pl.BlockSpec((ROW_TILE, d), row),
        out_shape=jax.ShapeDtypeStruct((n, d), F32),
        scratch_shapes=[pltpu.VMEM((d // LANES, ROW_TILE, LANES), F32)],
        compiler_params=_cparams(("parallel",)),
        name="proj_ln_pair",
    )(a_e, a_o, w, _row2(bias), x, _row2(g), _row2(b))


def _hyena_mixer_ln(x, batch, w_in, b_in, conv_w, conv_b, ffn_w1, ffn_b1, sin_freq, ffn_w2, ffn_b2,
                    ffn_w3, filt_bias, w_out, b_out, ln_g, ln_b):
    n, d = x.shape
    seq = n // batch
    half = seq // 2
    nj = d // DFT_COLS
    planes = _hy_filters(seq, _dft_consts(seq), ffn_w1, ffn_b1, sin_freq, ffn_w2, ffn_b2, ffn_w3)
    cos, fwd_i = _dft_consts(half)
    scale = 1.0 / seq
    k = jnp.arange(half, dtype=F32)[:, None] * (math.pi / seq)
    twiddle = jnp.pad(jnp.concatenate([jnp.cos(k), jnp.sin(k)], axis=1), ((0, 0), (0, LANES - 2)))
    consts = (cos, fwd_i, cos * scale, fwd_i.T * scale, twiddle)
    x_e, x_o = _split_even_odd(x)
    u_e, u_o = _hy_in(x_e, x_o, batch, w_in, b_in, conv_w, conv_b)
    u_e, u_o = u_e.reshape(batch, half, 3 * d), u_o.reshape(batch, half, 3 * d)
    y_e, y_o = _long_conv_gate(consts, planes, 0, (u_e, 0), (u_o, 0), (u_e, nj), (u_o, nj), filt_bias[0])
    y_e, y_o = _long_conv_gate(consts, planes, 1, (y_e, 0), (y_o, 0), (u_e, 2 * nj), (u_o, 2 * nj), filt_bias[1])
    return _proj_ln_pair(y_e.reshape(n // 2, d), y_o.reshape(n // 2, d), w_out.astype(BF16), b_out, x,
                         ln_g, ln_b)


def _na_in_kernel(x_ref, w_ref, b_ref, q_ref, k_ref, v_ref):
    d = D_MODEL
    p = jnp.dot(x_ref[...].astype(BF16), w_ref[...], preferred_element_type=F32) + b_ref[...]
    q_ref[...] = (p[:, :d] * (NA_HEAD_DIM ** -0.5 * LOG2E)).astype(BF16)
    k_ref[...] = p[:, d:2 * d].astype(BF16)
    v_ref[...] = p[:, 2 * d:].astype(BF16)


def _na_attn_kernel(q_ref, k_ref, vt_ref, bias_ref, o_ref):
    for i in range(NA_PAIRS):
        rp = pl.program_id(2) * NA_PAIRS + i
        o_ref[i * LANES:(i + 1) * LANES] = _na_pair(rp, q_ref[i * LANES:(i + 1) * LANES], k_ref, vt_ref, bias_ref)


def _na_pair(rp, q, k_ref, vt_ref, bias_ref):
    hpg = LANES // NA_HEAD_DIM
    n_rows = GRID_W
    r0 = 2 * rp
    rs0 = jnp.clip(r0 - NA_WIN_ROWS // 2, 0, n_rows - NA_WIN_ROWS)
    rs1 = jnp.clip(r0 + 1 - NA_WIN_ROWS // 2, 0, n_rows - NA_WIN_ROWS)
    start = jnp.minimum(rs0, n_rows - NA_KEY_ROWS)
    tok0 = pl.multiple_of(start * GRID_W, LANES)
    nkeys = NA_KEY_ROWS * GRID_W
    kw = k_ref[pl.ds(tok0, nkeys), :]
    vtw = vt_ref[0, :, pl.ds(tok0, nkeys)]
    lane = lax.broadcasted_iota(I32, (1, LANES), 1)
    sub = lax.broadcasted_iota(I32, (LANES, 1), 0)
    right = lane >= GRID_W
    acc = jnp.zeros((LANES, LANES), F32)
    for h in range(hpg):
        in_head = (lane >= h * NA_HEAD_DIM) & (lane < (h + 1) * NA_HEAD_DIM)
        qh = jnp.where(in_head, q, jnp.zeros_like(q))
        s = lax.dot_general(kw, qh, NT_DIMS, preferred_element_type=F32)
        tiles = []
        for a in range(NA_KEY_ROWS):
            kr = start + a
            dr0 = kr - r0 + (NA_WIN_ROWS - 1)
            bt = bias_ref[h, jnp.clip(dr0, 0, 2 * NA_WIN_ROWS - 1)]
            if a < 2 or a >= NA_WIN_ROWS:
                pen0 = jnp.where((kr >= rs0) & (kr < rs0 + NA_WIN_ROWS), 0.0, NEG)
                pen1 = jnp.where((kr >= rs1) & (kr < rs1 + NA_WIN_ROWS), 0.0, NEG)
                bt = bt + jnp.where(right, pen1, pen0)
            tiles.append(bt)
        s = s + jnp.concatenate(tiles, axis=0)
        m = jnp.max(s, axis=0, keepdims=True)
        p = jnp.exp2(s - m)
        inv = 1.0 / jnp.sum(p, axis=0, keepdims=True)
        in_head_rows = (sub >= h * NA_HEAD_DIM) & (sub < (h + 1) * NA_HEAD_DIM)
        pv = jnp.dot(vtw, p.astype(BF16), preferred_element_type=F32)
        acc = acc + jnp.where(in_head_rows, pv * inv, 0.0)
    return acc.T.astype(BF16)


def _na_bias_table(rpb):
    nh = rpb.shape[0]
    kc = jnp.arange(GRID_W)[:, None]
    qc = jnp.arange(GRID_W)[None, :]
    cs = jnp.clip(qc - NA_WIN_COLS // 2, 0, GRID_W - NA_WIN_COLS)
    inwin = (kc >= cs) & (kc < cs + NA_WIN_COLS)
    dc = jnp.clip(kc - qc + (NA_WIN_COLS - 1), 0, 2 * NA_WIN_COLS - 2)
    colb = jnp.where(inwin[None, None], rpb.astype(F32)[:, :, dc] * LOG2E, NEG)
    pad = jnp.full((nh, 1, GRID_W, GRID_W), NEG, F32)
    left = jnp.concatenate([colb, pad], axis=1)
    rgt = jnp.concatenate([pad, colb], axis=1)
    return jnp.concatenate([left, rgt], axis=-1)


def _na_mixer_ln(x, batch, w_in, b_in, rpb, w_out, b_out, ln_g, ln_b):
    n, d = x.shape
    seq = n // batch
    assert seq == GRID_W * GRID_W
    row = lambda i: (i, 0)
    fix = lambda i: (0, 0)
    qkv_shape = jax.ShapeDtypeStruct((n, d), BF16)
    q, k, v = pl.pallas_call(
        _na_in_kernel,
        grid=(n // ROW_TILE,),
        in_specs=[pl.BlockSpec((ROW_TILE, d), row), pl.BlockSpec((d, 3 * d), fix), pl.BlockSpec((1, 3 * d), fix)],
        out_specs=[pl.BlockSpec((ROW_TILE, d), row)] * 3,
        out_shape=[qkv_shape] * 3,
        compiler_params=_cparams(("parallel",)),
        name="na_in",
    )(x, w_in.astype(BF16), _row2(b_in))
    vt = jnp.swapaxes(v.reshape(batch, seq, d), 1, 2)
    bias = _na_bias_table(rpb)
    hpg = LANES // NA_HEAD_DIM
    n_pairs = GRID_W // 2
    n_steps = n_pairs // NA_PAIRS
    o = pl.pallas_call(
        _na_attn_kernel,
        grid=(batch, d // LANES, n_steps),
        in_specs=[pl.BlockSpec((NA_PAIRS * LANES, LANES), lambda b, g, r: (b * n_steps + r, g)),
                  pl.BlockSpec((seq, LANES), lambda b, g, r: (b, g)),
                  pl.BlockSpec((1, LANES, seq), lambda b, g, r: (b, g, 0)),
                  pl.BlockSpec((hpg, 2 * NA_WIN_ROWS, GRID_W, LANES), lambda b, g, r: (g, 0, 0, 0))],
        out_specs=pl.BlockSpec((NA_PAIRS * LANES, LANES), lambda b, g, r: (b * n_steps + r, g)),
        out_shape=qkv_shape,
        compiler_params=_cparams(("parallel", "parallel", "arbitrary")),
        name="na_attn",
    )(q, k, vt, bias)
    return _proj_ln(o, w_out.astype(BF16), b_out, x, ln_g, ln_b)


def _moe_route_kernel(x_ref, wr_ref, rb_ref, xs_ref, pos_ref, gate_ref, cnt_ref):
    ne, tb = N_EXPERTS, MOE_TB
    xb, x_lo = _split_bf16(x_ref[...])
    nt = lambda p, q: lax.dot_general(p, q, NT_DIMS, preferred_element_type=F32)
    logits = nt(wr_ref[0], xb) + nt(wr_ref[1], xb) + nt(wr_ref[0], x_lo) + rb_ref[...]
    eio = lax.broadcasted_iota(I32, (ne, tb), 0)
    work = logits
    vals, hots = [], []
    for _ in range(TOP_K):
        m = jnp.max(work, axis=0, keepdims=True)
        idx = jnp.min(jnp.where(work == m, eio, ne), axis=0, keepdims=True)
        hot = eio == idx
        vals.append(m)
        hots.append(hot)
        work = jnp.where(hot, -jnp.inf, work)
    exps = [jnp.exp(v - vals[0]) for v in vals]
    denom = exps[0] + exps[1] + exps[2] + exps[3]
    sel = jnp.zeros((ne, tb), F32)
    for hot in hots:
        sel = sel + jnp.where(hot, 1.0, 0.0)
    earlier = (lax.broadcasted_iota(I32, (tb, tb), 0) < lax.broadcasted_iota(I32, (tb, tb), 1))
    rank = jnp.dot(sel.astype(BF16), jnp.where(earlier, 1.0, 0.0).astype(BF16),
                   preferred_element_type=F32)
    cnt = jnp.sum(sel, axis=1, keepdims=True)
    cnt_pad = jnp.ceil(cnt * (1.0 / MOE_G)) * MOE_G
    below = (lax.broadcasted_iota(I32, (ne, ne), 1) < lax.broadcasted_iota(I32, (ne, ne), 0))
    seg_off = jnp.dot(jnp.where(below, 1.0, 0.0), jnp.broadcast_to(cnt_pad, (ne, tb)),
                      precision=HIGHEST, preferred_element_type=F32)
    slot = seg_off + rank
    pos = [jnp.sum(jnp.where(hot, slot, 0.0), axis=0, keepdims=True).astype(I32) for hot in hots]
    zero_row = jnp.zeros((1, tb), I32)
    pos_ref[0] = jnp.concatenate(pos + [zero_row] * (8 - TOP_K), axis=0)
    gate_ref[0] = jnp.concatenate([e / denom for e in exps] + [zero_row.astype(F32)] * (8 - TOP_K), axis=0)
    cnt_ref[0] = jnp.broadcast_to(cnt_pad, (ne, LANES)).astype(I32)
    local = lax.broadcasted_iota(I32, (MOE_PCH, tb), 0).astype(F32).astype(BF16)
    one, zero = jnp.ones((), BF16), jnp.zeros((), BF16)
    for r in range(MOE_RB // MOE_PCH):
        onehot = None
        for p in pos:
            hit = jnp.where(local == (p - r * MOE_PCH).astype(F32).astype(BF16), one, zero)
            onehot = hit if onehot is None else onehot + hit
        xs_ref[r * MOE_PCH:(r + 1) * MOE_PCH] = jnp.dot(
            onehot, xb, preferred_element_type=F32).astype(BF16)


def _moe_expert_kernel(te_ref, src_ref, dst_ref, nt_ref, xs_hbm, wgu_ref, bgu_ref, wd_ref, bd_ref,
                       ys_hbm, xbuf, ybuf, wgu_bf, wd_bf, sem_in, sem_out):
    t = pl.program_id(0)
    nt = nt_ref[0]
    s = t % 2

    def in_copy(chunk, slot, c):
        return pltpu.make_async_copy(
            xs_hbm.at[pl.ds(pl.multiple_of(chunk * MOE_G, MOE_G), MOE_G)],
            xbuf.at[slot, pl.ds(c * MOE_G, MOE_G)], sem_in.at[slot])

    def out_copy(chunk, slot, c):
        return pltpu.make_async_copy(
            ybuf.at[slot, pl.ds(c * MOE_G, MOE_G)],
            ys_hbm.at[pl.ds(pl.multiple_of(chunk * MOE_G, MOE_G), MOE_G)], sem_out.at[slot])

    def tile_chunks(table_ref, tile):
        base = tile * MOE_CPT
        return [table_ref[base + c] for c in range(MOE_CPT)]

    def start_in(tile, slot):
        for c, chunk in enumerate(tile_chunks(src_ref, tile)):
            in_copy(chunk, slot, c).start(priority=c % 2)

    def start_out(tile, slot):
        for c, chunk in enumerate(tile_chunks(dst_ref, tile)):
            out_copy(chunk, slot, c).start(priority=c % 2)

    def wait_in(slot):
        for c in range(MOE_CPT):
            in_copy(0, slot, c).wait()

    def wait_out(slot):
        for c in range(MOE_CPT):
            out_copy(0, slot, c).wait()

    @pl.when(t == 0)
    def _():
        start_in(0, 0)

    @pl.when(t + 1 < nt)
    def _():
        start_in(t + 1, 1 - s)

    @pl.when(t < nt)
    def _():
        @pl.when(jnp.logical_or(t == 0, te_ref[t] != te_ref[jnp.maximum(t - 1, 0)]))
        def _():
            wgu_bf[...] = wgu_ref[0, 0].astype(BF16)
            wd_bf[...] = wd_ref[0, 0].astype(BF16)

        wait_in(s)
        f = wd_bf.shape[0]
        h = jnp.dot(xbuf[s], wgu_bf[...], preferred_element_type=F32) + bgu_ref[0, 0]
        glu = jnp.minimum(h[:, :f], SWIGLU_LIMIT)
        lin = jnp.clip(h[:, f:], -SWIGLU_LIMIT, SWIGLU_LIMIT)
        act = (glu * jax.nn.sigmoid(SWIGLU_ALPHA * glu) * (lin + 1.0)).astype(BF16)
        y = jnp.dot(act, wd_bf[...], preferred_element_type=F32) + bd_ref[0, 0]

        @pl.when(t >= 2)
        def _():
            wait_out(s)

        ybuf[s] = y.astype(BF16)
        start_out(t, s)

        @pl.when(t == nt - 1)
        def _():
            wait_out(s)

            @pl.when(t >= 1)
            def _():
                wait_out(1 - s)


def _moe_combine_kernel(ys_ref, pos_ref, gate_ref, x_ref, g_ref, b_ref, o_ref):
    tb = MOE_TB
    pos = pos_ref[...]
    gate = gate_ref[...].astype(BF16)
    local = lax.broadcasted_iota(I32, (tb, MOE_PCH), 1).astype(F32).astype(BF16)
    zero = jnp.zeros((), BF16)
    f = jnp.zeros((tb, D_MODEL), F32)
    for r in range(MOE_RB // MOE_PCH):
        w = None
        for k in range(TOP_K):
            pk = (pos[:, k:k + 1] - r * MOE_PCH).astype(F32).astype(BF16)
            wk = jnp.where(local == pk, gate[:, k:k + 1], zero)
            w = wk if w is None else w + wk
        f = f + jnp.dot(w, ys_ref[r * MOE_PCH:(r + 1) * MOE_PCH], preferred_element_type=F32)
    o_ref[...] = _ln_rows(ALPHA * x_ref[...] + f, g_ref[...], b_ref[...])


def _moe_tile_tables(cnt_pad, n_tiles_max):
    nb, ne = cnt_pad.shape
    nch_eb = (cnt_pad // MOE_G).T
    seg_off = jnp.cumsum(cnt_pad, axis=1) - cnt_pad
    base_eb = ((jnp.arange(nb, dtype=I32)[:, None] * MOE_RB + seg_off) // MOE_G).T
    first_eb = jnp.cumsum(nch_eb, axis=1) - nch_eb
    tiles_e = (jnp.sum(nch_eb, axis=1) + MOE_CPT - 1) // MOE_CPT
    tile_end = jnp.cumsum(tiles_e)
    n_tiles = tile_end[-1]
    tau = jnp.arange(n_tiles_max, dtype=I32)
    tile_e = jnp.minimum(jnp.sum(tau[:, None] >= tile_end[None, :], axis=1), ne - 1).astype(I32)
    of_tile = tile_e[:, None] == jnp.arange(ne, dtype=I32)[None, :]
    pick = lambda tab: jnp.sum(jnp.where(of_tile[:, :, None], tab[None], 0), axis=1)
    tile_first = tau - jnp.sum(jnp.where(of_tile, (tile_end - tiles_e)[None, :], 0), axis=1)
    k = (tile_first * MOE_CPT)[:, None] + jnp.arange(MOE_CPT, dtype=I32)[None, :]
    first, count, base = pick(first_eb), pick(nch_eb), pick(base_eb)
    inside = (k[:, :, None] >= first[:, None, :]) & (k[:, :, None] < (first + count)[:, None, :])
    src = jnp.sum(jnp.where(inside, (base - first)[:, None, :] + k[:, :, None], 0), axis=2)
    occupied = jnp.any(inside, axis=2) & (tau < n_tiles)[:, None]
    spare_block = ((tau % 2)[:, None] * MOE_CPT + jnp.arange(MOE_CPT, dtype=I32)[None, :]) % nb
    dst = jnp.where(occupied, src, spare_block * MOE_BLOCK_CHUNKS + MOE_SPARE_CHUNK)
    src = jnp.where(occupied, src, MOE_ZERO_CHUNK)
    return (tile_e, src.reshape(-1).astype(I32), dst.reshape(-1).astype(I32),
            n_tiles.astype(I32).reshape(1))


def _moe_ln(x, layer, router_w, router_b, w_gu, b_gu, w_down, b_down, ln_g, ln_b):
    n, d = x.shape
    ne, f = N_EXPERTS, w_down.shape[2]
    depth = w_gu.shape[0]
    expert_block = lambda t, te, sr, ds, nt: (layer, te[t], 0, 0)
    nb = n // MOE_TB
    assert MOE_RB >= MOE_TB * TOP_K + ne * (MOE_G - 1) + 2 * MOE_G
    blk = lambda i: (i, 0)
    blk3 = lambda i: (i, 0, 0)
    fix = lambda i: (0, 0)
    slot_rows = nb * MOE_RB
    xs, pos, gates, cnt = pl.pallas_call(
        _moe_route_kernel,
        grid=(nb,),
        in_specs=[pl.BlockSpec((MOE_TB, d), blk), pl.BlockSpec((2, ne, d), lambda i: (0, 0, 0)),
                  pl.BlockSpec((ne, 1), fix)],
        out_specs=[pl.BlockSpec((MOE_RB, d), blk), pl.BlockSpec((1, 8, MOE_TB), blk3),
                   pl.BlockSpec((1, 8, MOE_TB), blk3), pl.BlockSpec((1, ne, LANES), blk3)],
        out_shape=[jax.ShapeDtypeStruct((slot_rows, d), BF16), jax.ShapeDtypeStruct((nb, 8, MOE_TB), I32),
                   jax.ShapeDtypeStruct((nb, 8, MOE_TB), F32), jax.ShapeDtypeStruct((nb, ne, LANES), I32)],
        compiler_params=_cparams(("parallel",)),
        name="moe_route",
    )(x, jnp.stack(_split_bf16(router_w.T)), router_b.reshape(ne, 1))

    n_tiles_max = nb * MOE_RB // MOE_TM + ne
    tile_e, src, dst, n_tiles = _moe_tile_tables(cnt[:, :, 0], n_tiles_max)
    ys = pl.pallas_call(
        _moe_expert_kernel,
        grid_spec=pltpu.PrefetchScalarGridSpec(
            num_scalar_prefetch=4,
            grid=(n_tiles_max,),
            in_specs=[pl.BlockSpec(memory_space=pl.ANY),
                      pl.BlockSpec((1, 1, d, 2 * f), expert_block),
                      pl.BlockSpec((1, 1, 1, 2 * f), expert_block),
                      pl.BlockSpec((1, 1, f, d), expert_block),
                      pl.BlockSpec((1, 1, 1, d), expert_block)],
            out_specs=pl.BlockSpec(memory_space=pl.ANY),
            scratch_shapes=[pltpu.VMEM((2, MOE_TM, d), BF16), pltpu.VMEM((2, MOE_TM, d), BF16),
                            pltpu.VMEM((d, 2 * f), BF16), pltpu.VMEM((f, d), BF16),
                            pltpu.SemaphoreType.DMA((2,)), pltpu.SemaphoreType.DMA((2,))]),
        out_shape=jax.ShapeDtypeStruct((slot_rows, d), BF16),
        input_output_aliases={4: 0},
        compiler_params=pltpu.CompilerParams(dimension_semantics=("arbitrary",),
                                             vmem_limit_bytes=MOE_EXPERT_VMEM),
        name="moe_expert",
    )(tile_e, src, dst, n_tiles, xs, w_gu, b_gu.reshape(depth, ne, 1, 2 * f), w_down,
      b_down.reshape(depth, ne, 1, d))

    pos_t = jnp.swapaxes(pos, 1, 2).reshape(n, 8)
    gate_t = jnp.swapaxes(gates, 1, 2).reshape(n, 8)
    return pl.pallas_call(
        _moe_combine_kernel,
        grid=(nb,),
        in_specs=[pl.BlockSpec((MOE_RB, d), blk), pl.BlockSpec((MOE_TB, 8), blk), pl.BlockSpec((MOE_TB, 8), blk),
                  pl.BlockSpec((MOE_TB, d), blk), pl.BlockSpec((1, d), fix), pl.BlockSpec((1, d), fix)],
        out_specs=pl.BlockSpec((MOE_TB, d), blk),
        out_shape=jax.ShapeDtypeStruct((n, d), F32),
        compiler_params=_cparams(("parallel",)),
        name="moe_combine",
    )(ys, pos_t, gate_t, x, _row2(ln_g), _row2(ln_b))


def kernel(x, ln1_g, ln1_b, ln2_g, ln2_b, gla_w_in, gla_gate_w2, gla_gate_b, gla_norm_g, gla_w_out,
           hy_w_in, hy_b_in, hy_conv_w, hy_conv_b, hy_ffn_w1, hy_ffn_b1, hy_sin_freq, hy_ffn_w2,
           hy_ffn_b2, hy_ffn_w3, hy_filt_bias, hy_w_out, hy_b_out, na_w_in, na_b_in, na_rpb, na_w_out,
           na_b_out, moe_router_w, moe_router_b, moe_w_gu, moe_b_gu, moe_w_down, moe_b_down):
    batch, seq, d = x.shape
    xt = x.reshape(batch * seq, d)
    for i in range(DEPTH):
        m, j = i % N_MIXERS, i // N_MIXERS
        if m == 0:
            xt = _gla_mixer_ln(xt, batch, gla_w_in[j], gla_gate_w2[j], gla_gate_b[j], gla_norm_g[j],
                               gla_w_out[j], ln1_g[i], ln1_b[i])
        elif m == 1:
            xt = _hyena_mixer_ln(xt, batch, hy_w_in[j], hy_b_in[j], hy_conv_w[j], hy_conv_b[j],
                                 hy_ffn_w1[j], hy_ffn_b1[j], hy_sin_freq[j], hy_ffn_w2[j], hy_ffn_b2[j],
                                 hy_ffn_w3[j], hy_filt_bias[j], hy_w_out[j], hy_b_out[j],
                                 ln1_g[i], ln1_b[i])
        else:
            xt = _na_mixer_ln(xt, batch, na_w_in[j], na_b_in[j], na_rpb[j], na_w_out[j], na_b_out[j],
                              ln1_g[i], ln1_b[i])
        xt = _moe_ln(xt, i, moe_router_w[i], moe_router_b[i], moe_w_gu, moe_b_gu, moe_w_down,
                     moe_b_down, ln2_g[i], ln2_b[i])
    return xt.reshape(batch, seq, d)
```
